```python
import jax
import jax.numpy as jnp
from jax import lax
import numpy as np

D_MODEL = 1024
BATCH = 8
SEQ = 4096
DEPTH = 2
DEC_BATCH = 32
DEC_SEQ = 8
PAST_LEN = 16384
PAGE_SIZE = 128

POOL_WINDOWS = (2, 4, 8, 16)
POOL_GROUPS = 4
POOL_GROUP_DIM = D_MODEL // 8
POOL_DIM = POOL_GROUPS * POOL_GROUP_DIM
POOL_STATE = max(POOL_WINDOWS) - 1
N_HEADS = 16
HEAD_DIM = 64
KV_HEADS = 4
GROUP = N_HEADS // KV_HEADS
NSA_DIM = N_HEADS * HEAD_DIM
KV_DIM = KV_HEADS * HEAD_DIM
ROPE_DIM = HEAD_DIM // 4
ROPE_THETA = 500000.0
CMP_BLOCK = 32
CMP_STRIDE = CMP_BLOCK // 2
CMP_HIDDEN = 2 * HEAD_DIM
SLC_BLOCK = 64
SLC_TOP = 16
WINDOW = 512
Q_BLOCK = 32
CONV_DIM = D_MODEL // 2
CONV_WIDTH = 31
D_FF = ((8 * D_MODEL + 3 * 256 - 1) // (3 * 256)) * 256
S_POOL = POOL_DIM
S_Q = S_POOL + NSA_DIM
S_KV = S_Q + 6 * KV_DIM
S_GNSA = S_KV + 3 * N_HEADS
S_CONV = S_GNSA + 2 * CONV_DIM
D_IN = S_CONV + 3 * D_MODEL

kernel_name = 'hybrid_pool_nsa_conformer_decoder'


def rms_norm(x, g):
    xf = x.astype(jnp.float32)
    y = xf * lax.rsqrt(jnp.mean(xf * xf, axis=-1, keepdims=True) + 1e-6)
    return (y * g.astype(jnp.float32)).astype(x.dtype)


def rope_partial(x, pos):
    half = ROPE_DIM // 2
    inv = ROPE_THETA ** (-jnp.arange(half, dtype=jnp.float32) * (2.0 / ROPE_DIM))
    ang = pos.astype(jnp.float32)[:, None] * inv[None, :]
    cos = jnp.cos(ang)[None, :, None, :]
    sin = jnp.sin(ang)[None, :, None, :]
    xr = x[..., :ROPE_DIM].astype(jnp.float32)
    x1, x2 = xr[..., :half], xr[..., half:]
    rot = jnp.concatenate([x1 * cos - x2 * sin, x2 * cos + x1 * sin], axis=-1).astype(x.dtype)
    return jnp.concatenate([rot, x[..., ROPE_DIM:]], axis=-1)


def masked_softmax(s, mask):
    s = jnp.where(mask, s, -jnp.inf)
    m = jnp.max(s, axis=-1, keepdims=True)
    m = jnp.where(jnp.isfinite(m), m, 0.0)
    e = jnp.where(mask, jnp.exp(s - m), 0.0)
    return e / jnp.maximum(jnp.sum(e, axis=-1, keepdims=True), 1e-30)


def project_in(x, pos0, norm_g, w_in):
    B, T, _ = x.shape
    h = rms_norm(x, norm_g)
    z = h @ w_in
    u, q, kv, g_nsa, c_in, g_mrg = jnp.split(z, [S_POOL, S_Q, S_KV, S_GNSA, S_CONV], axis=-1)
    pos = pos0 + jnp.arange(T)
    q = q.reshape(B, T, N_HEADS, HEAD_DIM)
    kv = kv.reshape(B, T, 6, KV_HEADS, HEAD_DIM)
    kv_cmp = kv[:, :, 0:2]
    kv_slc = jnp.stack([rope_partial(kv[:, :, 2], pos), kv[:, :, 3]], axis=2)
    kv_win = jnp.stack([rope_partial(kv[:, :, 4], pos), kv[:, :, 5]], axis=2)
    a = c_in[..., :CONV_DIM] * jax.nn.sigmoid(c_in[..., CONV_DIM:])
    return u, q, jax.nn.sigmoid(g_nsa), kv_cmp, kv_slc, kv_win, a, jax.nn.sigmoid(g_mrg)


def pool_mix(u, prefix, pos0, w_grp, scale):
    B, T, _ = u.shape
    P = prefix.shape[1]
    full = jnp.concatenate([prefix, u], axis=1)
    cs = jnp.pad(jnp.cumsum(full.astype(jnp.float32), axis=1), ((0, 0), (1, 0), (0, 0)))
    r = P + jnp.arange(T)
    pos = pos0 + jnp.arange(T)
    hi = cs[:, r + 1]
    parts = []
    for gi, w in enumerate(POOL_WINDOWS):
        c0, c1 = gi * POOL_GROUP_DIM, (gi + 1) * POOL_GROUP_DIM
        lo = cs[:, jnp.maximum(r + 1 - w, 0), c0:c1]
        cnt = jnp.minimum(pos + 1, w).astype(jnp.float32)[None, :, None]
        parts.append((hi[..., c0:c1] - lo) / cnt)
    pooled = (jnp.concatenate(parts, axis=-1) - u.astype(jnp.float32)).astype(u.dtype)
    y = jnp.einsum('btgc,gcd->btgd', pooled.reshape(B, T, POOL_GROUPS, POOL_GROUP_DIM), w_grp)
    return y.reshape(B, T, POOL_DIM) * scale, full


def conv_mix(a, prefix, w_dw, b_dw, ln_g, ln_b):
    full = jnp.concatenate([prefix, a], axis=1)
    y = lax.conv_general_dilated(full, w_dw[:, None, :].astype(full.dtype), (1,), 'VALID',
                                 dimension_numbers=('NWC', 'WIO', 'NWC'),
                                 feature_group_count=CONV_DIM) + b_dw
    yf = y.astype(jnp.float32)
    mu = jnp.mean(yf, axis=-1, keepdims=True)
    var = jnp.mean(jnp.square(yf - mu), axis=-1, keepdims=True)
    yn = (yf - mu) * lax.rsqrt(var + 1e-5) * ln_g.astype(jnp.float32) + ln_b.astype(jnp.float32)
    return jax.nn.silu(yn).astype(a.dtype), full


def compress(kv, w1, pe, w2):
    B, L = kv.shape[:2]
    n_ch = -(-L // CMP_STRIDE)
    kv = jnp.pad(kv, ((0, 0), (0, n_ch * CMP_STRIDE - L), (0, 0), (0, 0), (0, 0)))
    ch = kv.reshape(B, n_ch, CMP_STRIDE, 2, KV_HEADS, HEAD_DIM)
    lo = jnp.einsum('bnsckd,csde->bncke', ch, w1[:, :CMP_STRIDE])
    hi = jnp.einsum('bnsckd,csde->bncke', ch, w1[:, CMP_STRIDE:])
    bias = jnp.einsum('cpd,cpde->ce', pe, w1)[None, None, :, None, :]
    hid = jax.nn.gelu(lo[:, :-1] + hi[:, 1:] + bias)
    out = jnp.einsum('bncke,ced->bnckd', hid, w2)
    return out[:, :, 0], out[:, :, 1]


def cmp_to_slc(n_cmp, n_slc):
    i0 = jnp.arange(n_cmp)[:, None] * CMP_STRIDE
    j0 = jnp.arange(n_slc)[None, :] * SLC_BLOCK
    ov = jnp.clip(jnp.minimum(i0 + CMP_BLOCK, j0 + SLC_BLOCK) - jnp.maximum(i0, j0), 0, None)
    return ov.astype(jnp.float32) / CMP_BLOCK


def nsa_attend(q, gates, pos, k_cmp, v_cmp, fetch, n_slc, kv_win, win_pos):
    B, T = q.shape[:2]
    sm = HEAD_DIM ** -0.5
    dt = v_cmp.dtype
    q_c = q.reshape(B, T, KV_HEADS, GROUP, HEAD_DIM)
    q_r = rope_partial(q, pos).reshape(B, T, KV_HEADS, GROUP, HEAD_DIM)
    n_cmp = k_cmp.shape[1]
    cmp_end = jnp.arange(n_cmp) * CMP_STRIDE + (CMP_BLOCK - 1)
    vis = (cmp_end[None, :] <= pos[:, None])[None, :, None, None, :]
    s = jnp.einsum('btkgd,bnkd->btkgn', q_c, k_cmp).astype(jnp.float32) * sm
    p_cmp = masked_softmax(s, vis)
    o_cmp = jnp.einsum('btkgn,bnkd->btkgd', p_cmp.astype(dt), v_cmp)
    imp = jnp.einsum('btkgn,nj->btkj', p_cmp, cmp_to_slc(n_cmp, n_slc))
    blk = jnp.arange(n_slc)[None, :]
    cur = (pos // SLC_BLOCK)[:, None]
    forced = ((blk == 0) | (blk == cur) | (blk == cur - 1))[None, :, None, :]
    valid = (blk <= cur)[None, :, None, :]
    imp = jnp.where(valid, jnp.where(forced, jnp.inf, imp), -jnp.inf)
    _, idx = lax.top_k(imp, min(SLC_TOP, n_slc))
    n = idx.shape[-1]
    kv_sel = fetch(idx)
    key_pos = idx[..., None] * SLC_BLOCK + jnp.arange(SLC_BLOCK)
    m_sel = (key_pos <= pos[None, :, None, None, None]) & (idx <= cur[None, :, :, None])[..., None]
    s = jnp.einsum('btkgd,btknld->btkgnl', q_r, kv_sel[..., 0, :]).astype(jnp.float32) * sm
    p = masked_softmax(s.reshape(B, T, KV_HEADS, GROUP, n * SLC_BLOCK),
                       m_sel.reshape(B, T, KV_HEADS, 1, n * SLC_BLOCK))
    v_sel = kv_sel[..., 1, :].reshape(B, T, KV_HEADS, n * SLC_BLOCK, HEAD_DIM)
    o_sel = jnp.einsum('btkgm,btkmd->btkgd', p.astype(dt), v_sel)
    dp = pos[:, None] - win_pos[None, :]
    m_win = ((dp >= 0) & (dp < WINDOW) & (win_pos >= 0)[None, :])[None, :, None, None, :]
    s = jnp.einsum('btkgd,bskd->btkgs', q_r, kv_win[:, :, 0]).astype(jnp.float32) * sm
    p = masked_softmax(s, m_win)
    o_win = jnp.einsum('btkgs,bskd->btkgd', p.astype(dt), kv_win[:, :, 1])
    g = gates.reshape(B, T, KV_HEADS, GROUP, 3)
    o = g[..., 0:1] * o_cmp + g[..., 1:2] * o_sel + g[..., 2:3] * o_win
    return o.reshape(B, T, NSA_DIM)


def nsa_prompt(q, gates, kv_cmp, kv_slc, kv_win, w1, pe, w2):
    B, S = q.shape[:2]
    k_c, v_c = compress(kv_cmp, w1, pe, w2)
    n_slc = -(-S // SLC_BLOCK)
    blocks = jnp.pad(kv_slc, ((0, 0), (0, n_slc * SLC_BLOCK - S), (0, 0), (0, 0), (0, 0)))
    blocks = blocks.reshape(B, n_slc, SLC_BLOCK, 2, KV_HEADS, HEAD_DIM)
    bi = jnp.arange(B)[:, None, None, None]
    ki = jnp.arange(KV_HEADS)[None, None, :, None]

    def fetch(idx):
        return blocks[bi, idx, :, :, ki, :]

    win_pad = jnp.pad(kv_win, ((0, 0), (WINDOW, 0), (0, 0), (0, 0), (0, 0)))

    def one_block(t0):
        pos = t0 + jnp.arange(Q_BLOCK)
        qb = lax.dynamic_slice_in_dim(q, t0, Q_BLOCK, 1)
        gb = lax.dynamic_slice_in_dim(gates, t0, Q_BLOCK, 1)
        kvw = lax.dynamic_slice_in_dim(win_pad, t0, WINDOW + Q_BLOCK, 1)
        win_pos = t0 - WINDOW + jnp.arange(WINDOW + Q_BLOCK)
        return nsa_attend(qb, gb, pos, k_c, v_c, fetch, n_slc, kvw, win_pos)

    o = lax.map(one_block, jnp.arange(0, S, Q_BLOCK))
    return o.transpose(1, 0, 2, 3).reshape(B, S, NSA_DIM)


def nsa_sample(q, gates, kv_cmp, kv_slc, kv_win, pool_cmp, pool_slc, win_buf, page_table, w1, pe, w2):
    B, T = q.shape[:2]
    n_pages = page_table.shape[1]
    past = n_pages * PAGE_SIZE
    pos = past + jnp.arange(T)
    past_cmp = pool_cmp[page_table].reshape(B, past, 2, KV_HEADS, HEAD_DIM)
    k_c, v_c = compress(jnp.concatenate([past_cmp, kv_cmp], axis=1), w1, pe, w2)
    n_slc = -(-(past + T) // SLC_BLOCK)
    n_past_blk = past // SLC_BLOCK
    n_new_blk = n_slc - n_past_blk
    bpp = PAGE_SIZE // SLC_BLOCK
    pool_blocks = pool_slc.reshape(-1, SLC_BLOCK, 2, KV_HEADS, HEAD_DIM)
    new_blocks = jnp.pad(kv_slc, ((0, 0), (0, n_new_blk * SLC_BLOCK - T), (0, 0), (0, 0), (0, 0)))
    new_blocks = new_blocks.reshape(B, n_new_blk, SLC_BLOCK, 2, KV_HEADS, HEAD_DIM)
    bi = jnp.arange(B)[:, None, None, None]
    ki = jnp.arange(KV_HEADS)[None, None, :, None]

    def fetch(idx):
        jp = jnp.minimum(idx, n_past_blk - 1)
        phys = page_table[bi, jp // bpp] * bpp + jp % bpp
        jn = jnp.clip(idx - n_past_blk, 0, n_new_blk - 1)
        return jnp.where((idx < n_past_blk)[..., None, None, None],
                         pool_blocks[phys, :, :, ki, :], new_blocks[bi, jn, :, :, ki, :])

    wb = win_buf.shape[1]
    kv_w = jnp.concatenate([win_buf, kv_win], axis=1)
    win_pos = past - wb + jnp.arange(wb + T)
    o = nsa_attend(q, gates, pos, k_c, v_c, fetch, n_slc, kv_w, win_pos)
    return o, kv_w[:, -wb:]


def merge_and_ffn(x, o_pool, o_nsa, o_conv, g_mrg, wb_pool, wb_nsa, wb_conv, w_out, norm_ffn, w_gate, w_up, w_down):
    B, T, D = x.shape
    g = g_mrg.reshape(B, T, 3, D)
    m = g[:, :, 0] * (o_pool @ wb_pool) + g[:, :, 1] * (o_nsa @ wb_nsa) + g[:, :, 2] * (o_conv @ wb_conv)
    x = x + m @ w_out
    h = rms_norm(x, norm_ffn)
    return x + (jax.nn.silu(h @ w_gate) * (h @ w_up)) @ w_down


def setup_inputs(seed: int = 0) -> dict:
    key = jax.random.key(seed)
    ks = iter(jax.random.split(key, 40))
    f32 = jnp.float32

    def nrm(shape, scale):
        return jax.random.normal(next(ks), shape, f32) * scale

    n_pages = PAST_LEN // PAGE_SIZE
    n_phys = (DEC_BATCH * n_pages * 5) // 4
    win_buf = min(WINDOW, PAST_LEN)
    page_table = jax.random.permutation(next(ks), n_phys)[:DEC_BATCH * n_pages]
    page_table = page_table.reshape(DEC_BATCH, n_pages).astype(jnp.int32)
    return {
        'x_prompt': nrm((BATCH, SEQ, D_MODEL), 1.0),
        'x_sample': nrm((DEC_BATCH, DEC_SEQ, D_MODEL), 1.0),
        'cache_cmp_kv': nrm((DEPTH, n_phys, PAGE_SIZE, 2, KV_HEADS, HEAD_DIM), 1.0),
        'cache_slc_kv': nrm((DEPTH, n_phys, PAGE_SIZE, 2, KV_HEADS, HEAD_DIM), 1.0),
        'state_win_kv': nrm((DEPTH, DEC_BATCH, win_buf, 2, KV_HEADS, HEAD_DIM), 1.0),
        'state_pool': nrm((DEPTH, DEC_BATCH, POOL_STATE, POOL_DIM), 1.0),
        'state_conv': nrm((DEPTH, DEC_BATCH, CONV_WIDTH - 1, CONV_DIM), 1.0),
        'page_table': page_table,
        'norm_mix': 1.0 + nrm((DEPTH, D_MODEL), 0.02),
        'w_in': nrm((DEPTH, D_MODEL, D_IN), D_MODEL ** -0.5),
        'pool_w': nrm((DEPTH, POOL_GROUPS, POOL_GROUP_DIM, POOL_GROUP_DIM), POOL_GROUP_DIM ** -0.5),
        'pool_scale': 1.0 + nrm((DEPTH, POOL_DIM), 0.02),
        'cmp_w1': nrm((DEPTH, 2, CMP_BLOCK, HEAD_DIM, CMP_HIDDEN), (CMP_BLOCK * HEAD_DIM) ** -0.5),
        'cmp_pe': nrm((DEPTH, 2, CMP_BLOCK, HEAD_DIM), 0.1),
        'cmp_w2': nrm((DEPTH, 2, CMP_HIDDEN, HEAD_DIM), CMP_HIDDEN ** -0.5),
        'conv_w': nrm((DEPTH, CONV_WIDTH, CONV_DIM), CONV_WIDTH ** -0.5),
        'conv_b': nrm((DEPTH, CONV_DIM), 0.01),
        'conv_ln_g': 1.0 + nrm((DEPTH, CONV_DIM), 0.02),
        'conv_ln_b': nrm((DEPTH, CONV_DIM), 0.01),
        'w_br_pool': nrm((DEPTH, POOL_DIM, D_MODEL), POOL_DIM ** -0.5),
        'w_br_nsa': nrm((DEPTH, NSA_DIM, D_MODEL), NSA_DIM ** -0.5),
        'w_br_conv': nrm((DEPTH, CONV_DIM, D_MODEL), CONV_DIM ** -0.5),
        'w_out': nrm((DEPTH, D_MODEL, D_MODEL), D_MODEL ** -0.5),
        'norm_ffn': 1.0 + nrm((DEPTH, D_MODEL), 0.02),
        'w_gate': nrm((DEPTH, D_MODEL, D_FF), D_MODEL ** -0.5),
        'w_up': nrm((DEPTH, D_MODEL, D_FF), D_MODEL ** -0.5),
        'w_down': nrm((DEPTH, D_FF, D_MODEL), D_FF ** -0.5),
        'norm_final': 1.0 + nrm((D_MODEL,), 0.02),
    }


def reference(x_prompt, x_sample, cache_cmp_kv, cache_slc_kv, state_win_kv, state_pool, state_conv, page_table,
              norm_mix, w_in, pool_w, pool_scale, cmp_w1, cmp_pe, cmp_w2, conv_w, conv_b, conv_ln_g, conv_ln_b,
              w_br_pool, w_br_nsa, w_br_conv, w_out, norm_ffn, w_gate, w_up, w_down, norm_final):
    xp, xs = x_prompt, x_sample
    bp, sp = xp.shape[:2]
    past = page_table.shape[1] * PAGE_SIZE
    cmp_p, cmp_s, slc_p, slc_s, win_p, win_s = [], [], [], [], [], []
    pool_p, pool_s, conv_p, conv_s = [], [], [], []
    for l in range(DEPTH):
        u, q, g_nsa, kvc, kvs, kvw, a, g_mrg = project_in(xp, 0, norm_mix[l], w_in[l])
        o_nsa = nsa_prompt(q, g_nsa, kvc, kvs, kvw, cmp_w1[l], cmp_pe[l], cmp_w2[l])
        o_pool, u_full = pool_mix(u, jnp.zeros((bp, 0, POOL_DIM), u.dtype), 0, pool_w[l], pool_scale[l])
        o_conv, a_full = conv_mix(a, jnp.zeros((bp, CONV_WIDTH - 1, CONV_DIM), a.dtype),
                                  conv_w[l], conv_b[l], conv_ln_g[l], conv_ln_b[l])
        xp = merge_and_ffn(xp, o_pool, o_nsa, o_conv, g_mrg, w_br_pool[l], w_br_nsa[l], w_br_conv[l],
                           w_out[l], norm_ffn[l], w_gate[l], w_up[l], w_down[l])
        cmp_p.append(kvc)
        slc_p.append(kvs)
        win_p.append(kvw[:, -min(WINDOW, sp):])
        pool_p.append(u_full[:, -POOL_STATE:])
        conv_p.append(a_full[:, -(CONV_WIDTH - 1):])
        u, q, g_nsa, kvc, kvs, kvw, a, g_mrg = project_in(xs, past, norm_mix[l], w_in[l])
        o_nsa, win_new = nsa_sample(q, g_nsa, kvc, kvs, kvw, cache_cmp_kv[l], cache_slc_kv[l], state_win_kv[l],
                                    page_table, cmp_w1[l], cmp_pe[l], cmp_w2[l])
        o_pool, u_full = pool_mix(u, state_pool[l], past, pool_w[l], pool_scale[l])
        o_conv, a_full = conv_mix(a, state_conv[l], conv_w[l], conv_b[l], conv_ln_g[l], conv_ln_b[l])
        xs = merge_and_ffn(xs, o_pool, o_nsa, o_conv, g_mrg, w_br_pool[l], w_br_nsa[l], w_br_conv[l],
                           w_out[l], norm_ffn[l], w_gate[l], w_up[l], w_down[l])
        cmp_s.append(kvc)
        slc_s.append(kvs)
        win_s.append(win_new)
        pool_s.append(u_full[:, -POOL_STATE:])
        conv_s.append(a_full[:, -(CONV_WIDTH - 1):])
    y_prompt = rms_norm(xp, norm_final)
    y_sample = rms_norm(xs, norm_final)
    return (y_prompt, y_sample,
            jnp.stack(cmp_p), jnp.stack(cmp_s),
            jnp.stack(slc_p), jnp.stack(slc_s),
            jnp.stack(win_p), jnp.stack(win_s),
            jnp.stack(pool_p), jnp.stack(pool_s),
            jnp.stack(conv_p), jnp.stack(conv_s))
```

```python
import functools

import jax
import jax.numpy as jnp
from jax import lax
from jax.experimental import pallas as pl
from jax.experimental.pallas import tpu as pltpu

F32 = jnp.float32
BF16 = jnp.bfloat16

HEAD_DIM = 64
N_HEADS = 16
KV_HEADS = 4
GROUP = N_HEADS // KV_HEADS
ROPE_DIM = HEAD_DIM // 4
ROPE_THETA = 500000.0
CMP_BLOCK = 32
CMP_STRIDE = 16
SLC_BLOCK = 64
SLC_TOP = 16
WINDOW = 512
PAGE_SIZE = 128
POOL_WINDOWS = (2, 4, 8, 16)
POOL_PREFIX = 16
CONV_WIDTH = 31
CONV_PREFIX = 32
SM_SCALE = HEAD_DIM ** -0.5
NEG = -1e30
LANES = 128
GATE_COLS = 128
VMEM_LIMIT = 56 * 1024 * 1024


def _cparams(sem):
    return pltpu.CompilerParams(dimension_semantics=sem, vmem_limit_bytes=VMEM_LIMIT)


def _const_spec(shape):
    n = len(shape)
    return pl.BlockSpec(shape, lambda *a: (0,) * n, pipeline_mode=pl.Buffered(1))


def _nt(a, b):
    return lax.dot_general(a, b, (((1,), (1,)), ((), ())), preferred_element_type=F32)


def _mm(a, b):
    return jnp.dot(a, b, preferred_element_type=F32)


def _rms(x, g):
    return x * lax.rsqrt(jnp.mean(x * x, axis=-1, keepdims=True) + 1e-6) * g


def _rope128(slab, c, s1, s2):
    return slab * c + pltpu.roll(slab, LANES - ROPE_DIM // 2, 1) * s1 + pltpu.roll(slab, ROPE_DIM // 2, 1) * s2


def _proj_kernel(x_ref, g_ref, w_ref, rope_ref, u_ref, q_ref, qr_ref, kvc_ref, kvs_ref, kvw_ref,
                 kvsb_ref, kvwb_ref, gn_ref, a_ref, gm_ref):
    x = x_ref[...]
    hb = _rms(x, g_ref[...]).astype(BF16)

    def mm(c0, n):
        return _mm(hb, w_ref[:, c0:c0 + n])

    c = rope_ref[:, 0:128]
    s1 = rope_ref[:, 128:256]
    s2 = rope_ref[:, 256:384]
    u_ref[...] = mm(0, 512)
    q = mm(512, 1024)
    q_ref[...] = (q * SM_SCALE).astype(BF16)
    for i in range(8):
        sl = slice(i * 128, (i + 1) * 128)
        qr_ref[:, sl] = (_rope128(q[:, sl], c, s1, s2) * SM_SCALE).astype(BF16)
    kv = mm(1536, 1536)
    kvc_ref[...] = kv[:, 0:512]
    for off, o_ref, ob_ref in ((512, kvs_ref, kvsb_ref), (1024, kvw_ref, kvwb_ref)):
        for i in range(2):
            sl = slice(i * 128, (i + 1) * 128)
            r = _rope128(kv[:, off + i * 128:off + (i + 1) * 128], c, s1, s2)
            o_ref[:, sl] = r
            ob_ref[:, sl] = r.astype(BF16)
        v = kv[:, off + 256:off + 512]
        o_ref[:, 256:512] = v
        ob_ref[:, 256:512] = v.astype(BF16)
    gz = jax.nn.sigmoid(mm(3072, GATE_COLS))
    gn_ref[0] = gz
    for k in range(1, KV_HEADS):
        gn_ref[k] = pltpu.roll(gz, LANES - 3 * GROUP * k, 1)
    ci = mm(3072 + GATE_COLS, 1024)
    a_ref[...] = ci[:, :512] * jax.nn.sigmoid(ci[:, 512:])
    for i in range(3):
        gm_ref[:, i * 1024:(i + 1) * 1024] = jax.nn.sigmoid(mm(4096 + GATE_COLS + i * 1024, 1024))


def _proj(x2, g, w, rope, tm):
    m = x2.shape[0]
    n_rope = rope.shape[0] // tm
    row = lambda i: (i, 0)
    outs = [
        ((m, 512), F32), ((m, 1024), BF16), ((m, 1024), BF16),
        ((m, 512), F32), ((m, 512), F32), ((m, 512), F32),
        ((m, 512), BF16), ((m, 512), BF16),
        ((KV_HEADS, m, 128), F32), ((m, 512), F32), ((m, 3072), F32),
    ]
    out_specs = []
    for shp, _ in outs:
        if len(shp) == 3:
            out_specs.append(pl.BlockSpec((KV_HEADS, tm, 128), lambda i: (0, i, 0)))
        else:
            out_specs.append(pl.BlockSpec((tm, shp[1]), row))
    return pl.pallas_call(
        _proj_kernel,
        grid=(m // tm,),
        in_specs=[pl.BlockSpec((tm, 1024), row), _const_spec(g.shape), _const_spec(w.shape),
                  pl.BlockSpec((tm, 384), lambda i: (i % n_rope, 0))],
        out_specs=out_specs,
        out_shape=[jax.ShapeDtypeStruct(s, d) for s, d in outs],
        compiler_params=_cparams(("parallel",)),
        name="proj",
    )(x2, g, w, rope)


def _cmp_lohi_body(x_refs, w_ref, o_ref, nch_each):
    for c in range(2):
        for pr in range(2):
            slab = c * 2 + pr
            parts = []
            for xr in x_refs:
                parts.append(jnp.concatenate(
                    [xr[pl.ds(4 * s + slab, nch_each, stride=4 * CMP_STRIDE), :] for s in range(CMP_STRIDE)], axis=1))
            lhs = parts[0] if len(parts) == 1 else jnp.concatenate(parts, axis=0)
            o_ref[:, (c * 2 + pr) * 512:(c * 2 + pr + 1) * 512] = _mm(lhs.astype(BF16), w_ref[c])


def _cmp_lohi_dense_kernel(x_ref, w_ref, o_ref, *, nch):
    _cmp_lohi_body([x_ref], w_ref, o_ref, nch)


def _cmp_lohi_paged_kernel(pt_ref, *refs, n_pages):
    del pt_ref
    _cmp_lohi_body(refs[:n_pages], refs[n_pages], refs[n_pages + 1], PAGE_SIZE // CMP_STRIDE)


def _cmp_lohi_dense(x2, w1p, ntok):
    m = x2.shape[0]
    nch = ntok // CMP_STRIDE
    return pl.pallas_call(
        functools.partial(_cmp_lohi_dense_kernel, nch=nch),
        grid=(m // ntok,),
        in_specs=[pl.BlockSpec((ntok * 4, 128), lambda i: (i, 0)), _const_spec(w1p.shape)],
        out_specs=pl.BlockSpec((nch, 2048), lambda i: (i, 0)),
        out_shape=jax.ShapeDtypeStruct((m // CMP_STRIDE, 2048), F32),
        compiler_params=_cparams(("parallel",)),
        name="cmp_lohi",
    )(x2.reshape(m * 4, 128), w1p)


def _cmp_lohi_paged(cache3, page_table, w1p, pg):
    b, n_pages = page_table.shape
    nch = pg * (PAGE_SIZE // CMP_STRIDE)

    def page_spec(i):
        return pl.BlockSpec((None, PAGE_SIZE * 4, 128), lambda bb, s, pt: (pt[bb, s * pg + i], 0, 0))

    cache3 = cache3.reshape(cache3.shape[0], PAGE_SIZE * 4, 128)
    return pl.pallas_call(
        functools.partial(_cmp_lohi_paged_kernel, n_pages=pg),
        grid_spec=pltpu.PrefetchScalarGridSpec(
            num_scalar_prefetch=1,
            grid=(b, n_pages // pg),
            in_specs=[page_spec(i) for i in range(pg)] + [_const_spec(w1p.shape)],
            out_specs=pl.BlockSpec((None, nch, 2048), lambda bb, s, pt: (bb, s, 0)),
        ),
        out_shape=jax.ShapeDtypeStruct((b, n_pages * (PAGE_SIZE // CMP_STRIDE), 2048), F32),
        compiler_params=_cparams(("parallel", "parallel")),
        name="cmp_lohi_paged",
    )(page_table, *([cache3] * pg), w1p)


def _cmp_fin_kernel(lohi_ref, nxt_ref, bias_ref, w2_ref, o_ref, *, nch):
    row = lax.broadcasted_iota(jnp.int32, (nch, 128), 0)
    for c in range(2):
        for pr in range(2):
            base = (c * 2 + pr) * 512
            hs = []
            for kk in range(2):
                lo = lohi_ref[:, base + kk * 256:base + kk * 256 + 128]
                hi = lohi_ref[:, base + kk * 256 + 128:base + kk * 256 + 256]
                nx = nxt_ref[:, base + kk * 256 + 128:base + kk * 256 + 256]
                hin = jnp.where(row == nch - 1, nx, pltpu.roll(hi, nch - 1, 0))
                hs.append(jax.nn.gelu(lo + hin + bias_ref[c:c + 1, :]))
            hid = jnp.concatenate(hs, axis=1).astype(BF16)
            o_ref[:, c * 256 + pr * 128:c * 256 + (pr + 1) * 128] = _mm(hid, w2_ref[c]).astype(BF16)


def _cmp_fin(lohi, nxt, bias, w2p):
    b, nch, _ = lohi.shape
    return pl.pallas_call(
        functools.partial(_cmp_fin_kernel, nch=nch),
        grid=(b,),
        in_specs=[pl.BlockSpec((None, nch, 2048), lambda i: (i, 0, 0)),
                  pl.BlockSpec((None, 1, 2048), lambda i: (i, 0, 0)),
                  _const_spec(bias.shape), _const_spec(w2p.shape)],
        out_specs=pl.BlockSpec((None, nch, 512), lambda i: (i, 0, 0)),
        out_shape=jax.ShapeDtypeStruct((b, nch, 512), BF16),
        compiler_params=_cparams(("parallel",)),
        name="cmp_fin",
    )(lohi, nxt, bias, w2p)


def _pad_q(qf, par, g, lane):
    slab = qf[:, (g // 2) * 128:(g // 2 + 1) * 128]
    rolled = pltpu.roll(slab, 64, 1)
    src = jnp.where(par == g % 2, slab, rolled)
    return jnp.where((lane >> 6) == par, src, 0.0)


def _stack_q(q_blk, par):
    qf = q_blk.astype(F32)
    lane = lax.broadcasted_iota(jnp.int32, (qf.shape[0], 128), 1)
    return jnp.concatenate([_pad_q(qf, par, g, lane) for g in range(GROUP)], axis=0).astype(BF16)


def _softmax_parts(parts, q4, r):
    ss = []
    for kk, _, al in parts:
        s = _nt(q4, kk).reshape(GROUP, r, kk.shape[0])
        ss.append(jnp.where(al[None], s, NEG))
    m = ss[0].max(axis=-1, keepdims=True)
    for s in ss[1:]:
        m = jnp.maximum(m, s.max(axis=-1, keepdims=True))
    l = jnp.zeros_like(m)
    acc = jnp.zeros((GROUP, r, 128), F32)
    ps = []
    for (kk, vv, al), s in zip(parts, ss):
        p = jnp.where(al[None], jnp.exp(s - m), 0.0)
        ps.append(p)
        l = l + p.sum(axis=-1, keepdims=True)
        acc = acc + _mm(p.reshape(GROUP * r, kk.shape[0]).astype(BF16), vv).reshape(GROUP, r, 128)
    inv = 1.0 / jnp.maximum(l, 1e-30)
    return acc * inv, [p * inv for p in ps]


def _online_update(m_ref, l_ref, acc_ref, s, allowed, vv, r):
    n = s.shape[-1]
    s3 = jnp.where(allowed[None], s.reshape(GROUP, r, n), NEG)
    m_old = m_ref[...]
    m_new = jnp.maximum(m_old, s3.max(axis=-1, keepdims=True))
    alpha = jnp.exp(m_old - m_new)
    p = jnp.where(allowed[None], jnp.exp(s3 - m_new), 0.0)
    l_ref[...] = alpha * l_ref[...] + p.sum(axis=-1, keepdims=True)
    acc_ref[...] = alpha * acc_ref[...] + _mm(p.reshape(GROUP * r, n).astype(BF16), vv).reshape(GROUP, r, 128)
    m_ref[...] = m_new


def _init_online(m_ref, l_ref, acc_ref):
    m_ref[...] = jnp.full(m_ref.shape, NEG, F32)
    l_ref[...] = jnp.zeros(l_ref.shape, F32)
    acc_ref[...] = jnp.zeros(acc_ref.shape, F32)


def _split3(x):
    x1 = x.astype(BF16)
    r1 = x - x1.astype(F32)
    x2 = r1.astype(BF16)
    x3 = (r1 - x2.astype(F32)).astype(BF16)
    return x1, x2, x3


def _block_rank_select(imp_ref, n_blocks, valid):
    x = imp_ref[...]
    jrow = lax.broadcasted_iota(jnp.int32, x.shape, 0)

    def body(jp, cnt):
        r = imp_ref[pl.ds(jp, 1), :]
        return cnt + jnp.where(jrow > jp, jnp.where(r >= x, 1.0, 0.0), jnp.where(r > x, 1.0, 0.0))

    cnt = lax.fori_loop(0, n_blocks, body, jnp.zeros(x.shape, F32))
    return jnp.where((cnt < SLC_TOP) & valid, 1.0, 0.0)


def _place_heads(o4, par, r):
    lane = lax.broadcasted_iota(jnp.int32, (r, 128), 1)
    outs = []
    for pr in range(2):
        lo, hi = o4[2 * pr], o4[2 * pr + 1]
        lo = jnp.where(par == 0, lo, pltpu.roll(lo, 64, 1))
        hi = jnp.where(par == 1, hi, pltpu.roll(hi, 64, 1))
        outs.append(jnp.where(lane < 64, lo, hi))
    return jnp.concatenate(outs, axis=1)


def _gate(gn, br, r):
    return jnp.stack([gn[:, br * GROUP + g:br * GROUP + g + 1] for g in range(GROUP)], axis=0)


def _attn_prompt_kernel(q_ref, qr_ref, gn_ref, kc_ref, vc_ref, ks_ref, vs_ref, kw_ref, vw_ref, ct_ref, e_ref,
                        o_ref, imp_scr, m_scr, l_scr, acc_scr, *, tq, nch, n_slc, kc_sel, kc_win):
    k = pl.program_id(1)
    t0 = pl.program_id(2) * tq
    par = k % 2
    q4 = _stack_q(q_ref[...], par)
    qr4 = _stack_q(qr_ref[...], par)

    kc = kc_ref[...]
    posq = t0 + lax.broadcasted_iota(jnp.int32, (tq, nch), 0)
    cend = lax.broadcasted_iota(jnp.int32, (tq, nch), 1) * CMP_STRIDE + (CMP_BLOCK - 1)
    o_cmp, (p_cmp,) = _softmax_parts([(kc, vc_ref[...], cend <= posq)], q4, tq)

    p_sum = p_cmp.sum(axis=0)
    ct = ct_ref[...]
    imp = sum(_nt(ct, piece) for piece in _split3(p_sum))
    nb = imp.shape[0]
    jb = lax.broadcasted_iota(jnp.int32, (nb, tq), 0)
    cur = (t0 + lax.broadcasted_iota(jnp.int32, (nb, tq), 1)) >> 6
    forced = (jb == 0) | (jb == cur) | (jb == cur - 1)
    valid = jb <= cur
    imp_scr[...] = jnp.where(valid, jnp.where(forced, jnp.inf, imp), -jnp.inf)
    sel_t = _block_rank_select(imp_scr, n_slc, valid)
    sel = sel_t.T.astype(BF16)

    _init_online(m_scr, l_scr, acc_scr)

    def sel_body(c, carry):
        k0 = pl.multiple_of(c * kc_sel, kc_sel)
        kk = ks_ref[pl.ds(k0, kc_sel), :]
        vv = vs_ref[pl.ds(k0, kc_sel), :]
        msk = _mm(sel, e_ref[c])
        kpos = k0 + lax.broadcasted_iota(jnp.int32, (tq, kc_sel), 1)
        pq = t0 + lax.broadcasted_iota(jnp.int32, (tq, kc_sel), 0)
        _online_update(m_scr, l_scr, acc_scr, _nt(qr4, kk), (msk > 0.5) & (kpos <= pq), vv, tq)
        return carry

    lax.fori_loop(0, (t0 + tq - 1) // kc_sel + 1, sel_body, 0)
    o_sel = acc_scr[...] * (1.0 / jnp.maximum(l_scr[...], 1e-30))

    _init_online(m_scr, l_scr, acc_scr)

    def win_body(c, carry):
        k0 = pl.multiple_of(c * kc_win, kc_win)
        kk = kw_ref[pl.ds(k0, kc_win), :]
        vv = vw_ref[pl.ds(k0, kc_win), :]
        kpos = k0 + lax.broadcasted_iota(jnp.int32, (tq, kc_win), 1)
        dp = t0 + lax.broadcasted_iota(jnp.int32, (tq, kc_win), 0) - kpos
        _online_update(m_scr, l_scr, acc_scr, _nt(qr4, kk), (dp >= 0) & (dp < WINDOW), vv, tq)
        return carry

    lax.fori_loop(jnp.maximum(t0 - (WINDOW - 1), 0) // kc_win, (t0 + tq - 1) // kc_win + 1, win_body, 0)
    o_win = acc_scr[...] * (1.0 / jnp.maximum(l_scr[...], 1e-30))

    gn = gn_ref[...]
    o4 = _gate(gn, 0, tq) * o_cmp + _gate(gn, 1, tq) * o_sel + _gate(gn, 2, tq) * o_win
    o_ref[...] = _place_heads(o4, par, tq).astype(BF16)


def _attn_prompt(q, qr, gn, kvc, kvs, kvw, ct, e, bsz, seq, tq, kc_sel, kc_win):
    nch = kvc.shape[1]
    nt = seq // tq
    n_slc = seq // SLC_BLOCK
    nb = ct.shape[0]
    qspec = pl.BlockSpec((tq, 256), lambda b, k, t: (b * nt + t, k))
    kslab = lambda arr_rows, v: pl.BlockSpec((arr_rows, 128), lambda b, k, t: (b, 2 * v + k // 2))
    cslab = lambda v: pl.BlockSpec((None, nch, 128), lambda b, k, t: (b, 0, 2 * v + k // 2))
    return pl.pallas_call(
        functools.partial(_attn_prompt_kernel, tq=tq, nch=nch, n_slc=n_slc, kc_sel=kc_sel, kc_win=kc_win),
        grid=(bsz, KV_HEADS, nt),
        in_specs=[qspec, qspec,
                  pl.BlockSpec((None, tq, 128), lambda b, k, t: (k, b * nt + t, 0)),
                  cslab(0), cslab(1), kslab(seq, 0), kslab(seq, 1), kslab(seq, 0), kslab(seq, 1),
                  _const_spec(ct.shape), _const_spec(e.shape)],
        out_specs=pl.BlockSpec((tq, 256), lambda b, k, t: (b * nt + t, k)),
        out_shape=jax.ShapeDtypeStruct((bsz * seq, 1024), BF16),
        scratch_shapes=[pltpu.VMEM((nb, tq), F32), pltpu.VMEM((GROUP, tq, 1), F32),
                        pltpu.VMEM((GROUP, tq, 1), F32), pltpu.VMEM((GROUP, tq, 128), F32)],
        compiler_params=_cparams(("parallel", "parallel", "arbitrary")),
        name="attn_prompt",
    )(q, qr, gn, kvc, kvc, kvs, kvs, kvw, kvw, ct, e)


def _samp_a_kernel(q_ref, qr_ref, gn_ref, kvc_ref, win_ref, kvwn_ref, ct_ref, o_ref, sel_ref, imp_scr,
                   *, t, nch, n_slc, past, wb):
    nb = ct_ref.shape[0]
    p_sums = []
    pos_c = past + lax.broadcasted_iota(jnp.int32, (t, nch), 0)
    cend = lax.broadcasted_iota(jnp.int32, (t, nch), 1) * CMP_STRIDE + (CMP_BLOCK - 1)
    tt = lax.broadcasted_iota(jnp.int32, (t, wb), 0)
    ii = lax.broadcasted_iota(jnp.int32, (t, wb), 1)
    dp_buf = wb + tt - ii
    al_buf = (dp_buf >= 0) & (dp_buf < WINDOW) & (past - wb + ii >= 0)
    nn = kvwn_ref.shape[0]
    dp_new = lax.broadcasted_iota(jnp.int32, (t, nn), 0) - lax.broadcasted_iota(jnp.int32, (t, nn), 1)
    al_new = (dp_new >= 0) & (dp_new < WINDOW)
    for k in range(KV_HEADS):
        par, pr = k % 2, k // 2
        q4 = _stack_q(q_ref[:, k * 256:(k + 1) * 256], par)
        qr4 = _stack_q(qr_ref[:, k * 256:(k + 1) * 256], par)
        ksl = slice(pr * 128, (pr + 1) * 128)
        vsl = slice(256 + pr * 128, 256 + (pr + 1) * 128)
        o_cmp, (p_cmp,) = _softmax_parts([(kvc_ref[:, ksl], kvc_ref[:, vsl], cend <= pos_c)], q4, t)
        p_sums.append(p_cmp.sum(axis=0))
        o_win, _ = _softmax_parts(
            [(win_ref[:, ksl].astype(BF16), win_ref[:, vsl].astype(BF16), al_buf),
             (kvwn_ref[:, ksl].astype(BF16), kvwn_ref[:, vsl].astype(BF16), al_new)], qr4, t)
        gn = gn_ref[k]
        o4 = _gate(gn, 0, t) * o_cmp + _gate(gn, 2, t) * o_win
        o_ref[:, k * 256:(k + 1) * 256] = _place_heads(o4, par, t)
    p_all = jnp.concatenate(p_sums + [jnp.zeros((128 - KV_HEADS * t, nch), F32)], axis=0)
    ct = ct_ref[...]
    imp = sum(_nt(ct, piece) for piece in _split3(p_all))
    jb = lax.broadcasted_iota(jnp.int32, (nb, 128), 0)
    col = lax.broadcasted_iota(jnp.int32, (nb, 128), 1)
    cur = (past + (col & (t - 1))) >> 6
    forced = (jb == 0) | (jb == cur) | (jb == cur - 1)
    valid = jb <= cur
    imp_scr[...] = jnp.where(valid, jnp.where(forced, jnp.inf, imp), -jnp.inf)
    sel_ref[...] = _block_rank_select(imp_scr, n_slc, valid)


def _samp_a(q3, qr3, gn4, kvc, win3, kvwn3, ct, past):
    b, t, _ = q3.shape
    nch = kvc.shape[1]
    nb = ct.shape[0]
    wb = win3.shape[1]
    n_slc = (past + t + SLC_BLOCK - 1) // SLC_BLOCK
    per_b = lambda r, c: pl.BlockSpec((None, r, c), lambda i: (i, 0, 0))
    return pl.pallas_call(
        functools.partial(_samp_a_kernel, t=t, nch=nch, n_slc=n_slc, past=past, wb=wb),
        grid=(b,),
        in_specs=[per_b(t, 1024), per_b(t, 1024),
                  pl.BlockSpec((KV_HEADS, None, t, 128), lambda i: (0, i, 0, 0)),
                  per_b(nch, 512), per_b(wb, 512), per_b(kvwn3.shape[1], 512), _const_spec(ct.shape)],
        out_specs=[per_b(t, 1024), per_b(nb, 128)],
        out_shape=[jax.ShapeDtypeStruct((b, t, 1024), F32), jax.ShapeDtypeStruct((b, nb, 128), F32)],
        scratch_shapes=[pltpu.VMEM((nb, 128), F32)],
        compiler_params=_cparams(("parallel",)),
        name="samp_a",
    )(q3, qr3, gn4, kvc, win3, kvwn3, ct)


def _samp_b_kernel(pt_ref, *refs, pg, t):
    del pt_ref
    pages = refs[:pg]
    qr_ref, sel_ref, seln_ref, e_ref, gn_ref, kvn_ref, op_ref, o_ref, m_scr, l_scr, acc_scr = refs[pg:]
    s = pl.program_id(1)
    ns = pl.num_programs(1)
    kc = pg * PAGE_SIZE
    nbc = sel_ref.shape[0]

    @pl.when(s == 0)
    def _():
        _init_online(m_scr, l_scr, acc_scr)

    selc = jnp.concatenate([sel_ref[...], jnp.zeros((128 - nbc, 128), F32)], axis=0).T
    msk_all = _mm(selc[:KV_HEADS * t].astype(BF16), e_ref[...])
    for k in range(KV_HEADS):
        par, pr = k % 2, k // 2
        qr4 = _stack_q(qr_ref[:, k * 256:(k + 1) * 256], par)
        kk = jnp.concatenate([p[:, pr * 128:(pr + 1) * 128] for p in pages], axis=0).astype(BF16)
        vv = jnp.concatenate([p[:, 256 + pr * 128:256 + (pr + 1) * 128] for p in pages], axis=0).astype(BF16)
        _online_update(m_scr.at[k], l_scr.at[k], acc_scr.at[k], _nt(qr4, kk), msk_all[k * t:(k + 1) * t] > 0.5, vv, t)

    @pl.when(s == ns - 1)
    def _():
        nn = kvn_ref.shape[0]
        dp = lax.broadcasted_iota(jnp.int32, (t, nn), 0) - lax.broadcasted_iota(jnp.int32, (t, nn), 1)
        seln = jnp.concatenate([seln_ref[...], jnp.zeros((128 - 8, 128), F32)], axis=0).T
        for k in range(KV_HEADS):
            par, pr = k % 2, k // 2
            qr4 = _stack_q(qr_ref[:, k * 256:(k + 1) * 256], par)
            kk = kvn_ref[:, pr * 128:(pr + 1) * 128].astype(BF16)
            vv = kvn_ref[:, 256 + pr * 128:256 + (pr + 1) * 128].astype(BF16)
            al = (dp >= 0) & (seln[k * t:(k + 1) * t, 0:1] > 0.5)
            _online_update(m_scr.at[k], l_scr.at[k], acc_scr.at[k], _nt(qr4, kk), al, vv, t)
            o_sel = acc_scr[k] * (1.0 / jnp.maximum(l_scr[k], 1e-30))
            o4 = _gate(gn_ref[k], 1, t) * o_sel
            o_ref[:, k * 256:(k + 1) * 256] = (op_ref[:, k * 256:(k + 1) * 256] + _place_heads(o4, par, t)).astype(BF16)


def _samp_b(cache3, page_table, qr3, sel, e, gn4, kvn3, o_part, pg):
    b, n_pages = page_table.shape
    t = qr3.shape[1]
    ns = n_pages // pg
    nbc = pg * (PAGE_SIZE // SLC_BLOCK)
    n_past_blk = n_pages * (PAGE_SIZE // SLC_BLOCK)

    def page_spec(i):
        return pl.BlockSpec((None, PAGE_SIZE, 512), lambda bb, s, pt: (pt[bb, s * pg + i], 0, 0))

    per_b = lambda r, c: pl.BlockSpec((None, r, c), lambda bb, s, pt: (bb, 0, 0))
    return pl.pallas_call(
        functools.partial(_samp_b_kernel, pg=pg, t=t),
        grid_spec=pltpu.PrefetchScalarGridSpec(
            num_scalar_prefetch=1,
            grid=(b, ns),
            in_specs=[page_spec(i) for i in range(pg)] + [
                per_b(t, 1024),
                pl.BlockSpec((None, nbc, 128), lambda bb, s, pt: (bb, s, 0)),
                pl.BlockSpec((None, 8, 128), lambda bb, s, pt: (bb, n_past_blk // 8, 0)),
                _const_spec(e.shape),
                pl.BlockSpec((KV_HEADS, None, t, 128), lambda bb, s, pt: (0, bb, 0, 0)),
                per_b(kvn3.shape[1], 512), per_b(t, 1024)],
            out_specs=per_b(t, 1024),
            scratch_shapes=[pltpu.VMEM((KV_HEADS, GROUP, t, 1), F32), pltpu.VMEM((KV_HEADS, GROUP, t, 1), F32),
                            pltpu.VMEM((KV_HEADS, GROUP, t, 128), F32)],
        ),
        out_shape=jax.ShapeDtypeStruct((b, t, 1024), BF16),
        compiler_params=_cparams(("parallel", "arbitrary")),
        name="samp_b",
    )(page_table, *([cache3] * pg), qr3, sel, sel, e, gn4, kvn3, o_part)


def _mixpre_kernel(u_ref, up_ref, a_ref, ap_ref, cw_ref, cb_ref, lg_ref, lb_ref, pool_ref, cact_ref,
                   u_scr, a_scr, *, tm, tiles_per_seq, pos_base, has_state):
    ti = pl.program_id(0) % tiles_per_seq
    pos0 = pos_base + ti * tm
    keep = 1.0 if has_state else jnp.where(ti == 0, 0.0, 1.0)
    u = u_ref[...]
    u_scr[0:POOL_PREFIX, :] = up_ref[...] * keep
    u_scr[POOL_PREFIX:, :] = u
    a_scr[0:CONV_PREFIX, :] = ap_ref[...] * keep
    a_scr[CONV_PREFIX:, :] = a_ref[...]
    pos = pos0 + lax.broadcasted_iota(jnp.int32, (tm, 128), 0)
    for gi, w in enumerate(POOL_WINDOWS):
        sl = slice(gi * 128, (gi + 1) * 128)
        acc = u[:, sl]
        for d in range(1, w):
            acc = acc + u_scr[pl.ds(POOL_PREFIX - d, tm), sl]
        cnt = jnp.minimum(pos + 1, w).astype(F32)
        pool_ref[:, sl] = (acc / cnt - u[:, sl]).astype(BF16)
    y = jnp.zeros((tm, 512), F32) + cb_ref[...]
    for j in range(CONV_WIDTH):
        y = y + a_scr[pl.ds(CONV_PREFIX - (CONV_WIDTH - 1) + j, tm), :] * cw_ref[j:j + 1, :]
    mu = jnp.mean(y, axis=-1, keepdims=True)
    yc = y - mu
    var = jnp.mean(yc * yc, axis=-1, keepdims=True)
    yn = yc * lax.rsqrt(var + 1e-5) * lg_ref[...] + lb_ref[...]
    cact_ref[...] = (yn * jax.nn.sigmoid(yn)).astype(BF16)


def _mixpre(u, u_state, a, a_state, cw, cb, lg, lb, tm, tiles_per_seq, pos_base):
    has_state = u_state is not None
    if has_state:
        up, ap = u_state, a_state
        n_tiles = u.shape[0]
        up_spec = pl.BlockSpec((None, POOL_PREFIX, 512), lambda i: (i, 0, 0))
        ap_spec = pl.BlockSpec((None, CONV_PREFIX, 512), lambda i: (i, 0, 0))
        row = pl.BlockSpec((None, tm, 512), lambda i: (i, 0, 0))
        out_shape = (n_tiles, tm, 512)
    else:
        up, ap = u, a
        n_tiles = u.shape[0] // tm
        up_spec = pl.BlockSpec((POOL_PREFIX, 512), lambda i: (jnp.maximum(i * (tm // POOL_PREFIX) - 1, 0), 0))
        ap_spec = pl.BlockSpec((CONV_PREFIX, 512), lambda i: (jnp.maximum(i * (tm // CONV_PREFIX) - 1, 0), 0))
        row = pl.BlockSpec((tm, 512), lambda i: (i, 0))
        out_shape = (u.shape[0], 512)
    return pl.pallas_call(
        functools.partial(_mixpre_kernel, tm=tm, tiles_per_seq=tiles_per_seq, pos_base=pos_base, has_state=has_state),
        grid=(n_tiles,),
        in_specs=[row, up_spec, row, ap_spec, _const_spec(cw.shape), _const_spec(cb.shape),
                  _const_spec(lg.shape), _const_spec(lb.shape)],
        out_specs=[row, row],
        out_shape=[jax.ShapeDtypeStruct(out_shape, BF16), jax.ShapeDtypeStruct(out_shape, BF16)],
        scratch_shapes=[pltpu.VMEM((tm + POOL_PREFIX, 512), F32), pltpu.VMEM((tm + CONV_PREFIX, 512), F32)],
        compiler_params=_cparams(("parallel",)),
        name="mixpre",
    )(u, up, a, ap, cw, cb, lg, lb)


def _dense_kernel(x_ref, pool_ref, onsa_ref, cact_ref, gm_ref, pw_ref, ps_ref, wbp_ref, wbn_ref, wbc_ref, wo_ref,
                  nf_ref, wg_ref, wu_ref, wd_ref, nfin_ref, o_ref, *, final, ff_chunk):
    op = jnp.concatenate([_mm(pool_ref[:, g * 128:(g + 1) * 128], pw_ref[g]) for g in range(4)], axis=1)
    op = (op * ps_ref[...]).astype(BF16)
    m = gm_ref[:, 0:1024] * _mm(op, wbp_ref[...])
    m = m + gm_ref[:, 1024:2048] * _mm(onsa_ref[...], wbn_ref[...])
    m = m + gm_ref[:, 2048:3072] * _mm(cact_ref[...], wbc_ref[...])
    x1 = x_ref[...] + _mm(m.astype(BF16), wo_ref[...])
    hb = _rms(x1, nf_ref[...]).astype(BF16)
    dff = wg_ref.shape[1]
    acc = x1
    for c0 in range(0, dff, ff_chunk):
        gt = _mm(hb, wg_ref[:, c0:c0 + ff_chunk])
        up = _mm(hb, wu_ref[:, c0:c0 + ff_chunk])
        acc = acc + _mm((gt * jax.nn.sigmoid(gt) * up).astype(BF16), wd_ref[c0:c0 + ff_chunk, :])
    o_ref[...] = _rms(acc, nfin_ref[...]) if final else acc


def _dense(x2, pooled, onsa, cact, gm, weights, tm, final):
    m = x2.shape[0]
    row = lambda c: pl.BlockSpec((tm, c), lambda i: (i, 0))
    dff = weights[8].shape[1]
    ff_chunk = dff // 2 if (dff // 2) % 128 == 0 else dff
    return pl.pallas_call(
        functools.partial(_dense_kernel, final=final, ff_chunk=ff_chunk),
        grid=(m // tm,),
        in_specs=[row(1024), row(512), row(1024), row(512), row(3072)] + [_const_spec(w.shape) for w in weights],
        out_specs=row(1024),
        out_shape=jax.ShapeDtypeStruct((m, 1024), F32),
        compiler_params=_cparams(("parallel",)),
        name="dense",
    )(x2, pooled, onsa, cact, gm, *weights)


def _rope_table(pos):
    half = ROPE_DIM // 2
    inv = ROPE_THETA ** (-jnp.arange(half, dtype=F32) * (2.0 / ROPE_DIM))
    ang = pos.astype(F32)[:, None] * inv[None, :]
    cos, sin = jnp.cos(ang), jnp.sin(ang)
    t = pos.shape[0]
    zeros = jnp.zeros((t, HEAD_DIM - ROPE_DIM), F32)
    zh = jnp.zeros((t, half), F32)
    c = jnp.concatenate([cos, cos, zeros + 1.0], axis=1)
    s1 = jnp.concatenate([-sin, zh, zeros], axis=1)
    s2 = jnp.concatenate([zh, sin, zeros], axis=1)
    return jnp.concatenate([c, c, s1, s1, s2, s2], axis=1)


def _cmp_to_slc_t(n_cmp_rows, nb):
    i0 = jnp.arange(n_cmp_rows)[None, :] * CMP_STRIDE
    j0 = jnp.arange(nb)[:, None] * SLC_BLOCK
    ov = jnp.clip(jnp.minimum(i0 + CMP_BLOCK, j0 + SLC_BLOCK) - jnp.maximum(i0, j0), 0, None)
    return (ov.astype(F32) / CMP_BLOCK).astype(BF16)


def _expand_matrix(n_rows, n_keys, n_chunks):
    key = jnp.arange(n_chunks * n_keys).reshape(n_chunks, 1, n_keys)
    return (jnp.arange(n_rows)[None, :, None] == key // SLC_BLOCK).astype(BF16)


def _layer_weights(l, norm_mix, w_in, pool_w, pool_scale, cmp_w1, cmp_pe, cmp_w2, conv_w, conv_b, conv_ln_g,
                   conv_ln_b, w_br_pool, w_br_nsa, w_br_conv, w_out, norm_ffn, w_gate, w_up, w_down, norm_final):
    w = w_in[l]
    s_gate = 512 + 1024 + 1536
    n_g = 3 * N_HEADS
    idx = jnp.array([s_gate + (k * GROUP + g) * 3 + br for k in range(KV_HEADS) for br in range(3) for g in range(GROUP)])
    w_p = jnp.concatenate([w[:, :s_gate], w[:, idx], jnp.zeros((w.shape[0], GATE_COLS - n_g), w.dtype),
                           w[:, s_gate + n_g:]], axis=1).astype(BF16)
    w1 = cmp_w1[l]
    eye2 = jnp.eye(2, dtype=F32)
    w1h = w1.reshape(2, 2, CMP_STRIDE, HEAD_DIM, 128)
    w1p = jnp.einsum('chsde,kq->cskdqhe', w1h, eye2).reshape(2, CMP_STRIDE * 128, 512).astype(BF16)
    w2p = jnp.einsum('ced,kq->ckeqd', cmp_w2[l], eye2).reshape(2, 256, 128).astype(BF16)
    bias = jnp.einsum('cpd,cpde->ce', cmp_pe[l], w1, precision=lax.Precision.HIGHEST)
    cw = jnp.concatenate([conv_w[l], jnp.zeros((1, conv_w.shape[2]), F32)], axis=0)
    dense_w = (pool_w[l].astype(BF16), pool_scale[l][None], w_br_pool[l].astype(BF16), w_br_nsa[l].astype(BF16),
               w_br_conv[l].astype(BF16), w_out[l].astype(BF16), norm_ffn[l][None], w_gate[l].astype(BF16),
               w_up[l].astype(BF16), w_down[l].astype(BF16), norm_final[None])
    return dict(g=norm_mix[l][None], w_p=w_p, w1p=w1p, w2p=w2p, bias=bias, cw=cw, cb=conv_b[l][None],
                lg=conv_ln_g[l][None], lb=conv_ln_b[l][None], dense=dense_w)


def kernel(x_prompt, x_sample, cache_cmp_kv, cache_slc_kv, state_win_kv, state_pool, state_conv, page_table,
           norm_mix, w_in, pool_w, pool_scale, cmp_w1, cmp_pe, cmp_w2, conv_w, conv_b, conv_ln_g, conv_ln_b,
           w_br_pool, w_br_nsa, w_br_conv, w_out, norm_ffn, w_gate, w_up, w_down, norm_final):
    bp, sp, d = x_prompt.shape
    bs, ts, _ = x_sample.shape
    depth = w_in.shape[0]
    n_pages = page_table.shape[1]
    past = n_pages * PAGE_SIZE
    n_phys = cache_cmp_kv.shape[1]
    wb = state_win_kv.shape[2]
    assert d == 1024 and sp % 512 == 0 and ts == 8 and n_pages % 4 == 0

    tm = 256
    tq = 128
    kc_sel = 512
    kc_win = 256
    pg = 16 if n_pages % 16 == 0 else 4
    mp, ms = bp * sp, bs * ts

    rope_p = _rope_table(jnp.arange(sp))
    rope_s = jnp.tile(_rope_table(past + jnp.arange(ts)), (bs, 1))
    nch_p = sp // CMP_STRIDE
    nb_p = 128
    ct_p = _cmp_to_slc_t(nch_p, nb_p)
    e_p = _expand_matrix(nb_p, kc_sel, sp // kc_sel)
    nch_s = past // CMP_STRIDE
    n_slc_s = (past + ts + SLC_BLOCK - 1) // SLC_BLOCK
    nb_s = ((n_slc_s + 7) // 8) * 8
    ct_s = _cmp_to_slc_t(nch_s, nb_s)
    e_s = _expand_matrix(128, pg * PAGE_SIZE, 1)[0]

    xp = x_prompt.reshape(mp, d)
    xs = x_sample.reshape(ms, d)
    outs = {n: [] for n in ("cmp_p", "cmp_s", "slc_p", "slc_s", "win_p", "win_s", "pool_p", "pool_s", "conv_p", "conv_s")}
    kvshape = (2, KV_HEADS, HEAD_DIM)
    for l in range(depth):
        lw = _layer_weights(l, norm_mix, w_in, pool_w, pool_scale, cmp_w1, cmp_pe, cmp_w2, conv_w, conv_b, conv_ln_g,
                            conv_ln_b, w_br_pool, w_br_nsa, w_br_conv, w_out, norm_ffn, w_gate, w_up, w_down, norm_final)
        final = l == depth - 1

        u, q, qr, kvc, kvs, kvw, kvsb, kvwb, gn, a, gm = _proj(xp, lw["g"], lw["w_p"], rope_p, tm)
        lohi = _cmp_lohi_dense(kvc, lw["w1p"], sp).reshape(bp, nch_p, 2048)
        kvcb = _cmp_fin(lohi, jnp.zeros((bp, 1, 2048), F32), lw["bias"], lw["w2p"])
        onsa = _attn_prompt(q, qr, gn, kvcb, kvsb, kvwb, ct_p, e_p, bp, sp, tq, kc_sel, kc_win)
        pooled, cact = _mixpre(u, None, a, None, lw["cw"], lw["cb"], lw["lg"], lw["lb"], tm, sp // tm, 0)
        xp = _dense(xp, pooled, onsa, cact, gm, lw["dense"], tm, final)
        outs["cmp_p"].append(kvc.reshape(bp, sp, *kvshape))
        outs["slc_p"].append(kvs.reshape(bp, sp, *kvshape))
        outs["win_p"].append(kvw.reshape(bp, sp, *kvshape)[:, -min(WINDOW, sp):])
        outs["pool_p"].append(u.reshape(bp, sp, 512)[:, -(POOL_PREFIX - 1):])
        outs["conv_p"].append(a.reshape(bp, sp, 512)[:, -(CONV_WIDTH - 1):])

        u, q, qr, kvc, kvs, kvw, kvsb, kvwb, gn, a, gm = _proj(xs, lw["g"], lw["w_p"], rope_s, ms)
        pad8 = lambda z: jnp.concatenate([z.reshape(bs, ts, 512), jnp.zeros((bs, 16 - ts, 512), F32)], axis=1)
        lohi = _cmp_lohi_paged(cache_cmp_kv[l].reshape(n_phys, PAGE_SIZE, 512), page_table, lw["w1p"], pg)
        lohi_new = _cmp_lohi_dense(pad8(kvc).reshape(bs * 16, 512), lw["w1p"], bs * 16)
        kvcb = _cmp_fin(lohi, lohi_new.reshape(bs, 1, 2048), lw["bias"], lw["w2p"])
        q3, qr3 = q.reshape(bs, ts, 1024), qr.reshape(bs, ts, 1024)
        gn4 = gn.reshape(KV_HEADS, bs, ts, 128)
        win3 = state_win_kv[l].reshape(bs, wb, 512)
        o_part, sel = _samp_a(q3, qr3, gn4, kvcb, win3, pad8(kvw), ct_s, past)
        onsa = _samp_b(cache_slc_kv[l].reshape(n_phys, PAGE_SIZE, 512), page_table, qr3, sel, e_s, gn4, pad8(kvs),
                       o_part, pg)
        u_state = jnp.concatenate([jnp.zeros((bs, 1, 512), F32), state_pool[l]], axis=1)
        a_state = jnp.concatenate([jnp.zeros((bs, 2, 512), F32), state_conv[l]], axis=1)
        pooled, cact = _mixpre(u.reshape(bs, ts, 512), u_state, a.reshape(bs, ts, 512), a_state,
                               lw["cw"], lw["cb"], lw["lg"], lw["lb"], ts, 1, past)
        xs = _dense(xs, pooled.reshape(ms, 512), onsa.reshape(ms, 1024), cact.reshape(ms, 512), gm, lw["dense"],
                    ms, final)
        outs["cmp_s"].append(kvc.reshape(bs, ts, *kvshape))
        outs["slc_s"].append(kvs.reshape(bs, ts, *kvshape))
        outs["win_s"].append(jnp.concatenate([win3, kvw.reshape(bs, ts, 512)], axis=1)[:, -wb:].reshape(bs, wb, *kvshape))
        outs["pool_s"].append(jnp.concatenate([state_pool[l], u.reshape(bs, ts, 512)], axis=1)[:, -(POOL_PREFIX - 1):])
        outs["conv_s"].append(jnp.concatenate([state_conv[l], a.reshape(bs, ts, 512)], axis=1)[:, -(CONV_WIDTH - 1):])

    st = lambda n: jnp.stack(outs[n])
    return (xp.reshape(bp, sp, d), xs.reshape(bs, ts, d), st("cmp_p"), st("cmp_s"), st("slc_p"), st("slc_s"),
            st("win_p"), st("win_s"), st("pool_p"), st("pool_s"), st("conv_p"), st("conv_s"))
```

```python
import functools

import jax
import jax.numpy as jnp
from jax import lax
from jax.experimental import pallas as pl
from jax.experimental.pallas import tpu as pltpu

F32 = jnp.float32
BF16 = jnp.bfloat16

HEAD_DIM = 64
N_HEADS = 16
KV_HEADS = 4
GROUP = N_HEADS // KV_HEADS
ROPE_DIM = HEAD_DIM // 4
ROPE_HALF = ROPE_DIM // 2
ROPE_THETA = 500000.0
CMP_BLOCK = 32
CMP_STRIDE = 16
SLC_BLOCK = 64
SLC_TOP = 16
WINDOW = 512
PAGE_SIZE = 128
POOL_WINDOWS = (2, 4, 8, 16)
POOL_PREFIX = 16
CONV_WIDTH = 31
CONV_PREFIX = 32
SM_SCALE = HEAD_DIM ** -0.5
NEG = -1e30
LANES = 128
GATE_COLS = 128
GATE_ROWS = 16
VMEM_LIMIT = 56 * 1024 * 1024


def _cparams(sem):
    return pltpu.CompilerParams(dimension_semantics=sem, vmem_limit_bytes=VMEM_LIMIT)


def _const_spec(shape):
    n = len(shape)
    return pl.BlockSpec(shape, lambda *a: (0,) * n, pipeline_mode=pl.Buffered(1))


def _nt(a, b):
    return lax.dot_general(a, b, (((1,), (1,)), ((), ())), preferred_element_type=F32)


def _mm(a, b):
    return jnp.dot(a, b, preferred_element_type=F32)


def _rms(x, g):
    return x * lax.rsqrt(jnp.mean(x * x, axis=-1, keepdims=True) + 1e-6) * g


def _rope128(slab, c, s1, s2):
    return slab * c + pltpu.roll(slab, LANES - ROPE_HALF, 1) * s1 + pltpu.roll(slab, ROPE_HALF, 1) * s2


def _proj_sample_kernel(x_ref, g_ref, w_ref, rope_ref, u_ref, q_ref, qr_ref, kvc_ref, kvs_ref, kvw_ref,
                        gn_ref, a_ref, gm_ref):
    x = x_ref[...]
    hb = _rms(x, g_ref[...]).astype(BF16)

    def mm(c0, n):
        return _mm(hb, w_ref[:, c0:c0 + n])

    c = rope_ref[:, 0:128]
    s1 = rope_ref[:, 128:256]
    s2 = rope_ref[:, 256:384]
    u_ref[...] = mm(0, 512)
    q = mm(512, 1024)
    q_ref[...] = (q * SM_SCALE).astype(BF16)
    for i in range(8):
        sl = slice(i * 128, (i + 1) * 128)
        qr_ref[:, sl] = (_rope128(q[:, sl], c, s1, s2) * SM_SCALE).astype(BF16)
    kv = mm(1536, 1536)
    kvc_ref[...] = kv[:, 0:512]
    for off, o_ref in ((512, kvs_ref), (1024, kvw_ref)):
        for i in range(2):
            o_ref[:, i * 128:(i + 1) * 128] = _rope128(kv[:, off + i * 128:off + (i + 1) * 128], c, s1, s2)
        o_ref[:, 256:512] = kv[:, off + 256:off + 512]
    gz = jax.nn.sigmoid(mm(3072, GATE_COLS))
    gn_ref[0] = gz
    for k in range(1, KV_HEADS):
        gn_ref[k] = pltpu.roll(gz, LANES - 3 * GROUP * k, 1)
    ci = mm(3072 + GATE_COLS, 1024)
    a_ref[...] = ci[:, :512] * jax.nn.sigmoid(ci[:, 512:])
    for i in range(3):
        gm_ref[:, i * 1024:(i + 1) * 1024] = jax.nn.sigmoid(mm(4096 + GATE_COLS + i * 1024, 1024))


def _proj_sample(x2, g, w, rope):
    m = x2.shape[0]
    outs = [((m, 512), F32), ((m, 1024), BF16), ((m, 1024), BF16), ((m, 512), F32), ((m, 512), F32),
            ((m, 512), F32), ((KV_HEADS, m, 128), F32), ((m, 512), F32), ((m, 3072), F32)]
    full = lambda shp: pl.BlockSpec(shp, lambda i: (0,) * len(shp))
    return pl.pallas_call(
        _proj_sample_kernel,
        grid=(1,),
        in_specs=[full(x2.shape), _const_spec(g.shape), _const_spec(w.shape), full(rope.shape)],
        out_specs=[full(s) for s, _ in outs],
        out_shape=[jax.ShapeDtypeStruct(s, d) for s, d in outs],
        compiler_params=_cparams(("arbitrary",)),
        name="proj_sample",
    )(x2, g, w, rope)


def _rope_t(z, cos, sin, n_heads):
    out = []
    for h in range(n_heads):
        x1 = z[h * 64:h * 64 + ROPE_HALF]
        x2 = z[h * 64 + ROPE_HALF:h * 64 + ROPE_DIM]
        out += [x1 * cos - x2 * sin, x2 * cos + x1 * sin, z[h * 64 + ROPE_DIM:(h + 1) * 64]]
    return jnp.concatenate(out, axis=0)


def _proj_prompt_kernel(x_ref, g_ref, w_ref, wt_ref, rope_ref, u_ref, a_ref, gm_ref, kvc_ref, ks_ref, kw_ref,
                        qt_ref, qrt_ref, kvct_ref, kvst_ref, kvwt_ref, vst_ref, vwt_ref, gt_ref):
    hb = _rms(x_ref[...], g_ref[...]).astype(BF16)

    def mm(c0, n):
        return _mm(hb, w_ref[:, c0:c0 + n])

    def mt(r0, n):
        return _nt(wt_ref[r0:r0 + n, :], hb)

    u_ref[...] = mm(0, 512)
    kvc_ref[...] = mm(512, 512)
    ci = mm(1024, 1024)
    a_ref[...] = ci[:, :512] * jax.nn.sigmoid(ci[:, 512:])
    for i in range(3):
        gm_ref[:, i * 1024:(i + 1) * 1024] = jax.nn.sigmoid(mm(2048 + i * 1024, 1024))
    cos = rope_ref[0:ROPE_HALF, :]
    sin = rope_ref[ROPE_HALF:ROPE_DIM, :]
    qt = mt(0, 1024) * SM_SCALE
    qt_ref[...] = qt.astype(BF16)
    qrt_ref[...] = _rope_t(qt, cos, sin, N_HEADS).astype(BF16)
    kvct_ref[...] = mt(1024, 512)
    for off, kvt_ref, vt_ref, k_ref in ((1536, kvst_ref, vst_ref, ks_ref), (2048, kvwt_ref, vwt_ref, kw_ref)):
        kt = _rope_t(mt(off, 256), cos, sin, KV_HEADS)
        vt = mt(off + 256, 256)
        kvt_ref[0:256, :] = kt
        kvt_ref[256:512, :] = vt
        vt_ref[...] = vt.astype(BF16)
        k_ref[...] = kt.T.astype(BF16)
    gt_ref[...] = jax.nn.sigmoid(mt(2560, KV_HEADS * GATE_ROWS))


def _proj_prompt(x2, g, w, wt, rope_t, bsz, seq, tm):
    m = x2.shape[0]
    nt = seq // tm
    row = lambda c: pl.BlockSpec((tm, c), lambda i: (i, 0))
    fm = lambda r: pl.BlockSpec((r, tm), lambda i: (i // nt, i % nt))
    outs = [((m, 512), F32, row(512)), ((m, 512), F32, row(512)), ((m, 3072), F32, row(3072)),
            ((m, 512), F32, row(512)), ((m, 256), BF16, row(256)), ((m, 256), BF16, row(256)),
            ((bsz * 1024, seq), BF16, fm(1024)), ((bsz * 1024, seq), BF16, fm(1024)),
            ((bsz * 512, seq), F32, fm(512)), ((bsz * 512, seq), F32, fm(512)), ((bsz * 512, seq), F32, fm(512)),
            ((bsz * 256, seq), BF16, fm(256)), ((bsz * 256, seq), BF16, fm(256)),
            ((bsz * KV_HEADS * GATE_ROWS, seq), F32, fm(KV_HEADS * GATE_ROWS))]
    return pl.pallas_call(
        _proj_prompt_kernel,
        grid=(m // tm,),
        in_specs=[row(1024), _const_spec(g.shape), _const_spec(w.shape), _const_spec(wt.shape),
                  pl.BlockSpec((ROPE_DIM, tm), lambda i: (0, i % nt))],
        out_specs=[o[2] for o in outs],
        out_shape=[jax.ShapeDtypeStruct(o[0], o[1]) for o in outs],
        compiler_params=_cparams(("parallel",)),
        name="proj_prompt",
    )(x2, g, w, wt, rope_t)


def _cmp_lohi_matmuls(load, w_ref, o_ref):
    for c in range(2):
        for pr in range(2):
            slab = c * 2 + pr
            lhs = jnp.concatenate([load(slab, s) for s in range(CMP_STRIDE)], axis=1)
            o_ref[:, slab * 512:(slab + 1) * 512] = _mm(lhs.astype(BF16), w_ref[c])


def _cmp_lohi_dense_kernel(x_ref, w_ref, o_ref, *, nch):
    _cmp_lohi_matmuls(lambda slab, s: x_ref[pl.ds(4 * s + slab, nch, stride=4 * CMP_STRIDE), :], w_ref, o_ref)


def _cmp_lohi_dense(x2, w1p, ntok):
    m = x2.shape[0]
    nch = ntok // CMP_STRIDE
    return pl.pallas_call(
        functools.partial(_cmp_lohi_dense_kernel, nch=nch),
        grid=(m // ntok,),
        in_specs=[pl.BlockSpec((ntok * 4, 128), lambda i: (i, 0)), _const_spec(w1p.shape)],
        out_specs=pl.BlockSpec((nch, 2048), lambda i: (i, 0)),
        out_shape=jax.ShapeDtypeStruct((m // CMP_STRIDE, 2048), F32),
        compiler_params=_cparams(("parallel",)),
        name="cmp_lohi",
    )(x2.reshape(m * 4, 128), w1p)


def _cmp_lohi_paged_kernel(pt_ref, *refs, n_pages):
    del pt_ref
    pages = refs[:n_pages]
    w_ref, o_ref, x_scr = refs[n_pages:]
    for i, p in enumerate(pages):
        for c in range(2):
            xt = p[c].reshape(KV_HEADS * HEAD_DIM, PAGE_SIZE).T
            for pr in range(2):
                x_scr[c * 2 + pr, i * PAGE_SIZE:(i + 1) * PAGE_SIZE, :] = xt[:, pr * 128:(pr + 1) * 128]
    nch = n_pages * (PAGE_SIZE // CMP_STRIDE)
    _cmp_lohi_matmuls(lambda slab, s: x_scr[slab, pl.ds(s, nch, stride=CMP_STRIDE), :], w_ref, o_ref)


def _cmp_lohi_paged(cache_t, layer, page_table, w1p, pg):
    b, n_pages = page_table.shape
    nch = pg * (PAGE_SIZE // CMP_STRIDE)

    def page_spec(i):
        return pl.BlockSpec((None, None, 2, KV_HEADS, HEAD_DIM, PAGE_SIZE),
                            lambda bb, s, pt: (layer, pt[bb, s * pg + i], 0, 0, 0, 0))

    return pl.pallas_call(
        functools.partial(_cmp_lohi_paged_kernel, n_pages=pg),
        grid_spec=pltpu.PrefetchScalarGridSpec(
            num_scalar_prefetch=1,
            grid=(b, n_pages // pg),
            in_specs=[page_spec(i) for i in range(pg)] + [_const_spec(w1p.shape)],
            out_specs=pl.BlockSpec((None, nch, 2048), lambda bb, s, pt: (bb, s, 0)),
            scratch_shapes=[pltpu.VMEM((4, pg * PAGE_SIZE, 128), F32)],
        ),
        out_shape=jax.ShapeDtypeStruct((b, n_pages * (PAGE_SIZE // CMP_STRIDE), 2048), F32),
        compiler_params=_cparams(("parallel", "parallel")),
        name="cmp_lohi_paged",
    )(page_table, *([cache_t] * pg), w1p)


def _cmp_fin_kernel(lohi_ref, nxt_ref, bias_ref, w2_ref, o_ref, vt_ref, *, nch):
    row = lax.broadcasted_iota(jnp.int32, (nch, 128), 0)
    for c in range(2):
        for pr in range(2):
            base = (c * 2 + pr) * 512
            hs = []
            for kk in range(2):
                lo = lohi_ref[:, base + kk * 256:base + kk * 256 + 128]
                hi = lohi_ref[:, base + kk * 256 + 128:base + kk * 256 + 256]
                nx = nxt_ref[:, base + kk * 256 + 128:base + kk * 256 + 256]
                hin = jnp.where(row == nch - 1, nx, pltpu.roll(hi, nch - 1, 0))
                hs.append(jax.nn.gelu(lo + hin + bias_ref[c:c + 1, :]))
            hid = jnp.concatenate(hs, axis=1).astype(BF16)
            out = _mm(hid, w2_ref[c])
            o_ref[:, c * 256 + pr * 128:c * 256 + (pr + 1) * 128] = out.astype(BF16)
            if c == 1:
                vt_ref[pr * 128:(pr + 1) * 128, :] = out.T.astype(BF16)


def _cmp_fin(lohi, nxt, bias, w2p):
    b, nch, _ = lohi.shape
    return pl.pallas_call(
        functools.partial(_cmp_fin_kernel, nch=nch),
        grid=(b,),
        in_specs=[pl.BlockSpec((None, nch, 2048), lambda i: (i, 0, 0)),
                  pl.BlockSpec((None, 1, 2048), lambda i: (i, 0, 0)),
                  _const_spec(bias.shape), _const_spec(w2p.shape)],
        out_specs=[pl.BlockSpec((None, nch, 512), lambda i: (i, 0, 0)),
                   pl.BlockSpec((None, 256, nch), lambda i: (i, 0, 0))],
        out_shape=[jax.ShapeDtypeStruct((b, nch, 512), BF16), jax.ShapeDtypeStruct((b, 256, nch), BF16)],
        compiler_params=_cparams(("parallel",)),
        name="cmp_fin",
    )(lohi, nxt, bias, w2p)


def _split3(x):
    x1 = x.astype(BF16)
    r1 = x - x1.astype(F32)
    x2 = r1.astype(BF16)
    x3 = (r1 - x2.astype(F32)).astype(BF16)
    return x1, x2, x3


def _block_rank_select(imp_ref, n_blocks, valid):
    x = imp_ref[...]
    jrow = lax.broadcasted_iota(jnp.int32, x.shape, 0)

    def body(jp, cnt):
        r = imp_ref[pl.ds(jp, 1), :]
        return cnt + jnp.where(jrow > jp, jnp.where(r >= x, 1.0, 0.0), jnp.where(r > x, 1.0, 0.0))

    cnt = lax.fori_loop(0, n_blocks, body, jnp.zeros(x.shape, F32))
    return jnp.where((cnt < SLC_TOP) & valid, 1.0, 0.0)


def _softmax_t(s_t, allowed, v_t, tq):
    es, invs = [], []
    p_sum = None
    for g in range(GROUP):
        s = jnp.where(allowed, s_t[:, g * tq:(g + 1) * tq], NEG)
        m = s.max(axis=0, keepdims=True)
        e = jnp.where(allowed, jnp.exp(s - m), 0.0)
        inv = 1.0 / jnp.maximum(e.sum(axis=0, keepdims=True), 1e-30)
        p_sum = e * inv if p_sum is None else p_sum + e * inv
        es.append(e.astype(BF16))
        invs.append(inv)
    o = _mm(v_t, jnp.concatenate(es, axis=1)) * jnp.concatenate(invs, axis=1)
    return o, p_sum


def _online_t(m_ref, l_ref, acc_ref, s_t, allowed, v_t, tq):
    ps, alphas = [], []
    for g in range(GROUP):
        sl = slice(g * tq, (g + 1) * tq)
        s = jnp.where(allowed, s_t[:, sl], NEG)
        m_old = m_ref[:, sl]
        m_new = jnp.maximum(m_old, s.max(axis=0, keepdims=True))
        alpha = jnp.exp(m_old - m_new)
        p = jnp.exp(s - m_new)
        l_ref[:, sl] = alpha * l_ref[:, sl] + p.sum(axis=0, keepdims=True)
        m_ref[:, sl] = m_new
        ps.append(p.astype(BF16))
        alphas.append(alpha)
    acc_ref[...] = acc_ref[...] * jnp.concatenate(alphas, axis=1) + _mm(v_t, jnp.concatenate(ps, axis=1))


def _attn_prompt_kernel(qt_ref, qrt_ref, gt_ref, kc_ref, vct_ref, ks_ref, vst_ref, kw_ref, vwt_ref, ct_ref,
                        o_ref, imp_scr, lim_scr, m_scr, l_scr, acc_scr, *, tq, nch, n_slc, kc_sel):
    k = pl.program_id(1)
    t0 = pl.program_id(2) * tq
    par = k % 2
    r = GROUP * tq

    def stack(ref):
        z = jnp.zeros((HEAD_DIM, tq), BF16)
        cols = []
        for g in range(GROUP):
            qg = ref[g * HEAD_DIM:(g + 1) * HEAD_DIM, :]
            cols.append(jnp.where(par == 0, jnp.concatenate([qg, z], axis=0), jnp.concatenate([z, qg], axis=0)))
        return jnp.concatenate(cols, axis=1)

    q4 = stack(qt_ref)
    qr4 = stack(qrt_ref)
    pos = t0 + lax.broadcasted_iota(jnp.int32, (1, tq), 1)

    cend = lax.broadcasted_iota(jnp.int32, (nch, tq), 0) * CMP_STRIDE + (CMP_BLOCK - 1)
    o_cmp, p_sum = _softmax_t(_mm(kc_ref[...], q4), cend <= pos, vct_ref[...], tq)

    ct = ct_ref[...]
    imp = sum(_mm(ct, piece) for piece in _split3(p_sum))
    nb = imp.shape[0]
    jb = lax.broadcasted_iota(jnp.int32, (nb, tq), 0)
    cur = pos >> 6
    forced = (jb == 0) | (jb == cur) | (jb == cur - 1)
    valid = jb <= cur
    posb = jnp.broadcast_to(pos, (nb, tq))
    imp_scr[...] = jnp.where(valid, jnp.where(forced, jnp.inf, imp), -jnp.inf)

    @pl.when(t0 + tq <= SLC_TOP * SLC_BLOCK)
    def _():
        lim_scr[...] = jnp.where(valid, posb, -1)

    @pl.when(t0 + tq > SLC_TOP * SLC_BLOCK)
    def _():
        sel = _block_rank_select(imp_scr, n_slc, valid)
        lim_scr[...] = jnp.where(sel > 0.5, posb, -1)

    m_scr[...] = jnp.full((1, r), NEG, F32)
    l_scr[...] = jnp.zeros((1, r), F32)
    acc_scr[...] = jnp.zeros((128, r), F32)
    nblk = kc_sel // SLC_BLOCK
    kiota = lax.broadcasted_iota(jnp.int32, (kc_sel, tq), 0)

    def sel_body(c, carry):
        k0 = pl.multiple_of(c * kc_sel, kc_sel)
        limb = jnp.concatenate(
            [jnp.broadcast_to(lim_scr[pl.ds(c * nblk + jj, 1), :], (SLC_BLOCK, tq)) for jj in range(nblk)], axis=0)
        s_t = _mm(ks_ref[pl.ds(k0, kc_sel), :], qr4)
        _online_t(m_scr, l_scr, acc_scr, s_t, (k0 + kiota) <= limb, vst_ref[:, pl.ds(k0, kc_sel)], tq)
        return carry

    lax.fori_loop(0, (t0 + tq - 1) // kc_sel + 1, sel_body, 0)
    o_sel = acc_scr[...] * (1.0 / jnp.maximum(l_scr[...], 1e-30))

    wk = WINDOW + tq
    k0 = pl.multiple_of(jnp.maximum(t0 - WINDOW, 0), tq)
    dp = pos - (k0 + lax.broadcasted_iota(jnp.int32, (wk, tq), 0))
    o_win, _ = _softmax_t(_mm(kw_ref[pl.ds(k0, wk), :], qr4), (dp >= 0) & (dp < WINDOW),
                          vwt_ref[:, pl.ds(k0, wk)], tq)

    gt = gt_ref[...]
    for g in range(GROUP):
        sl = slice(g * tq, (g + 1) * tq)
        og = gt[g:g + 1] * o_cmp[:, sl] + gt[GROUP + g:GROUP + g + 1] * o_sel[:, sl] \
            + gt[2 * GROUP + g:2 * GROUP + g + 1] * o_win[:, sl]
        o_ref[g * HEAD_DIM:(g + 1) * HEAD_DIM, :] = jnp.where(par == 0, og[0:HEAD_DIM], og[HEAD_DIM:]).astype(BF16)


def _attn_prompt(qt, qrt, gt, kvc, vct, ks, vst, kw, vwt, ct, bsz, seq, tq, kc_sel):
    nch = kvc.shape[1]
    nt = seq // tq
    n_slc = seq // SLC_BLOCK
    nb = ct.shape[0]
    r = GROUP * tq
    qspec = pl.BlockSpec((256, tq), lambda b, k, t: (b * KV_HEADS + k, t))
    kspec = pl.BlockSpec((seq, 128), lambda b, k, t: (b, k // 2))
    vspec = pl.BlockSpec((128, seq), lambda b, k, t: (b * 2 + k // 2, 0))
    return pl.pallas_call(
        functools.partial(_attn_prompt_kernel, tq=tq, nch=nch, n_slc=n_slc, kc_sel=kc_sel),
        grid=(bsz, KV_HEADS, nt),
        in_specs=[qspec, qspec,
                  pl.BlockSpec((GATE_ROWS, tq), lambda b, k, t: (b * KV_HEADS + k, t)),
                  pl.BlockSpec((None, nch, 128), lambda b, k, t: (b, 0, k // 2)),
                  pl.BlockSpec((None, 128, nch), lambda b, k, t: (b, k // 2, 0)),
                  kspec, vspec, kspec, vspec, _const_spec(ct.shape)],
        out_specs=pl.BlockSpec((256, tq), lambda b, k, t: (b * KV_HEADS + k, t)),
        out_shape=jax.ShapeDtypeStruct((bsz * 1024, seq), BF16),
        scratch_shapes=[pltpu.VMEM((nb, tq), F32), pltpu.VMEM((nb, tq), jnp.int32), pltpu.VMEM((1, r), F32),
                        pltpu.VMEM((1, r), F32), pltpu.VMEM((128, r), F32)],
        compiler_params=_cparams(("parallel", "parallel", "arbitrary")),
        name="attn_prompt",
    )(qt, qrt, gt, kvc, vct, ks, vst, kw, vwt, ct)


def _pad_q(qf, par, g, lane):
    slab = qf[:, (g // 2) * 128:(g // 2 + 1) * 128]
    rolled = pltpu.roll(slab, 64, 1)
    src = slab if par == g % 2 else rolled
    return jnp.where((lane >> 6) == par, src, 0.0)


def _stack_q(q_blk, par):
    qf = q_blk.astype(F32)
    lane = lax.broadcasted_iota(jnp.int32, (qf.shape[0], 128), 1)
    return jnp.concatenate([_pad_q(qf, par, g, lane) for g in range(GROUP)], axis=0).astype(BF16)


def _softmax_parts(parts, r):
    ss = []
    for s, al, _ in parts:
        ss.append(jnp.where(al[None], s.reshape(GROUP, r, s.shape[-1]), NEG))
    m = ss[0].max(axis=-1, keepdims=True)
    for s in ss[1:]:
        m = jnp.maximum(m, s.max(axis=-1, keepdims=True))
    l = jnp.zeros_like(m)
    acc = jnp.zeros((GROUP, r, 128), F32)
    ps = []
    for (_, al, pv), s in zip(parts, ss):
        p = jnp.where(al[None], jnp.exp(s - m), 0.0)
        ps.append(p)
        l = l + p.sum(axis=-1, keepdims=True)
        acc = acc + pv(p.reshape(GROUP * r, p.shape[-1]).astype(BF16)).reshape(GROUP, r, 128)
    inv = 1.0 / jnp.maximum(l, 1e-30)
    return acc * inv, [p * inv for p in ps]


def _online_update(m_ref, l_ref, acc_ref, s, allowed, pv, r):
    n = s.shape[-1]
    s3 = jnp.where(allowed[None], s.reshape(GROUP, r, n), NEG)
    m_old = m_ref[...]
    m_new = jnp.maximum(m_old, s3.max(axis=-1, keepdims=True))
    alpha = jnp.exp(m_old - m_new)
    p = jnp.where(allowed[None], jnp.exp(s3 - m_new), 0.0)
    l_ref[...] = alpha * l_ref[...] + p.sum(axis=-1, keepdims=True)
    acc_ref[...] = alpha * acc_ref[...] + pv(p.reshape(GROUP * r, n).astype(BF16)).reshape(GROUP, r, 128)
    m_ref[...] = m_new


def _place_heads(o4, par, r):
    lane = lax.broadcasted_iota(jnp.int32, (r, 128), 1)
    outs = []
    for pr in range(2):
        lo, hi = o4[2 * pr], o4[2 * pr + 1]
        lo = lo if par == 0 else pltpu.roll(lo, 64, 1)
        hi = hi if par == 1 else pltpu.roll(hi, 64, 1)
        outs.append(jnp.where(lane < 64, lo, hi))
    return jnp.concatenate(outs, axis=1)


def _gate(gn, br, r):
    return jnp.stack([gn[:, br * GROUP + g:br * GROUP + g + 1] for g in range(GROUP)], axis=0)


def _pair_t(ref, c, pr):
    return ref[c, 2 * pr:2 * pr + 2].reshape(2 * HEAD_DIM, ref.shape[-1]).astype(BF16)


def _samp_a_kernel(q_ref, qr_ref, gn_ref, kvc_ref, win_ref, kvwn_ref, ct_ref, o_ref, sel_ref, imp_scr,
                   *, t, nch, n_slc, past, wb):
    nb = ct_ref.shape[0]
    p_sums = []
    pos_c = past + lax.broadcasted_iota(jnp.int32, (t, nch), 0)
    cend = lax.broadcasted_iota(jnp.int32, (t, nch), 1) * CMP_STRIDE + (CMP_BLOCK - 1)
    tt = lax.broadcasted_iota(jnp.int32, (t, wb), 0)
    ii = lax.broadcasted_iota(jnp.int32, (t, wb), 1)
    dp_buf = wb + tt - ii
    al_buf = (dp_buf >= 0) & (dp_buf < WINDOW) & (past - wb + ii >= 0)
    nn = kvwn_ref.shape[0]
    dp_new = lax.broadcasted_iota(jnp.int32, (t, nn), 0) - lax.broadcasted_iota(jnp.int32, (t, nn), 1)
    al_new = (dp_new >= 0) & (dp_new < WINDOW)
    for k in range(KV_HEADS):
        par, pr = k % 2, k // 2
        q4 = _stack_q(q_ref[:, k * 256:(k + 1) * 256], par)
        qr4 = _stack_q(qr_ref[:, k * 256:(k + 1) * 256], par)
        ksl = slice(pr * 128, (pr + 1) * 128)
        vsl = slice(256 + pr * 128, 256 + (pr + 1) * 128)
        kc, vc = kvc_ref[:, ksl], kvc_ref[:, vsl]
        o_cmp, (p_cmp,) = _softmax_parts([(_nt(q4, kc), cend <= pos_c, lambda p, vc=vc: _mm(p, vc))], t)
        p_sums.append(p_cmp.sum(axis=0))
        kwt, vwt = _pair_t(win_ref, 0, pr), _pair_t(win_ref, 1, pr)
        kn, vn = kvwn_ref[:, ksl].astype(BF16), kvwn_ref[:, vsl].astype(BF16)
        o_win, _ = _softmax_parts([(_mm(qr4, kwt), al_buf, lambda p, vwt=vwt: _nt(p, vwt)),
                                   (_nt(qr4, kn), al_new, lambda p, vn=vn: _mm(p, vn))], t)
        gn = gn_ref[k]
        o4 = _gate(gn, 0, t) * o_cmp + _gate(gn, 2, t) * o_win
        o_ref[:, k * 256:(k + 1) * 256] = _place_heads(o4, par, t)
    p_all = jnp.concatenate(p_sums + [jnp.zeros((128 - KV_HEADS * t, nch), F32)], axis=0)
    ct = ct_ref[...]
    imp = sum(_nt(ct, piece) for piece in _split3(p_all))
    jb = lax.broadcasted_iota(jnp.int32, (nb, 128), 0)
    col = lax.broadcasted_iota(jnp.int32, (nb, 128), 1)
    cur = (past + (col & (t - 1))) >> 6
    forced = (jb == 0) | (jb == cur) | (jb == cur - 1)
    valid = jb <= cur
    imp_scr[...] = jnp.where(valid, jnp.where(forced, jnp.inf, imp), -jnp.inf)
    sel_ref[...] = _block_rank_select(imp_scr, n_slc, valid)


def _samp_a(q3, qr3, gn4, kvc, win_t, layer, kvwn3, ct, past):
    b, t, _ = q3.shape
    nch = kvc.shape[1]
    nb = ct.shape[0]
    wb = win_t.shape[-1]
    n_slc = (past + t + SLC_BLOCK - 1) // SLC_BLOCK
    per_b = lambda r, c: pl.BlockSpec((None, r, c), lambda i: (i, 0, 0))
    return pl.pallas_call(
        functools.partial(_samp_a_kernel, t=t, nch=nch, n_slc=n_slc, past=past, wb=wb),
        grid=(b,),
        in_specs=[per_b(t, 1024), per_b(t, 1024),
                  pl.BlockSpec((KV_HEADS, None, t, 128), lambda i: (0, i, 0, 0)),
                  per_b(nch, 512),
                  pl.BlockSpec((None, None, 2, KV_HEADS, HEAD_DIM, wb), lambda i: (layer, i, 0, 0, 0, 0)),
                  per_b(kvwn3.shape[1], 512), _const_spec(ct.shape)],
        out_specs=[per_b(t, 1024), per_b(nb, 128)],
        out_shape=[jax.ShapeDtypeStruct((b, t, 1024), F32), jax.ShapeDtypeStruct((b, nb, 128), F32)],
        scratch_shapes=[pltpu.VMEM((nb, 128), F32)],
        compiler_params=_cparams(("parallel",)),
        name="samp_a",
    )(q3, qr3, gn4, kvc, win_t, kvwn3, ct)


def _samp_b_kernel(pt_ref, *refs, pg, t):
    del pt_ref
    pages = refs[:pg]
    qr_ref, sel_ref, seln_ref, e_ref, gn_ref, kvn_ref, op_ref, o_ref, m_scr, l_scr, acc_scr = refs[pg:]
    s = pl.program_id(1)
    ns = pl.num_programs(1)
    nbc = sel_ref.shape[0]

    @pl.when(s == 0)
    def _():
        m_scr[...] = jnp.full(m_scr.shape, NEG, F32)
        l_scr[...] = jnp.zeros(l_scr.shape, F32)
        acc_scr[...] = jnp.zeros(acc_scr.shape, F32)

    selc = jnp.concatenate([sel_ref[...], jnp.zeros((128 - nbc, 128), F32)], axis=0).T
    msk_all = _mm(selc[:KV_HEADS * t].astype(BF16), e_ref[...])
    for k in range(KV_HEADS):
        par, pr = k % 2, k // 2
        qr4 = _stack_q(qr_ref[:, k * 256:(k + 1) * 256], par)
        kt = jnp.concatenate([_pair_t(p, 0, pr) for p in pages], axis=1)
        vt = jnp.concatenate([_pair_t(p, 1, pr) for p in pages], axis=1)
        _online_update(m_scr.at[k], l_scr.at[k], acc_scr.at[k], _mm(qr4, kt), msk_all[k * t:(k + 1) * t] > 0.5,
                       lambda p, vt=vt: _nt(p, vt), t)

    @pl.when(s == ns - 1)
    def _():
        nn = kvn_ref.shape[0]
        dp = lax.broadcasted_iota(jnp.int32, (t, nn), 0) - lax.broadcasted_iota(jnp.int32, (t, nn), 1)
        seln = jnp.concatenate([seln_ref[...], jnp.zeros((128 - 8, 128), F32)], axis=0).T
        for k in range(KV_HEADS):
            par, pr = k % 2, k // 2
            qr4 = _stack_q(qr_ref[:, k * 256:(k + 1) * 256], par)
            kk = kvn_ref[:, pr * 128:(pr + 1) * 128].astype(BF16)
            vv = kvn_ref[:, 256 + pr * 128:256 + (pr + 1) * 128].astype(BF16)
            al = (dp >= 0) & (seln[k * t:(k + 1) * t, 0:1] > 0.5)
            _online_update(m_scr.at[k], l_scr.at[k], acc_scr.at[k], _nt(qr4, kk), al, lambda p, vv=vv: _mm(p, vv), t)
            o_sel = acc_scr[k] * (1.0 / jnp.maximum(l_scr[k], 1e-30))
            o4 = _gate(gn_ref[k], 1, t) * o_sel
            o_ref[:, k * 256:(k + 1) * 256] = (op_ref[:, k * 256:(k + 1) * 256] + _place_heads(o4, par, t)).astype(BF16)


def _samp_b(cache_t, layer, page_table, qr3, sel, e, gn4, kvn3, o_part, pg):
    b, n_pages = page_table.shape
    t = qr3.shape[1]
    ns = n_pages // pg
    nbc = pg * (PAGE_SIZE // SLC_BLOCK)
    n_past_blk = n_pages * (PAGE_SIZE // SLC_BLOCK)

    def page_spec(i):
        return pl.BlockSpec((None, None, 2, KV_HEADS, HEAD_DIM, PAGE_SIZE),
                            lambda bb, s, pt: (layer, pt[bb, s * pg + i], 0, 0, 0, 0))

    per_b = lambda r, c: pl.BlockSpec((None, r, c), lambda bb, s, pt: (bb, 0, 0))
    return pl.pallas_call(
        functools.partial(_samp_b_kernel, pg=pg, t=t),
        grid_spec=pltpu.PrefetchScalarGridSpec(
            num_scalar_prefetch=1,
            grid=(b, ns),
            in_specs=[page_spec(i) for i in range(pg)] + [
                per_b(t, 1024),
                pl.BlockSpec((None, nbc, 128), lambda bb, s, pt: (bb, s, 0)),
                pl.BlockSpec((None, 8, 128), lambda bb, s, pt: (bb, n_past_blk // 8, 0)),
                _const_spec(e.shape),
                pl.BlockSpec((KV_HEADS, None, t, 128), lambda bb, s, pt: (0, bb, 0, 0)),
                per_b(kvn3.shape[1], 512), per_b(t, 1024)],
            out_specs=per_b(t, 1024),
            scratch_shapes=[pltpu.VMEM((KV_HEADS, GROUP, t, 1), F32), pltpu.VMEM((KV_HEADS, GROUP, t, 1), F32),
                            pltpu.VMEM((KV_HEADS, GROUP, t, 128), F32)],
        ),
        out_shape=jax.ShapeDtypeStruct((b, t, 1024), BF16),
        compiler_params=_cparams(("parallel", "arbitrary")),
        name="samp_b",
    )(page_table, *([cache_t] * pg), qr3, sel, sel, e, gn4, kvn3, o_part)


def _mixpre_kernel(u_ref, up_ref, a_ref, ap_ref, cw_ref, cb_ref, lg_ref, lb_ref, pool_ref, cact_ref,
                   u_scr, a_scr, *, tm, tiles_per_seq, pos_base, has_state):
    ti = pl.program_id(0) % tiles_per_seq
    pos0 = pos_base + ti * tm
    keep = 1.0 if has_state else jnp.where(ti == 0, 0.0, 1.0)
    u = u_ref[...]
    u_scr[0:POOL_PREFIX, :] = up_ref[...] * keep
    u_scr[POOL_PREFIX:, :] = u
    a_scr[0:CONV_PREFIX, :] = ap_ref[...] * keep
    a_scr[CONV_PREFIX:, :] = a_ref[...]
    pos = pos0 + lax.broadcasted_iota(jnp.int32, (tm, 128), 0)
    for gi, w in enumerate(POOL_WINDOWS):
        sl = slice(gi * 128, (gi + 1) * 128)
        acc = u[:, sl]
        for d in range(1, w):
            acc = acc + u_scr[pl.ds(POOL_PREFIX - d, tm), sl]
        cnt = jnp.minimum(pos + 1, w).astype(F32)
        pool_ref[:, sl] = (acc / cnt - u[:, sl]).astype(BF16)
    y = jnp.zeros((tm, 512), F32) + cb_ref[...]
    for j in range(CONV_WIDTH):
        y = y + a_scr[pl.ds(CONV_PREFIX - (CONV_WIDTH - 1) + j, tm), :] * cw_ref[j:j + 1, :]
    mu = jnp.mean(y, axis=-1, keepdims=True)
    yc = y - mu
    var = jnp.mean(yc * yc, axis=-1, keepdims=True)
    yn = yc * lax.rsqrt(var + 1e-5) * lg_ref[...] + lb_ref[...]
    cact_ref[...] = (yn * jax.nn.sigmoid(yn)).astype(BF16)


def _mixpre(u, u_state, a, a_state, cw, cb, lg, lb, tm, tiles_per_seq, pos_base):
    has_state = u_state is not None
    if has_state:
        up, ap = u_state, a_state
        n_tiles = u.shape[0]
        up_spec = pl.BlockSpec((None, POOL_PREFIX, 512), lambda i: (i, 0, 0))
        ap_spec = pl.BlockSpec((None, CONV_PREFIX, 512), lambda i: (i, 0, 0))
        row = pl.BlockSpec((None, tm, 512), lambda i: (i, 0, 0))
        out_shape = (n_tiles, tm, 512)
    else:
        up, ap = u, a
        n_tiles = u.shape[0] // tm
        up_spec = pl.BlockSpec((POOL_PREFIX, 512), lambda i: (jnp.maximum(i * (tm // POOL_PREFIX) - 1, 0), 0))
        ap_spec = pl.BlockSpec((CONV_PREFIX, 512), lambda i: (jnp.maximum(i * (tm // CONV_PREFIX) - 1, 0), 0))
        row = pl.BlockSpec((tm, 512), lambda i: (i, 0))
        out_shape = (u.shape[0], 512)
    return pl.pallas_call(
        functools.partial(_mixpre_kernel, tm=tm, tiles_per_seq=tiles_per_seq, pos_base=pos_base, has_state=has_state),
        grid=(n_tiles,),
        in_specs=[row, up_spec, row, ap_spec, _const_spec(cw.shape), _const_spec(cb.shape),
                  _const_spec(lg.shape), _const_spec(lb.shape)],
        out_specs=[row, row],
        out_shape=[jax.ShapeDtypeStruct(out_shape, BF16), jax.ShapeDtypeStruct(out_shape, BF16)],
        scratch_shapes=[pltpu.VMEM((tm + POOL_PREFIX, 512), F32), pltpu.VMEM((tm + CONV_PREFIX, 512), F32)],
        compiler_params=_cparams(("parallel",)),
        name="mixpre",
    )(u, up, a, ap, cw, cb, lg, lb)


def _dense_kernel(x_ref, pool_ref, onsa_ref, cact_ref, gm_ref, pw_ref, ps_ref, wbp_ref, wbn_ref, wbc_ref, wo_ref,
                  nf_ref, wg_ref, wu_ref, wd_ref, nfin_ref, o_ref, *, final, ff_chunk, onsa_t):
    op = jnp.concatenate([_mm(pool_ref[:, g * 128:(g + 1) * 128], pw_ref[g]) for g in range(4)], axis=1)
    op = (op * ps_ref[...]).astype(BF16)
    onsa = onsa_ref[...].astype(F32).T.astype(BF16) if onsa_t else onsa_ref[...]
    m = gm_ref[:, 0:1024] * _mm(op, wbp_ref[...])
    m = m + gm_ref[:, 1024:2048] * _mm(onsa, wbn_ref[...])
    m = m + gm_ref[:, 2048:3072] * _mm(cact_ref[...], wbc_ref[...])
    x1 = x_ref[...] + _mm(m.astype(BF16), wo_ref[...])
    hb = _rms(x1, nf_ref[...]).astype(BF16)
    dff = wg_ref.shape[1]
    acc = x1
    for c0 in range(0, dff, ff_chunk):
        gt = _mm(hb, wg_ref[:, c0:c0 + ff_chunk])
        up = _mm(hb, wu_ref[:, c0:c0 + ff_chunk])
        acc = acc + _mm((gt * jax.nn.sigmoid(gt) * up).astype(BF16), wd_ref[c0:c0 + ff_chunk, :])
    o_ref[...] = _rms(acc, nfin_ref[...]) if final else acc


def _dense(x2, pooled, onsa, cact, gm, weights, tm, final, seq=None):
    m = x2.shape[0]
    row = lambda c: pl.BlockSpec((tm, c), lambda i: (i, 0))
    if seq is None:
        onsa_spec = row(1024)
    else:
        nt = seq // tm
        onsa_spec = pl.BlockSpec((1024, tm), lambda i: (i // nt, i % nt))
    dff = weights[8].shape[1]
    ff_chunk = dff // 2 if (dff // 2) % 128 == 0 else dff
    return pl.pallas_call(
        functools.partial(_dense_kernel, final=final, ff_chunk=ff_chunk, onsa_t=seq is not None),
        grid=(m // tm,),
        in_specs=[row(1024), row(512), onsa_spec, row(512), row(3072)] + [_const_spec(w.shape) for w in weights],
        out_specs=row(1024),
        out_shape=jax.ShapeDtypeStruct((m, 1024), F32),
        compiler_params=_cparams(("parallel",)),
        name="dense",
    )(x2, pooled, onsa, cact, gm, *weights)


def _rope_angles(pos):
    inv = ROPE_THETA ** (-jnp.arange(ROPE_HALF, dtype=F32) * (2.0 / ROPE_DIM))
    return pos.astype(F32)[:, None] * inv[None, :]


def _rope_table(pos):
    ang = _rope_angles(pos)
    cos, sin = jnp.cos(ang), jnp.sin(ang)
    t = pos.shape[0]
    zeros = jnp.zeros((t, HEAD_DIM - ROPE_DIM), F32)
    zh = jnp.zeros((t, ROPE_HALF), F32)
    c = jnp.concatenate([cos, cos, zeros + 1.0], axis=1)
    s1 = jnp.concatenate([-sin, zh, zeros], axis=1)
    s2 = jnp.concatenate([zh, sin, zeros], axis=1)
    return jnp.concatenate([c, c, s1, s1, s2, s2], axis=1)


def _rope_table_t(pos):
    ang = _rope_angles(pos)
    return jnp.concatenate([jnp.cos(ang).T, jnp.sin(ang).T], axis=0)


def _cmp_to_slc_t(n_cmp_rows, nb):
    i0 = jnp.arange(n_cmp_rows)[None, :] * CMP_STRIDE
    j0 = jnp.arange(nb)[:, None] * SLC_BLOCK
    ov = jnp.clip(jnp.minimum(i0 + CMP_BLOCK, j0 + SLC_BLOCK) - jnp.maximum(i0, j0), 0, None)
    return (ov.astype(F32) / CMP_BLOCK).astype(BF16)


def _expand_matrix(n_rows, n_keys):
    return (jnp.arange(n_rows)[:, None] == jnp.arange(n_keys)[None, :] // SLC_BLOCK).astype(BF16)


def _layer_weights(l, norm_mix, w_in, pool_w, pool_scale, cmp_w1, cmp_pe, cmp_w2, conv_w, conv_b, conv_ln_g,
                   conv_ln_b, w_br_pool, w_br_nsa, w_br_conv, w_out, norm_ffn, w_gate, w_up, w_down, norm_final):
    w = w_in[l]
    s_q, s_kv, s_gate = 512, 512 + 1024, 512 + 1024 + 1536
    n_g = 3 * N_HEADS
    order = [(k, br, g) for k in range(KV_HEADS) for br in range(3) for g in range(GROUP)]
    idx = jnp.array([s_gate + (k * GROUP + g) * 3 + br for k, br, g in order])
    w_s = jnp.concatenate([w[:, :s_gate], w[:, idx], jnp.zeros((w.shape[0], GATE_COLS - n_g), w.dtype),
                           w[:, s_gate + n_g:]], axis=1).astype(BF16)
    w_rm = jnp.concatenate([w[:, :s_q], w[:, s_kv:s_kv + 512], w[:, s_gate + n_g:]], axis=1).astype(BF16)
    gate_rows = jnp.zeros((KV_HEADS * GATE_ROWS, w.shape[0]), w.dtype)
    gate_rows = gate_rows.at[jnp.array([k * GATE_ROWS + br * GROUP + g for k, br, g in order])].set(w[:, idx].T)
    w_fm = jnp.concatenate([w[:, s_q:s_gate].T, gate_rows], axis=0).astype(BF16)
    w1 = cmp_w1[l]
    eye2 = jnp.eye(2, dtype=F32)
    w1h = w1.reshape(2, 2, CMP_STRIDE, HEAD_DIM, 128)
    w1p = jnp.einsum('chsde,kq->cskdqhe', w1h, eye2).reshape(2, CMP_STRIDE * 128, 512).astype(BF16)
    w2p = jnp.einsum('ced,kq->ckeqd', cmp_w2[l], eye2).reshape(2, 256, 128).astype(BF16)
    bias = jnp.einsum('cpd,cpde->ce', cmp_pe[l], w1, precision=lax.Precision.HIGHEST)
    cw = jnp.concatenate([conv_w[l], jnp.zeros((1, conv_w.shape[2]), F32)], axis=0)
    dense_w = (pool_w[l].astype(BF16), pool_scale[l][None], w_br_pool[l].astype(BF16), w_br_nsa[l].astype(BF16),
               w_br_conv[l].astype(BF16), w_out[l].astype(BF16), norm_ffn[l][None], w_gate[l].astype(BF16),
               w_up[l].astype(BF16), w_down[l].astype(BF16), norm_final[None])
    return dict(g=norm_mix[l][None], w_s=w_s, w_rm=w_rm, w_fm=w_fm, w1p=w1p, w2p=w2p, bias=bias, cw=cw,
                cb=conv_b[l][None], lg=conv_ln_g[l][None], lb=conv_ln_b[l][None], dense=dense_w)


def kernel(x_prompt, x_sample, cache_cmp_kv, cache_slc_kv, state_win_kv, state_pool, state_conv, page_table,
           norm_mix, w_in, pool_w, pool_scale, cmp_w1, cmp_pe, cmp_w2, conv_w, conv_b, conv_ln_g, conv_ln_b,
           w_br_pool, w_br_nsa, w_br_conv, w_out, norm_ffn, w_gate, w_up, w_down, norm_final):
    bp, sp, d = x_prompt.shape
    bs, ts, _ = x_sample.shape
    depth = w_in.shape[0]
    n_pages = page_table.shape[1]
    past = n_pages * PAGE_SIZE
    wb = state_win_kv.shape[2]
    assert d == 1024 and sp % 512 == 0 and sp >= WINDOW + 128 and ts == 8 and n_pages % 4 == 0

    tm = 256
    tq = 128
    kc_sel = 512
    pg = 16 if n_pages % 16 == 0 else 4
    mp, ms = bp * sp, bs * ts

    rope_pt = _rope_table_t(jnp.arange(sp))
    rope_s = jnp.tile(_rope_table(past + jnp.arange(ts)), (bs, 1))
    nch_p = sp // CMP_STRIDE
    nb_p = ((sp // SLC_BLOCK + 7) // 8) * 8
    ct_p = _cmp_to_slc_t(nch_p, nb_p)
    nch_s = past // CMP_STRIDE
    n_slc_s = (past + ts + SLC_BLOCK - 1) // SLC_BLOCK
    nb_s = ((n_slc_s + 7) // 8) * 8
    ct_s = _cmp_to_slc_t(nch_s, nb_s)
    e_s = _expand_matrix(128, pg * PAGE_SIZE)

    fm = lambda z: jnp.transpose(z, (0, 1, 3, 4, 5, 2))
    cache_cmp_t, cache_slc_t, win_t = fm(cache_cmp_kv), fm(cache_slc_kv), fm(state_win_kv)

    xp = x_prompt.reshape(mp, d)
    xs = x_sample.reshape(ms, d)
    outs = {n: [] for n in ("cmp_p", "cmp_s", "slc_p", "slc_s", "win_p", "win_s", "pool_p", "pool_s", "conv_p", "conv_s")}
    kvshape = (2, KV_HEADS, HEAD_DIM)
    tok_major = lambda z: jnp.transpose(z.reshape(bp, *kvshape, z.shape[-1]), (0, 4, 1, 2, 3))
    for l in range(depth):
        lw = _layer_weights(l, norm_mix, w_in, pool_w, pool_scale, cmp_w1, cmp_pe, cmp_w2, conv_w, conv_b, conv_ln_g,
                            conv_ln_b, w_br_pool, w_br_nsa, w_br_conv, w_out, norm_ffn, w_gate, w_up, w_down, norm_final)
        final = l == depth - 1

        (u, a, gm, kvc, ks, kw, qt, qrt, kvct, kvst, kvwt, vst, vwt, gt) = _proj_prompt(
            xp, lw["g"], lw["w_rm"], lw["w_fm"], rope_pt, bp, sp, tm)
        lohi = _cmp_lohi_dense(kvc, lw["w1p"], sp).reshape(bp, nch_p, 2048)
        kvcb, vct = _cmp_fin(lohi, jnp.zeros((bp, 1, 2048), F32), lw["bias"], lw["w2p"])
        onsa_t = _attn_prompt(qt, qrt, gt, kvcb, vct, ks, vst, kw, vwt, ct_p, bp, sp, tq, kc_sel)
        pooled, cact = _mixpre(u, None, a, None, lw["cw"], lw["cb"], lw["lg"], lw["lb"], tm, sp // tm, 0)
        xp = _dense(xp, pooled, onsa_t, cact, gm, lw["dense"], tm, final, seq=sp)
        outs["cmp_p"].append(tok_major(kvct))
        outs["slc_p"].append(tok_major(kvst))
        outs["win_p"].append(tok_major(kvwt[:, -min(WINDOW, sp):]))
        outs["pool_p"].append(u.reshape(bp, sp, 512)[:, -(POOL_PREFIX - 1):])
        outs["conv_p"].append(a.reshape(bp, sp, 512)[:, -(CONV_WIDTH - 1):])

        u, q, qr, kvc, kvs, kvw, gn, a, gm = _proj_sample(xs, lw["g"], lw["w_s"], rope_s)
        pad8 = lambda z: jnp.concatenate([z.reshape(bs, ts, 512), jnp.zeros((bs, 16 - ts, 512), F32)], axis=1)
        lohi = _cmp_lohi_paged(cache_cmp_t, l, page_table, lw["w1p"], pg)
        lohi_new = _cmp_lohi_dense(pad8(kvc).reshape(bs * 16, 512), lw["w1p"], bs * 16)
        kvcb, _ = _cmp_fin(lohi, lohi_new.reshape(bs, 1, 2048), lw["bias"], lw["w2p"])
        q3, qr3 = q.reshape(bs, ts, 1024), qr.reshape(bs, ts, 1024)
        gn4 = gn.reshape(KV_HEADS, bs, ts, 128)
        o_part, sel = _samp_a(q3, qr3, gn4, kvcb, win_t, l, pad8(kvw), ct_s, past)
        onsa = _samp_b(cache_slc_t, l, page_table, qr3, sel, e_s, gn4, pad8(kvs), o_part, pg)
        u_state = jnp.concatenate([jnp.zeros((bs, 1, 512), F32), state_pool[l]], axis=1)
        a_state = jnp.concatenate([jnp.zeros((bs, 2, 512), F32), state_conv[l]], axis=1)
        pooled, cact = _mixpre(u.reshape(bs, ts, 512), u_state, a.reshape(bs, ts, 512), a_state,
                               lw["cw"], lw["cb"], lw["lg"], lw["lb"], ts, 1, past)
        xs = _dense(xs, pooled.reshape(ms, 512), onsa.reshape(ms, 1024), cact.reshape(ms, 512), gm, lw["dense"],
                    ms, final)
        outs["cmp_s"].append(kvc.reshape(bs, ts, *kvshape))
        outs["slc_s"].append(kvs.reshape(bs, ts, *kvshape))
        kvw_t = jnp.transpose(kvw.reshape(bs, ts, *kvshape), (0, 2, 3, 4, 1))
        win_new_t = jnp.concatenate([win_t[l], kvw_t], axis=-1)[..., -wb:]
        outs["win_s"].append(jnp.transpose(win_new_t, (0, 4, 1, 2, 3)))
        outs["pool_s"].append(jnp.concatenate([state_pool[l], u.reshape(bs, ts, 512)], axis=1)[:, -(POOL_PREFIX - 1):])
        outs["conv_s"].append(jnp.concatenate([state_conv[l], a.reshape(bs, ts, 512)], axis=1)[:, -(CONV_WIDTH - 1):])

    st = lambda n: jnp.stack(outs[n])
    return (xp.reshape(bp, sp, d), xs.reshape(bs, ts, d), st("cmp_p"), st("cmp_s"), st("slc_p"), st("slc_s"),
            st("win_p"), st("win_s"), st("pool_p"), st("pool_s"), st("conv_p"), st("conv_s"))
```

```python
import functools

import jax
import jax.numpy as jnp
from jax import lax
from jax.experimental import pallas as pl
from jax.experimental.pallas import tpu as pltpu

F32 = jnp.float32
BF16 = jnp.bfloat16

HEAD_DIM = 64
N_HEADS = 16
KV_HEADS = 4
GROUP = N_HEADS // KV_HEADS
ROPE_DIM = HEAD_DIM // 4
ROPE_HALF = ROPE_DIM // 2
ROPE_THETA = 500000.0
CMP_BLOCK = 32
CMP_STRIDE = 16
SLC_BLOCK = 64
SLC_TOP = 16
WINDOW = 512
PAGE_SIZE = 128
POOL_WINDOWS = (2, 4, 8, 16)
POOL_PREFIX = 16
CONV_WIDTH = 31
CONV_PREFIX = 32
SM_SCALE = HEAD_DIM ** -0.5
LOG2E = 1.4426950408889634
NEG = -1e30
LANES = 128
GATE_COLS = 128
GATE_ROWS = 16
VMEM_LIMIT = 56 * 1024 * 1024


def _cparams(sem):
    return pltpu.CompilerParams(dimension_semantics=sem, vmem_limit_bytes=VMEM_LIMIT)


def _const_spec(shape):
    n = len(shape)
    return pl.BlockSpec(shape, lambda *a: (0,) * n, pipeline_mode=pl.Buffered(1))


def _nt(a, b):
    return lax.dot_general(a, b, (((1,), (1,)), ((), ())), preferred_element_type=F32)


def _mm(a, b):
    return jnp.dot(a, b, preferred_element_type=F32)


def _rms(x, g):
    return x * lax.rsqrt(jnp.mean(x * x, axis=-1, keepdims=True) + 1e-6) * g


def _rope128(slab, c, s1, s2):
    return slab * c + pltpu.roll(slab, LANES - ROPE_HALF, 1) * s1 + pltpu.roll(slab, ROPE_HALF, 1) * s2


def _proj_sample_kernel(x_ref, g_ref, w_ref, rope_ref, u_ref, q_ref, qr_ref, kvc_ref, kvs_ref, kvw_ref,
                        gn_ref, a_ref, gm_ref):
    x = x_ref[...]
    hb = _rms(x, g_ref[...]).astype(BF16)

    def mm(c0, n):
        return _mm(hb, w_ref[:, c0:c0 + n])

    c = rope_ref[:, 0:128]
    s1 = rope_ref[:, 128:256]
    s2 = rope_ref[:, 256:384]
    u_ref[...] = mm(0, 512)
    q = mm(512, 1024)
    q_ref[...] = (q * SM_SCALE).astype(BF16)
    for i in range(8):
        sl = slice(i * 128, (i + 1) * 128)
        qr_ref[:, sl] = (_rope128(q[:, sl], c, s1, s2) * SM_SCALE).astype(BF16)
    kv = mm(1536, 1536)
    kvc_ref[...] = kv[:, 0:512]
    for off, o_ref in ((512, kvs_ref), (1024, kvw_ref)):
        for i in range(2):
            o_ref[:, i * 128:(i + 1) * 128] = _rope128(kv[:, off + i * 128:off + (i + 1) * 128], c, s1, s2)
        o_ref[:, 256:512] = kv[:, off + 256:off + 512]
    gz = jax.nn.sigmoid(mm(3072, GATE_COLS))
    gn_ref[0] = gz
    for k in range(1, KV_HEADS):
        gn_ref[k] = pltpu.roll(gz, LANES - 3 * GROUP * k, 1)
    ci = mm(3072 + GATE_COLS, 1024)
    a_ref[...] = ci[:, :512] * jax.nn.sigmoid(ci[:, 512:])
    for i in range(3):
        gm_ref[:, i * 1024:(i + 1) * 1024] = jax.nn.sigmoid(mm(4096 + GATE_COLS + i * 1024, 1024))


def _proj_sample(x2, g, w, rope):
    m = x2.shape[0]
    outs = [((m, 512), F32), ((m, 1024), BF16), ((m, 1024), BF16), ((m, 512), F32), ((m, 512), F32),
            ((m, 512), F32), ((KV_HEADS, m, 128), F32), ((m, 512), F32), ((m, 3072), F32)]
    full = lambda shp: pl.BlockSpec(shp, lambda i: (0,) * len(shp))
    return pl.pallas_call(
        _proj_sample_kernel,
        grid=(1,),
        in_specs=[full(x2.shape), _const_spec(g.shape), _const_spec(w.shape), full(rope.shape)],
        out_specs=[full(s) for s, _ in outs],
        out_shape=[jax.ShapeDtypeStruct(s, d) for s, d in outs],
        compiler_params=_cparams(("arbitrary",)),
        name="proj_sample",
    )(x2, g, w, rope)


def _rope_t(z, cos, sin, n_heads):
    out = []
    for h in range(n_heads):
        x1 = z[h * 64:h * 64 + ROPE_HALF]
        x2 = z[h * 64 + ROPE_HALF:h * 64 + ROPE_DIM]
        out += [x1 * cos - x2 * sin, x2 * cos + x1 * sin, z[h * 64 + ROPE_DIM:(h + 1) * 64]]
    return jnp.concatenate(out, axis=0)


def _proj_prompt_kernel(x_ref, g_ref, w_ref, wt_ref, rope_ref, u_ref, a_ref, gm_ref, kvc_ref, ks_ref, kw_ref,
                        qt_ref, qrt_ref, kvct_ref, kvst_ref, kvwt_ref, vst_ref, vwt_ref, gt_ref):
    hb = _rms(x_ref[...], g_ref[...]).astype(BF16)

    def mm(c0, n):
        return _mm(hb, w_ref[:, c0:c0 + n])

    def mt(r0, n):
        return _nt(wt_ref[r0:r0 + n, :], hb)

    u_ref[...] = mm(0, 512)
    kvc_ref[...] = mm(512, 512)
    ci = mm(1024, 1024)
    a_ref[...] = ci[:, :512] * jax.nn.sigmoid(ci[:, 512:])
    for i in range(3):
        gm_ref[:, i * 1024:(i + 1) * 1024] = jax.nn.sigmoid(mm(2048 + i * 1024, 1024))
    cos = rope_ref[0:ROPE_HALF, :]
    sin = rope_ref[ROPE_HALF:ROPE_DIM, :]
    qt = mt(0, 1024) * (SM_SCALE * LOG2E)
    qt_ref[...] = qt.astype(BF16)
    qrt_ref[...] = _rope_t(qt, cos, sin, N_HEADS).astype(BF16)
    kvct_ref[...] = mt(1024, 512)
    for off, kvt_ref, vt_ref, k_ref in ((1536, kvst_ref, vst_ref, ks_ref), (2048, kvwt_ref, vwt_ref, kw_ref)):
        kt = _rope_t(mt(off, 256), cos, sin, KV_HEADS)
        vt = mt(off + 256, 256)
        kvt_ref[0:256, :] = kt
        kvt_ref[256:512, :] = vt
        vt_ref[...] = vt.astype(BF16)
        k_ref[...] = kt.T.astype(BF16)
    gt_ref[...] = jax.nn.sigmoid(mt(2560, KV_HEADS * GATE_ROWS))


def _proj_prompt(x2, g, w, wt, rope_t, bsz, seq, tm):
    m = x2.shape[0]
    nt = seq // tm
    row = lambda c: pl.BlockSpec((tm, c), lambda i: (i, 0))
    fm = lambda r: pl.BlockSpec((r, tm), lambda i: (i // nt, i % nt))
    outs = [((m, 512), F32, row(512)), ((m, 512), F32, row(512)), ((m, 3072), F32, row(3072)),
            ((m, 512), F32, row(512)), ((m, 256), BF16, row(256)), ((m, 256), BF16, row(256)),
            ((bsz * 1024, seq), BF16, fm(1024)), ((bsz * 1024, seq), BF16, fm(1024)),
            ((bsz * 512, seq), F32, fm(512)), ((bsz * 512, seq), F32, fm(512)), ((bsz * 512, seq), F32, fm(512)),
            ((bsz * 256, seq), BF16, fm(256)), ((bsz * 256, seq), BF16, fm(256)),
            ((bsz * KV_HEADS * GATE_ROWS, seq), F32, fm(KV_HEADS * GATE_ROWS))]
    return pl.pallas_call(
        _proj_prompt_kernel,
        grid=(m // tm,),
        in_specs=[row(1024), _const_spec(g.shape), _const_spec(w.shape), _const_spec(wt.shape),
                  pl.BlockSpec((ROPE_DIM, tm), lambda i: (0, i % nt))],
        out_specs=[o[2] for o in outs],
        out_shape=[jax.ShapeDtypeStruct(o[0], o[1]) for o in outs],
        compiler_params=_cparams(("parallel",)),
        name="proj_prompt",
    )(x2, g, w, wt, rope_t)


def _cmp_lohi_matmuls(load, w_ref, o_ref):
    for c in range(2):
        for pr in range(2):
            slab = c * 2 + pr
            lhs = jnp.concatenate([load(slab, s) for s in range(CMP_STRIDE)], axis=1)
            o_ref[:, slab * 512:(slab + 1) * 512] = _mm(lhs.astype(BF16), w_ref[c])


def _cmp_lohi_dense_kernel(x_ref, w_ref, o_ref, *, nch):
    _cmp_lohi_matmuls(lambda slab, s: x_ref[pl.ds(4 * s + slab, nch, stride=4 * CMP_STRIDE), :], w_ref, o_ref)


def _cmp_lohi_dense(x2, w1p, ntok):
    m = x2.shape[0]
    nch = ntok // CMP_STRIDE
    return pl.pallas_call(
        functools.partial(_cmp_lohi_dense_kernel, nch=nch),
        grid=(m // ntok,),
        in_specs=[pl.BlockSpec((ntok * 4, 128), lambda i: (i, 0)), _const_spec(w1p.shape)],
        out_specs=pl.BlockSpec((nch, 2048), lambda i: (i, 0)),
        out_shape=jax.ShapeDtypeStruct((m // CMP_STRIDE, 2048), F32),
        compiler_params=_cparams(("parallel",)),
        name="cmp_lohi",
    )(x2.reshape(m * 4, 128), w1p)


def _cmp_lohi_paged_kernel(pt_ref, *refs, n_pages):
    del pt_ref
    pages = refs[:n_pages]
    w_ref, o_ref, x_scr = refs[n_pages:]
    for i, p in enumerate(pages):
        for c in range(2):
            xt = p[c].reshape(KV_HEADS * HEAD_DIM, PAGE_SIZE).T
            for pr in range(2):
                x_scr[c * 2 + pr, i * PAGE_SIZE:(i + 1) * PAGE_SIZE, :] = xt[:, pr * 128:(pr + 1) * 128]
    nch = n_pages * (PAGE_SIZE // CMP_STRIDE)
    _cmp_lohi_matmuls(lambda slab, s: x_scr[slab, pl.ds(s, nch, stride=CMP_STRIDE), :], w_ref, o_ref)


def _cmp_lohi_paged(cache_t, layer, page_table, w1p, pg):
    b, n_pages = page_table.shape
    nch = pg * (PAGE_SIZE // CMP_STRIDE)

    def page_spec(i):
        return pl.BlockSpec((None, None, 2, KV_HEADS, HEAD_DIM, PAGE_SIZE),
                            lambda bb, s, pt: (layer, pt[bb, s * pg + i], 0, 0, 0, 0))

    return pl.pallas_call(
        functools.partial(_cmp_lohi_paged_kernel, n_pages=pg),
        grid_spec=pltpu.PrefetchScalarGridSpec(
            num_scalar_prefetch=1,
            grid=(b, n_pages // pg),
            in_specs=[page_spec(i) for i in range(pg)] + [_const_spec(w1p.shape)],
            out_specs=pl.BlockSpec((None, nch, 2048), lambda bb, s, pt: (bb, s, 0)),
            scratch_shapes=[pltpu.VMEM((4, pg * PAGE_SIZE, 128), F32)],
        ),
        out_shape=jax.ShapeDtypeStruct((b, n_pages * (PAGE_SIZE // CMP_STRIDE), 2048), F32),
        compiler_params=_cparams(("parallel", "parallel")),
        name="cmp_lohi_paged",
    )(page_table, *([cache_t] * pg), w1p)


def _cmp_fin_kernel(lohi_ref, nxt_ref, bias_ref, w2_ref, o_ref, vt_ref, *, nch):
    row = lax.broadcasted_iota(jnp.int32, (nch, 128), 0)
    for c in range(2):
        for pr in range(2):
            base = (c * 2 + pr) * 512
            hs = []
            for kk in range(2):
                lo = lohi_ref[:, base + kk * 256:base + kk * 256 + 128]
                hi = lohi_ref[:, base + kk * 256 + 128:base + kk * 256 + 256]
                nx = nxt_ref[:, base + kk * 256 + 128:base + kk * 256 + 256]
                hin = jnp.where(row == nch - 1, nx, pltpu.roll(hi, nch - 1, 0))
                hs.append(jax.nn.gelu(lo + hin + bias_ref[c:c + 1, :]))
            hid = jnp.concatenate(hs, axis=1).astype(BF16)
            out = _mm(hid, w2_ref[c])
            o_ref[:, c * 256 + pr * 128:c * 256 + (pr + 1) * 128] = out.astype(BF16)
            if c == 1:
                vt_ref[pr * 128:(pr + 1) * 128, :] = out.T.astype(BF16)


def _cmp_fin(lohi, nxt, bias, w2p):
    b, nch, _ = lohi.shape
    return pl.pallas_call(
        functools.partial(_cmp_fin_kernel, nch=nch),
        grid=(b,),
        in_specs=[pl.BlockSpec((None, nch, 2048), lambda i: (i, 0, 0)),
                  pl.BlockSpec((None, 1, 2048), lambda i: (i, 0, 0)),
                  _const_spec(bias.shape), _const_spec(w2p.shape)],
        out_specs=[pl.BlockSpec((None, nch, 512), lambda i: (i, 0, 0)),
                   pl.BlockSpec((None, 256, nch), lambda i: (i, 0, 0))],
        out_shape=[jax.ShapeDtypeStruct((b, nch, 512), BF16), jax.ShapeDtypeStruct((b, 256, nch), BF16)],
        compiler_params=_cparams(("parallel",)),
        name="cmp_fin",
    )(lohi, nxt, bias, w2p)


def _split3(x):
    x1 = x.astype(BF16)
    r1 = x - x1.astype(F32)
    x2 = r1.astype(BF16)
    x3 = (r1 - x2.astype(F32)).astype(BF16)
    return x1, x2, x3


def _block_rank_select(imp_ref, n_blocks, valid):
    x = imp_ref[...]
    jrow = lax.broadcasted_iota(jnp.int32, x.shape, 0)

    def body(jp, cnt):
        r = imp_ref[pl.ds(jp, 1), :]
        return cnt + jnp.where(jrow > jp, jnp.where(r >= x, 1.0, 0.0), jnp.where(r > x, 1.0, 0.0))

    cnt = lax.fori_loop(0, n_blocks, body, jnp.zeros(x.shape, F32))
    return jnp.where((cnt < SLC_TOP) & valid, 1.0, 0.0)


def _update_t(m_ref, acc_ref, sl, s, v_ext):
    m_old = m_ref[:, sl]
    m_new = jnp.maximum(m_old, s.max(axis=0, keepdims=True))
    alpha = jnp.exp2(m_old - m_new)
    p = jnp.exp2(s - m_new).astype(BF16)
    acc_ref[:, sl] = acc_ref[:, sl] * alpha + _mm(v_ext, p)
    m_ref[:, sl] = m_new


def _finish_t(acc_ref, sl):
    return acc_ref[0:128, sl] * (1.0 / jnp.maximum(acc_ref[128:129, sl], 1e-30))


def _attn_prompt_kernel(qt_ref, qrt_ref, gt_ref, kc_ref, vct_ref, ks_ref, vst_ref, kw_ref, vwt_ref, ct_ref, et_ref,
                        o_ref, imp_scr, lim_scr, qa_scr, oc_scr, m_scr, acc_scr, mw_scr, accw_scr,
                        *, tq, nch, n_slc, kc_sel, kc_win):
    k = pl.program_id(1)
    t0 = pl.program_id(2) * tq
    par = k % 2
    r = GROUP * tq

    def stack(ref):
        z = jnp.zeros((HEAD_DIM, tq), BF16)
        cols = []
        for g in range(GROUP):
            qg = ref[g * HEAD_DIM:(g + 1) * HEAD_DIM, :]
            cols.append(jnp.where(par == 0, jnp.concatenate([qg, z], axis=0), jnp.concatenate([z, qg], axis=0)))
        return jnp.concatenate(cols, axis=1)

    q4 = stack(qt_ref)
    qr4 = stack(qrt_ref)
    pos = t0 + lax.broadcasted_iota(jnp.int32, (1, tq), 1)

    vis = lax.broadcasted_iota(jnp.int32, (nch, tq), 0) <= ((pos - (CMP_BLOCK - 1)) >> 4)
    kc = kc_ref[...]
    vct = vct_ref[...]
    p_sum = None
    s_c = _mm(kc, q4)
    es, invs = [], []
    for g in range(GROUP):
        sg = jnp.where(vis, s_c[:, g * tq:(g + 1) * tq], NEG)
        e = jnp.where(vis, jnp.exp2(sg - sg.max(axis=0, keepdims=True)), 0.0)
        inv = 1.0 / jnp.maximum(e.sum(axis=0, keepdims=True), 1e-30)
        p_sum = e * inv if p_sum is None else p_sum + e * inv
        es.append(e.astype(BF16))
        invs.append(inv)
    oc_scr[...] = _mm(vct, jnp.concatenate(es, axis=1)) * jnp.concatenate(invs, axis=1)

    ct = ct_ref[...]
    imp = sum(_mm(ct, piece) for piece in _split3(p_sum))
    nb = imp.shape[0]
    jb = lax.broadcasted_iota(jnp.int32, (nb, tq), 0)
    cur = pos >> 6
    forced = (jb == 0) | (jb == cur) | (jb == cur - 1)
    valid = jb <= cur
    posb = jnp.broadcast_to(pos, (nb, tq))
    imp_scr[...] = jnp.where(valid, jnp.where(forced, jnp.inf, imp), -jnp.inf)

    @pl.when(t0 + tq <= SLC_TOP * SLC_BLOCK)
    def _():
        lim_scr[...] = jnp.where(valid, posb, -1)

    @pl.when(t0 + tq > SLC_TOP * SLC_BLOCK)
    def _():
        sel = _block_rank_select(imp_scr, jnp.minimum(n_slc, (t0 + tq - 1) // SLC_BLOCK + 1), valid)
        lim_scr[...] = jnp.where(sel > 0.5, posb, -1)

    bias = jnp.where(lim_scr[...] >= 0, 0.0, NEG).astype(BF16)
    bias = jnp.concatenate([bias, jnp.zeros((128 - nb, tq), BF16)], axis=0)
    qa_scr[0:128, :] = qr4
    qa_scr[128:256, :] = jnp.concatenate([bias] * GROUP, axis=1)
    m_scr[...] = jnp.full((1, r), NEG, F32)
    acc_scr[...] = jnp.zeros(acc_scr.shape, F32)
    ones_s = jnp.ones((16, kc_sel), BF16)

    def sel_chunk(k0, causal):
        k_aug = jnp.concatenate([ks_ref[pl.ds(k0, kc_sel), :], et_ref[pl.ds(k0, kc_sel), :]], axis=1)
        v_ext = jnp.concatenate([vst_ref[:, pl.ds(k0, kc_sel)], ones_s], axis=0)
        s = _mm(k_aug, qa_scr[...])
        if causal:
            al = (k0 + lax.broadcasted_iota(jnp.int32, (kc_sel, tq), 0)) <= pos
            s = jnp.concatenate([jnp.where(al, s[:, g * tq:(g + 1) * tq], NEG) for g in range(GROUP)], axis=1)
        _update_t(m_scr, acc_scr, slice(None), s, v_ext)

    def sel_body(c, carry):
        sel_chunk(pl.multiple_of(c * kc_sel, kc_sel), False)
        return carry

    c_last = t0 // kc_sel
    lax.fori_loop(0, c_last, sel_body, 0)
    sel_chunk(pl.multiple_of(c_last * kc_sel, kc_sel), True)

    mw_scr[...] = jnp.full((1, r), NEG, F32)
    accw_scr[...] = jnp.zeros(accw_scr.shape, F32)
    ones_w = jnp.ones((16, kc_win), BF16)
    k0w = jnp.maximum(t0 - WINDOW, 0)
    for i in range((WINDOW + tq) // kc_win):
        kk = pl.multiple_of(k0w + i * kc_win, tq)
        dp = pos - (kk + lax.broadcasted_iota(jnp.int32, (kc_win, tq), 0))
        al = lax.bitcast_convert_type(dp, jnp.uint32) < WINDOW
        s = _mm(kw_ref[pl.ds(kk, kc_win), :], qa_scr[0:128, :])
        s = jnp.concatenate([jnp.where(al, s[:, g * tq:(g + 1) * tq], NEG) for g in range(GROUP)], axis=1)
        _update_t(mw_scr, accw_scr, slice(None), s,
                  jnp.concatenate([vwt_ref[:, pl.ds(kk, kc_win)], ones_w], axis=0))

    gt = gt_ref[...]
    for g in range(GROUP):
        sl = slice(g * tq, (g + 1) * tq)
        og = gt[g:g + 1] * oc_scr[:, sl] + gt[GROUP + g:GROUP + g + 1] * _finish_t(acc_scr, sl) \
            + gt[2 * GROUP + g:2 * GROUP + g + 1] * _finish_t(accw_scr, sl)
        o_ref[g * HEAD_DIM:(g + 1) * HEAD_DIM, :] = jnp.where(par == 0, og[0:HEAD_DIM], og[HEAD_DIM:]).astype(BF16)


def _attn_prompt(qt, qrt, gt, kvc, vct, ks, vst, kw, vwt, ct, et, bsz, seq, tq, kc_sel, kc_win):
    nch = kvc.shape[1]
    nt = seq // tq
    n_slc = seq // SLC_BLOCK
    nb = ct.shape[0]
    r = GROUP * tq
    qspec = pl.BlockSpec((256, tq), lambda b, k, t: (b * KV_HEADS + k, t))
    kspec = pl.BlockSpec((seq, 128), lambda b, k, t: (b, k // 2))
    vspec = pl.BlockSpec((128, seq), lambda b, k, t: (b * 2 + k // 2, 0))
    return pl.pallas_call(
        functools.partial(_attn_prompt_kernel, tq=tq, nch=nch, n_slc=n_slc, kc_sel=kc_sel, kc_win=kc_win),
        grid=(bsz, KV_HEADS, nt),
        in_specs=[qspec, qspec,
                  pl.BlockSpec((GATE_ROWS, tq), lambda b, k, t: (b * KV_HEADS + k, t)),
                  pl.BlockSpec((None, nch, 128), lambda b, k, t: (b, 0, k // 2)),
                  pl.BlockSpec((None, 128, nch), lambda b, k, t: (b, k // 2, 0)),
                  kspec, vspec, kspec, vspec, _const_spec(ct.shape), _const_spec(et.shape)],
        out_specs=pl.BlockSpec((256, tq), lambda b, k, t: (b * KV_HEADS + k, t)),
        out_shape=jax.ShapeDtypeStruct((bsz * 1024, seq), BF16),
        scratch_shapes=[pltpu.VMEM((nb, tq), F32), pltpu.VMEM((nb, tq), jnp.int32), pltpu.VMEM((256, r), BF16),
                        pltpu.VMEM((128, r), F32),
                        pltpu.VMEM((1, r), F32), pltpu.VMEM((128 + 16, r), F32),
                        pltpu.VMEM((1, r), F32), pltpu.VMEM((128 + 16, r), F32)],
        compiler_params=_cparams(("parallel", "parallel", "arbitrary")),
        name="attn_prompt",
    )(qt, qrt, gt, kvc, vct, ks, vst, kw, vwt, ct, et)


def _pad_q(qf, par, g, lane):
    slab = qf[:, (g // 2) * 128:(g // 2 + 1) * 128]
    rolled = pltpu.roll(slab, 64, 1)
    src = slab if par == g % 2 else rolled
    return jnp.where((lane >> 6) == par, src, 0.0)


def _stack_q(q_blk, par):
    qf = q_blk.astype(F32)
    lane = lax.broadcasted_iota(jnp.int32, (qf.shape[0], 128), 1)
    return jnp.concatenate([_pad_q(qf, par, g, lane) for g in range(GROUP)], axis=0).astype(BF16)


def _softmax_parts(parts, r):
    ss = []
    for s, al, _ in parts:
        ss.append(jnp.where(al[None], s.reshape(GROUP, r, s.shape[-1]), NEG))
    m = ss[0].max(axis=-1, keepdims=True)
    for s in ss[1:]:
        m = jnp.maximum(m, s.max(axis=-1, keepdims=True))
    l = jnp.zeros_like(m)
    acc = jnp.zeros((GROUP, r, 128), F32)
    ps = []
    for (_, al, pv), s in zip(parts, ss):
        p = jnp.where(al[None], jnp.exp(s - m), 0.0)
        ps.append(p)
        l = l + p.sum(axis=-1, keepdims=True)
        acc = acc + pv(p.reshape(GROUP * r, p.shape[-1]).astype(BF16)).reshape(GROUP, r, 128)
    inv = 1.0 / jnp.maximum(l, 1e-30)
    return acc * inv, [p * inv for p in ps]


def _online_update(m_ref, l_ref, acc_ref, s, allowed, pv, r):
    n = s.shape[-1]
    s3 = jnp.where(allowed[None], s.reshape(GROUP, r, n), NEG)
    m_old = m_ref[...]
    m_new = jnp.maximum(m_old, s3.max(axis=-1, keepdims=True))
    alpha = jnp.exp(m_old - m_new)
    p = jnp.where(allowed[None], jnp.exp(s3 - m_new), 0.0)
    l_ref[...] = alpha * l_ref[...] + p.sum(axis=-1, keepdims=True)
    acc_ref[...] = alpha * acc_ref[...] + pv(p.reshape(GROUP * r, n).astype(BF16)).reshape(GROUP, r, 128)
    m_ref[...] = m_new


def _place_heads(o4, par, r):
    lane = lax.broadcasted_iota(jnp.int32, (r, 128), 1)
    outs = []
    for pr in range(2):
        lo, hi = o4[2 * pr], o4[2 * pr + 1]
        lo = lo if par == 0 else pltpu.roll(lo, 64, 1)
        hi = hi if par == 1 else pltpu.roll(hi, 64, 1)
        outs.append(jnp.where(lane < 64, lo, hi))
    return jnp.concatenate(outs, axis=1)


def _gate(gn, br, r):
    return jnp.stack([gn[:, br * GROUP + g:br * GROUP + g + 1] for g in range(GROUP)], axis=0)


def _pair_t(ref, c, pr):
    return ref[c, 2 * pr:2 * pr + 2].reshape(2 * HEAD_DIM, ref.shape[-1]).astype(BF16)


def _samp_a_kernel(q_ref, qr_ref, gn_ref, kvc_ref, win_ref, kvwn_ref, ct_ref, o_ref, sel_ref, imp_scr,
                   *, t, nch, n_slc, past, wb):
    nb = ct_ref.shape[0]
    p_sums = []
    pos_c = past + lax.broadcasted_iota(jnp.int32, (t, nch), 0)
    cend = lax.broadcasted_iota(jnp.int32, (t, nch), 1) * CMP_STRIDE + (CMP_BLOCK - 1)
    tt = lax.broadcasted_iota(jnp.int32, (t, wb), 0)
    ii = lax.broadcasted_iota(jnp.int32, (t, wb), 1)
    dp_buf = wb + tt - ii
    al_buf = (dp_buf >= 0) & (dp_buf < WINDOW) & (past - wb + ii >= 0)
    nn = kvwn_ref.shape[0]
    dp_new = lax.broadcasted_iota(jnp.int32, (t, nn), 0) - lax.broadcasted_iota(jnp.int32, (t, nn), 1)
    al_new = (dp_new >= 0) & (dp_new < WINDOW)
    for k in range(KV_HEADS):
        par, pr = k % 2, k // 2
        q4 = _stack_q(q_ref[:, k * 256:(k + 1) * 256], par)
        qr4 = _stack_q(qr_ref[:, k * 256:(k + 1) * 256], par)
        ksl = slice(pr * 128, (pr + 1) * 128)
        vsl = slice(256 + pr * 128, 256 + (pr + 1) * 128)
        kc, vc = kvc_ref[:, ksl], kvc_ref[:, vsl]
        o_cmp, (p_cmp,) = _softmax_parts([(_nt(q4, kc), cend <= pos_c, lambda p, vc=vc: _mm(p, vc))], t)
        p_sums.append(p_cmp.sum(axis=0))
        kwt, vwt = _pair_t(win_ref, 0, pr), _pair_t(win_ref, 1, pr)
        kn, vn = kvwn_ref[:, ksl].astype(BF16), kvwn_ref[:, vsl].astype(BF16)
        o_win, _ = _softmax_parts([(_mm(qr4, kwt), al_buf, lambda p, vwt=vwt: _nt(p, vwt)),
                                   (_nt(qr4, kn), al_new, lambda p, vn=vn: _mm(p, vn))], t)
        gn = gn_ref[k]
        o4 = _gate(gn, 0, t) * o_cmp + _gate(gn, 2, t) * o_win
        o_ref[:, k * 256:(k + 1) * 256] = _place_heads(o4, par, t)
    p_all = jnp.concatenate(p_sums + [jnp.zeros((128 - KV_HEADS * t, nch), F32)], axis=0)
    ct = ct_ref[...]
    imp = sum(_nt(ct, piece) for piece in _split3(p_all))
    jb = lax.broadcasted_iota(jnp.int32, (nb, 128), 0)
    col = lax.broadcasted_iota(jnp.int32, (nb, 128), 1)
    cur = (past + (col & (t - 1))) >> 6
    forced = (jb == 0) | (jb == cur) | (jb == cur - 1)
    valid = jb <= cur
    imp_scr[...] = jnp.where(valid, jnp.where(forced, jnp.inf, imp), -jnp.inf)
    sel_ref[...] = _block_rank_select(imp_scr, n_slc, valid)


def _samp_a(q3, qr3, gn4, kvc, win_t, layer, kvwn3, ct, past):
    b, t, _ = q3.shape
    nch = kvc.shape[1]
    nb = ct.shape[0]
    wb = win_t.shape[-1]
    n_slc = (past + t + SLC_BLOCK - 1) // SLC_BLOCK
    per_b = lambda r, c: pl.BlockSpec((None, r, c), lambda i: (i, 0, 0))
    return pl.pallas_call(
        functools.partial(_samp_a_kernel, t=t, nch=nch, n_slc=n_slc, past=past, wb=wb),
        grid=(b,),
        in_specs=[per_b(t, 1024), per_b(t, 1024),
                  pl.BlockSpec((KV_HEADS, None, t, 128), lambda i: (0, i, 0, 0)),
                  per_b(nch, 512),
                  pl.BlockSpec((None, None, 2, KV_HEADS, HEAD_DIM, wb), lambda i: (layer, i, 0, 0, 0, 0)),
                  per_b(kvwn3.shape[1], 512), _const_spec(ct.shape)],
        out_specs=[per_b(t, 1024), per_b(nb, 128)],
        out_shape=[jax.ShapeDtypeStruct((b, t, 1024), F32), jax.ShapeDtypeStruct((b, nb, 128), F32)],
        scratch_shapes=[pltpu.VMEM((nb, 128), F32)],
        compiler_params=_cparams(("parallel",)),
        name="samp_a",
    )(q3, qr3, gn4, kvc, win_t, kvwn3, ct)


def _samp_b_kernel(pt_ref, *refs, pg, t):
    del pt_ref
    pages = refs[:pg]
    qr_ref, sel_ref, seln_ref, e_ref, gn_ref, kvn_ref, op_ref, o_ref, m_scr, l_scr, acc_scr = refs[pg:]
    s = pl.program_id(1)
    ns = pl.num_programs(1)
    nbc = sel_ref.shape[0]

    @pl.when(s == 0)
    def _():
        m_scr[...] = jnp.full(m_scr.shape, NEG, F32)
        l_scr[...] = jnp.zeros(l_scr.shape, F32)
        acc_scr[...] = jnp.zeros(acc_scr.shape, F32)

    selc = jnp.concatenate([sel_ref[...], jnp.zeros((128 - nbc, 128), F32)], axis=0).T
    msk_all = _mm(selc[:KV_HEADS * t].astype(BF16), e_ref[...])
    for k in range(KV_HEADS):
        par, pr = k % 2, k // 2
        qr4 = _stack_q(qr_ref[:, k * 256:(k + 1) * 256], par)
        kt = jnp.concatenate([_pair_t(p, 0, pr) for p in pages], axis=1)
        vt = jnp.concatenate([_pair_t(p, 1, pr) for p in pages], axis=1)
        _online_update(m_scr.at[k], l_scr.at[k], acc_scr.at[k], _mm(qr4, kt), msk_all[k * t:(k + 1) * t] > 0.5,
                       lambda p, vt=vt: _nt(p, vt), t)

    @pl.when(s == ns - 1)
    def _():
        nn = kvn_ref.shape[0]
        dp = lax.broadcasted_iota(jnp.int32, (t, nn), 0) - lax.broadcasted_iota(jnp.int32, (t, nn), 1)
        seln = jnp.concatenate([seln_ref[...], jnp.zeros((128 - 8, 128), F32)], axis=0).T
        for k in range(KV_HEADS):
            par, pr = k % 2, k // 2
            qr4 = _stack_q(qr_ref[:, k * 256:(k + 1) * 256], par)
            kk = kvn_ref[:, pr * 128:(pr + 1) * 128].astype(BF16)
            vv = kvn_ref[:, 256 + pr * 128:256 + (pr + 1) * 128].astype(BF16)
            al = (dp >= 0) & (seln[k * t:(k + 1) * t, 0:1] > 0.5)
            _online_update(m_scr.at[k], l_scr.at[k], acc_scr.at[k], _nt(qr4, kk), al, lambda p, vv=vv: _mm(p, vv), t)
            o_sel = acc_scr[k] * (1.0 / jnp.maximum(l_scr[k], 1e-30))
            o4 = _gate(gn_ref[k], 1, t) * o_sel
            o_ref[:, k * 256:(k + 1) * 256] = (op_ref[:, k * 256:(k + 1) * 256] + _place_heads(o4, par, t)).astype(BF16)


def _samp_b(cache_t, layer, page_table, qr3, sel, e, gn4, kvn3, o_part, pg):
    b, n_pages = page_table.shape
    t = qr3.shape[1]
    ns = n_pages // pg
    nbc = pg * (PAGE_SIZE // SLC_BLOCK)
    n_past_blk = n_pages * (PAGE_SIZE // SLC_BLOCK)

    def page_spec(i):
        return pl.BlockSpec((None, None, 2, KV_HEADS, HEAD_DIM, PAGE_SIZE),
                            lambda bb, s, pt: (layer, pt[bb, s * pg + i], 0, 0, 0, 0))

    per_b = lambda r, c: pl.BlockSpec((None, r, c), lambda bb, s, pt: (bb, 0, 0))
    return pl.pallas_call(
        functools.partial(_samp_b_kernel, pg=pg, t=t),
        grid_spec=pltpu.PrefetchScalarGridSpec(
            num_scalar_prefetch=1,
            grid=(b, ns),
            in_specs=[page_spec(i) for i in range(pg)] + [
                per_b(t, 1024),
                pl.BlockSpec((None, nbc, 128), lambda bb, s, pt: (bb, s, 0)),
                pl.BlockSpec((None, 8, 128), lambda bb, s, pt: (bb, n_past_blk // 8, 0)),
                _const_spec(e.shape),
                pl.BlockSpec((KV_HEADS, None, t, 128), lambda bb, s, pt: (0, bb, 0, 0)),
                per_b(kvn3.shape[1], 512), per_b(t, 1024)],
            out_specs=per_b(t, 1024),
            scratch_shapes=[pltpu.VMEM((KV_HEADS, GROUP, t, 1), F32), pltpu.VMEM((KV_HEADS, GROUP, t, 1), F32),
                            pltpu.VMEM((KV_HEADS, GROUP, t, 128), F32)],
        ),
        out_shape=jax.ShapeDtypeStruct((b, t, 1024), BF16),
        compiler_params=_cparams(("parallel", "arbitrary")),
        name="samp_b",
    )(page_table, *([cache_t] * pg), qr3, sel, sel, e, gn4, kvn3, o_part)


def _mixpre_kernel(u_ref, up_ref, a_ref, ap_ref, cw_ref, cb_ref, lg_ref, lb_ref, pool_ref, cact_ref,
                   u_scr, a_scr, *, tm, tiles_per_seq, pos_base, has_state):
    ti = pl.program_id(0) % tiles_per_seq
    pos0 = pos_base + ti * tm
    keep = 1.0 if has_state else jnp.where(ti == 0, 0.0, 1.0)
    u = u_ref[...]
    u_scr[0:POOL_PREFIX, :] = up_ref[...] * keep
    u_scr[POOL_PREFIX:, :] = u
    a_scr[0:CONV_PREFIX, :] = ap_ref[...] * keep
    a_scr[CONV_PREFIX:, :] = a_ref[...]
    pos = pos0 + lax.broadcasted_iota(jnp.int32, (tm, 128), 0)
    for gi, w in enumerate(POOL_WINDOWS):
        sl = slice(gi * 128, (gi + 1) * 128)
        acc = u[:, sl]
        for d in range(1, w):
            acc = acc + u_scr[pl.ds(POOL_PREFIX - d, tm), sl]
        cnt = jnp.minimum(pos + 1, w).astype(F32)
        pool_ref[:, sl] = (acc / cnt - u[:, sl]).astype(BF16)
    y = jnp.zeros((tm, 512), F32) + cb_ref[...]
    for j in range(CONV_WIDTH):
        y = y + a_scr[pl.ds(CONV_PREFIX - (CONV_WIDTH - 1) + j, tm), :] * cw_ref[j:j + 1, :]
    mu = jnp.mean(y, axis=-1, keepdims=True)
    yc = y - mu
    var = jnp.mean(yc * yc, axis=-1, keepdims=True)
    yn = yc * lax.rsqrt(var + 1e-5) * lg_ref[...] + lb_ref[...]
    cact_ref[...] = (yn * jax.nn.sigmoid(yn)).astype(BF16)


def _mixpre(u, u_state, a, a_state, cw, cb, lg, lb, tm, tiles_per_seq, pos_base):
    has_state = u_state is not None
    if has_state:
        up, ap = u_state, a_state
        n_tiles = u.shape[0]
        up_spec = pl.BlockSpec((None, POOL_PREFIX, 512), lambda i: (i, 0, 0))
        ap_spec = pl.BlockSpec((None, CONV_PREFIX, 512), lambda i: (i, 0, 0))
        row = pl.BlockSpec((None, tm, 512), lambda i: (i, 0, 0))
        out_shape = (n_tiles, tm, 512)
    else:
        up, ap = u, a
        n_tiles = u.shape[0] // tm
        up_spec = pl.BlockSpec((POOL_PREFIX, 512), lambda i: (jnp.maximum(i * (tm // POOL_PREFIX) - 1, 0), 0))
        ap_spec = pl.BlockSpec((CONV_PREFIX, 512), lambda i: (jnp.maximum(i * (tm // CONV_PREFIX) - 1, 0), 0))
        row = pl.BlockSpec((tm, 512), lambda i: (i, 0))
        out_shape = (u.shape[0], 512)
    return pl.pallas_call(
        functools.partial(_mixpre_kernel, tm=tm, tiles_per_seq=tiles_per_seq, pos_base=pos_base, has_state=has_state),
        grid=(n_tiles,),
        in_specs=[row, up_spec, row, ap_spec, _const_spec(cw.shape), _const_spec(cb.shape),
                  _const_spec(lg.shape), _const_spec(lb.shape)],
        out_specs=[row, row],
        out_shape=[jax.ShapeDtypeStruct(out_shape, BF16), jax.ShapeDtypeStruct(out_shape, BF16)],
        scratch_shapes=[pltpu.VMEM((tm + POOL_PREFIX, 512), F32), pltpu.VMEM((tm + CONV_PREFIX, 512), F32)],
        compiler_params=_cparams(("parallel",)),
        name="mixpre",
    )(u, up, a, ap, cw, cb, lg, lb)


def _dense_kernel(x_ref, pool_ref, onsa_ref, cact_ref, gm_ref, pw_ref, ps_ref, wbp_ref, wbn_ref, wbc_ref, wo_ref,
                  nf_ref, wg_ref, wu_ref, wd_ref, nfin_ref, o_ref, *, final, ff_chunk, onsa_t):
    op = jnp.concatenate([_mm(pool_ref[:, g * 128:(g + 1) * 128], pw_ref[g]) for g in range(4)], axis=1)
    op = (op * ps_ref[...]).astype(BF16)
    onsa = onsa_ref[...].astype(F32).T.astype(BF16) if onsa_t else onsa_ref[...]
    m = gm_ref[:, 0:1024] * _mm(op, wbp_ref[...])
    m = m + gm_ref[:, 1024:2048] * _mm(onsa, wbn_ref[...])
    m = m + gm_ref[:, 2048:3072] * _mm(cact_ref[...], wbc_ref[...])
    x1 = x_ref[...] + _mm(m.astype(BF16), wo_ref[...])
    hb = _rms(x1, nf_ref[...]).astype(BF16)
    dff = wg_ref.shape[1]
    acc = x1
    for c0 in range(0, dff, ff_chunk):
        gt = _mm(hb, wg_ref[:, c0:c0 + ff_chunk])
        up = _mm(hb, wu_ref[:, c0:c0 + ff_chunk])
        acc = acc + _mm((gt * jax.nn.sigmoid(gt) * up).astype(BF16), wd_ref[c0:c0 + ff_chunk, :])
    o_ref[...] = _rms(acc, nfin_ref[...]) if final else acc


def _dense(x2, pooled, onsa, cact, gm, weights, tm, final, seq=None):
    m = x2.shape[0]
    row = lambda c: pl.BlockSpec((tm, c), lambda i: (i, 0))
    if seq is None:
        onsa_spec = row(1024)
    else:
        nt = seq // tm
        onsa_spec = pl.BlockSpec((1024, tm), lambda i: (i // nt, i % nt))
    dff = weights[8].shape[1]
    ff_chunk = dff // 2 if (dff // 2) % 128 == 0 else dff
    return pl.pallas_call(
        functools.partial(_dense_kernel, final=final, ff_chunk=ff_chunk, onsa_t=seq is not None),
        grid=(m // tm,),
        in_specs=[row(1024), row(512), onsa_spec, row(512), row(3072)] + [_const_spec(w.shape) for w in weights],
        out_specs=row(1024),
        out_shape=jax.ShapeDtypeStruct((m, 1024), F32),
        compiler_params=_cparams(("parallel",)),
        name="dense",
    )(x2, pooled, onsa, cact, gm, *weights)


def _rope_angles(pos):
    inv = ROPE_THETA ** (-jnp.arange(ROPE_HALF, dtype=F32) * (2.0 / ROPE_DIM))
    return pos.astype(F32)[:, None] * inv[None, :]


def _rope_table(pos):
    ang = _rope_angles(pos)
    cos, sin = jnp.cos(ang), jnp.sin(ang)
    t = pos.shape[0]
    zeros = jnp.zeros((t, HEAD_DIM - ROPE_DIM), F32)
    zh = jnp.zeros((t, ROPE_HALF), F32)
    c = jnp.concatenate([cos, cos, zeros + 1.0], axis=1)
    s1 = jnp.concatenate([-sin, zh, zeros], axis=1)
    s2 = jnp.concatenate([zh, sin, zeros], axis=1)
    return jnp.concatenate([c, c, s1, s1, s2, s2], axis=1)


def _rope_table_t(pos):
    ang = _rope_angles(pos)
    return jnp.concatenate([jnp.cos(ang).T, jnp.sin(ang).T], axis=0)


def _cmp_to_slc_t(n_cmp_rows, nb):
    i0 = jnp.arange(n_cmp_rows)[None, :] * CMP_STRIDE
    j0 = jnp.arange(nb)[:, None] * SLC_BLOCK
    ov = jnp.clip(jnp.minimum(i0 + CMP_BLOCK, j0 + SLC_BLOCK) - jnp.maximum(i0, j0), 0, None)
    return (ov.astype(F32) / CMP_BLOCK).astype(BF16)


def _expand_matrix(n_rows, n_keys):
    return (jnp.arange(n_rows)[:, None] == jnp.arange(n_keys)[None, :] // SLC_BLOCK).astype(BF16)


def _layer_weights(l, norm_mix, w_in, pool_w, pool_scale, cmp_w1, cmp_pe, cmp_w2, conv_w, conv_b, conv_ln_g,
                   conv_ln_b, w_br_pool, w_br_nsa, w_br_conv, w_out, norm_ffn, w_gate, w_up, w_down, norm_final):
    w = w_in[l]
    s_q, s_kv, s_gate = 512, 512 + 1024, 512 + 1024 + 1536
    n_g = 3 * N_HEADS
    order = [(k, br, g) for k in range(KV_HEADS) for br in range(3) for g in range(GROUP)]
    idx = jnp.array([s_gate + (k * GROUP + g) * 3 + br for k, br, g in order])
    w_s = jnp.concatenate([w[:, :s_gate], w[:, idx], jnp.zeros((w.shape[0], GATE_COLS - n_g), w.dtype),
                           w[:, s_gate + n_g:]], axis=1).astype(BF16)
    w_rm = jnp.concatenate([w[:, :s_q], w[:, s_kv:s_kv + 512], w[:, s_gate + n_g:]], axis=1).astype(BF16)
    gate_rows = jnp.zeros((KV_HEADS * GATE_ROWS, w.shape[0]), w.dtype)
    gate_rows = gate_rows.at[jnp.array([k * GATE_ROWS + br * GROUP + g for k, br, g in order])].set(w[:, idx].T)
    w_fm = jnp.concatenate([w[:, s_q:s_gate].T, gate_rows], axis=0).astype(BF16)
    w1 = cmp_w1[l]
    eye2 = jnp.eye(2, dtype=F32)
    w1h = w1.reshape(2, 2, CMP_STRIDE, HEAD_DIM, 128)
    w1p = jnp.einsum('chsde,kq->cskdqhe', w1h, eye2).reshape(2, CMP_STRIDE * 128, 512).astype(BF16)
    w2p = jnp.einsum('ced,kq->ckeqd', cmp_w2[l], eye2).reshape(2, 256, 128).astype(BF16)
    bias = jnp.einsum('cpd,cpde->ce', cmp_pe[l], w1, precision=lax.Precision.HIGHEST)
    cw = jnp.concatenate([conv_w[l], jnp.zeros((1, conv_w.shape[2]), F32)], axis=0)
    dense_w = (pool_w[l].astype(BF16), pool_scale[l][None], w_br_pool[l].astype(BF16), w_br_nsa[l].astype(BF16),
               w_br_conv[l].astype(BF16), w_out[l].astype(BF16), norm_ffn[l][None], w_gate[l].astype(BF16),
               w_up[l].astype(BF16), w_down[l].astype(BF16), norm_final[None])
    return dict(g=norm_mix[l][None], w_s=w_s, w_rm=w_rm, w_fm=w_fm, w1p=w1p, w2p=w2p, bias=bias, cw=cw,
                cb=conv_b[l][None], lg=conv_ln_g[l][None], lb=conv_ln_b[l][None], dense=dense_w)


def kernel(x_prompt, x_sample, cache_cmp_kv, cache_slc_kv, state_win_kv, state_pool, state_conv, page_table,
           norm_mix, w_in, pool_w, pool_scale, cmp_w1, cmp_pe, cmp_w2, conv_w, conv_b, conv_ln_g, conv_ln_b,
           w_br_pool, w_br_nsa, w_br_conv, w_out, norm_ffn, w_gate, w_up, w_down, norm_final):
    bp, sp, d = x_prompt.shape
    bs, ts, _ = x_sample.shape
    depth = w_in.shape[0]
    n_pages = page_table.shape[1]
    past = n_pages * PAGE_SIZE
    wb = state_win_kv.shape[2]
    assert d == 1024 and sp % 512 == 0 and sp >= WINDOW + 256 and ts == 8 and n_pages % 4 == 0

    tm = 256
    tq = 256
    kc_sel = 512
    kc_win = WINDOW + tq
    pg = 16 if n_pages % 16 == 0 else 4
    mp, ms = bp * sp, bs * ts

    rope_pt = _rope_table_t(jnp.arange(sp))
    rope_s = jnp.tile(_rope_table(past + jnp.arange(ts)), (bs, 1))
    nch_p = sp // CMP_STRIDE
    nb_p = ((sp // SLC_BLOCK + 7) // 8) * 8
    ct_p = _cmp_to_slc_t(nch_p, nb_p)
    nch_s = past // CMP_STRIDE
    n_slc_s = (past + ts + SLC_BLOCK - 1) // SLC_BLOCK
    nb_s = ((n_slc_s + 7) // 8) * 8
    ct_s = _cmp_to_slc_t(nch_s, nb_s)
    e_s = _expand_matrix(128, pg * PAGE_SIZE)
    et_p = _expand_matrix(128, sp).T

    fm = lambda z: jnp.transpose(z, (0, 1, 3, 4, 5, 2))
    cache_cmp_t, cache_slc_t, win_t = fm(cache_cmp_kv), fm(cache_slc_kv), fm(state_win_kv)

    xp = x_prompt.reshape(mp, d)
    xs = x_sample.reshape(ms, d)
    outs = {n: [] for n in ("cmp_p", "cmp_s", "slc_p", "slc_s", "win_p", "win_s", "pool_p", "pool_s", "conv_p", "conv_s")}
    kvshape = (2, KV_HEADS, HEAD_DIM)
    tok_major = lambda z: jnp.transpose(z.reshape(bp, *kvshape, z.shape[-1]), (0, 4, 1, 2, 3))
    for l in range(depth):
        lw = _layer_weights(l, norm_mix, w_in, pool_w, pool_scale, cmp_w1, cmp_pe, cmp_w2, conv_w, conv_b, conv_ln_g,
                            conv_ln_b, w_br_pool, w_br_nsa, w_br_conv, w_out, norm_ffn, w_gate, w_up, w_down, norm_final)
        final = l == depth - 1

        (u, a, gm, kvc, ks, kw, qt, qrt, kvct, kvst, kvwt, vst, vwt, gt) = _proj_prompt(
            xp, lw["g"], lw["w_rm"], lw["w_fm"], rope_pt, bp, sp, tm)
        lohi = _cmp_lohi_dense(kvc, lw["w1p"], sp).reshape(bp, nch_p, 2048)
        kvcb, vct = _cmp_fin(lohi, jnp.zeros((bp, 1, 2048), F32), lw["bias"], lw["w2p"])
        onsa_t = _attn_prompt(qt, qrt, gt, kvcb, vct, ks, vst, kw, vwt, ct_p, et_p, bp, sp, tq, kc_sel, kc_win)
        pooled, cact = _mixpre(u, None, a, None, lw["cw"], lw["cb"], lw["lg"], lw["lb"], tm, sp // tm, 0)
        xp = _dense(xp, pooled, onsa_t, cact, gm, lw["dense"], tm, final, seq=sp)
        outs["cmp_p"].append(tok_major(kvct))
        outs["slc_p"].append(tok_major(kvst))
        outs["win_p"].append(tok_major(kvwt[:, -min(WINDOW, sp):]))
        outs["pool_p"].append(u.reshape(bp, sp, 512)[:, -(POOL_PREFIX - 1):])
        outs["conv_p"].append(a.reshape(bp, sp, 512)[:, -(CONV_WIDTH - 1):])

        u, q, qr, kvc, kvs, kvw, gn, a, gm = _proj_sample(xs, lw["g"], lw["w_s"], rope_s)
        pad8 = lambda z: jnp.concatenate([z.reshape(bs, ts, 512), jnp.zeros((bs, 16 - ts, 512), F32)], axis=1)
        lohi = _cmp_lohi_paged(cache_cmp_t, l, page_table, lw["w1p"], pg)
        lohi_new = _cmp_lohi_dense(pad8(kvc).reshape(bs * 16, 512), lw["w1p"], bs * 16)
        kvcb, _ = _cmp_fin(lohi, lohi_new.reshape(bs, 1, 2048), lw["bias"], lw["w2p"])
        q3, qr3 = q.reshape(bs, ts, 1024), qr.reshape(bs, ts, 1024)
        gn4 = gn.reshape(KV_HEADS, bs, ts, 128)
        o_part, sel = _samp_a(q3, qr3, gn4, kvcb, win_t, l, pad8(kvw), ct_s, past)
        onsa = _samp_b(cache_slc_t, l, page_table, qr3, sel, e_s, gn4, pad8(kvs), o_part, pg)
        u_state = jnp.concatenate([jnp.zeros((bs, 1, 512), F32), state_pool[l]], axis=1)
        a_state = jnp.concatenate([jnp.zeros((bs, 2, 512), F32), state_conv[l]], axis=1)
        pooled, cact = _mixpre(u.reshape(bs, ts, 512), u_state, a.reshape(bs, ts, 512), a_state,
                               lw["cw"], lw["cb"], lw["lg"], lw["lb"], ts, 1, past)
        xs = _dense(xs, pooled.reshape(ms, 512), onsa.reshape(ms, 1024), cact.reshape(ms, 512), gm, lw["dense"],
                    ms, final)
        outs["cmp_s"].append(kvc.reshape(bs, ts, *kvshape))
        outs["slc_s"].append(kvs.reshape(bs, ts, *kvshape))
        kvw_t = jnp.transpose(kvw.reshape(bs, ts, *kvshape), (0, 2, 3, 4, 1))
        win_new_t = jnp.concatenate([win_t[l], kvw_t], axis=-1)[..., -wb:]
        outs["win_s"].append(jnp.transpose(win_new_t, (0, 4, 1, 2, 3)))
        outs["pool_s"].append(jnp.concatenate([state_pool[l], u.reshape(bs, ts, 512)], axis=1)[:, -(POOL_PREFIX - 1):])
        outs["conv_s"].append(jnp.concatenate([state_conv[l], a.reshape(bs, ts, 512)], axis=1)[:, -(CONV_WIDTH - 1):])

    st = lambda n: jnp.stack(outs[n])
    return (xp.reshape(bp, sp, d), xs.reshape(bs, ts, d), st("cmp_p"), st("cmp_s"), st("slc_p"), st("slc_s"),
            st("win_p"), st("win_s"), st("pool_p"), st("pool_s"), st("conv_p"), st("conv_s"))
```

```python
import functools

import jax
import jax.numpy as jnp
from jax import lax
from jax.experimental import pallas as pl
from jax.experimental.pallas import tpu as pltpu

F32 = jnp.float32
BF16 = jnp.bfloat16

HEAD_DIM = 64
N_HEADS = 16
KV_HEADS = 4
GROUP = N_HEADS // KV_HEADS
ROPE_DIM = HEAD_DIM // 4
ROPE_HALF = ROPE_DIM // 2
ROPE_THETA = 500000.0
CMP_BLOCK = 32
CMP_STRIDE = 16
SLC_BLOCK = 64
SLC_TOP = 16
WINDOW = 512
PAGE_SIZE = 128
POOL_WINDOWS = (2, 4, 8, 16)
POOL_PREFIX = 16
CONV_WIDTH = 31
CONV_PREFIX = 32
SM_SCALE = HEAD_DIM ** -0.5
LOG2E = 1.4426950408889634
NEG = -1e30
RANK_UNROLL = 4
LANES = 128
GATE_COLS = 128
GATE_ROWS = 16
VMEM_LIMIT = 56 * 1024 * 1024


def _cparams(sem):
    return pltpu.CompilerParams(dimension_semantics=sem, vmem_limit_bytes=VMEM_LIMIT)


def _const_spec(shape):
    n = len(shape)
    return pl.BlockSpec(shape, lambda *a: (0,) * n, pipeline_mode=pl.Buffered(1))


def _nt(a, b):
    return lax.dot_general(a, b, (((1,), (1,)), ((), ())), preferred_element_type=F32)


def _mm(a, b):
    return jnp.dot(a, b, preferred_element_type=F32)


def _rms(x, g):
    return x * lax.rsqrt(jnp.mean(x * x, axis=-1, keepdims=True) + 1e-6) * g


def _rope128(slab, c, s1, s2):
    return slab * c + pltpu.roll(slab, LANES - ROPE_HALF, 1) * s1 + pltpu.roll(slab, ROPE_HALF, 1) * s2


def _proj_sample_kernel(x_ref, g_ref, w_ref, rope_ref, u_ref, q_ref, qr_ref, kvc_ref, kvs_ref, kvw_ref,
                        gn_ref, a_ref, gm_ref):
    x = x_ref[...]
    hb = _rms(x, g_ref[...]).astype(BF16)

    def mm(c0, n):
        return _mm(hb, w_ref[:, c0:c0 + n])

    c = rope_ref[:, 0:128]
    s1 = rope_ref[:, 128:256]
    s2 = rope_ref[:, 256:384]
    u_ref[...] = mm(0, 512)
    q = mm(512, 1024)
    q_ref[...] = (q * SM_SCALE).astype(BF16)
    for i in range(8):
        sl = slice(i * 128, (i + 1) * 128)
        qr_ref[:, sl] = (_rope128(q[:, sl], c, s1, s2) * SM_SCALE).astype(BF16)
    kv = mm(1536, 1536)
    kvc_ref[...] = kv[:, 0:512]
    for off, o_ref in ((512, kvs_ref), (1024, kvw_ref)):
        for i in range(2):
            o_ref[:, i * 128:(i + 1) * 128] = _rope128(kv[:, off + i * 128:off + (i + 1) * 128], c, s1, s2)
        o_ref[:, 256:512] = kv[:, off + 256:off + 512]
    gz = jax.nn.sigmoid(mm(3072, GATE_COLS))
    gn_ref[0] = gz
    for k in range(1, KV_HEADS):
        gn_ref[k] = pltpu.roll(gz, LANES - 3 * GROUP * k, 1)
    ci = mm(3072 + GATE_COLS, 1024)
    a_ref[...] = ci[:, :512] * jax.nn.sigmoid(ci[:, 512:])
    for i in range(3):
        gm_ref[:, i * 1024:(i + 1) * 1024] = jax.nn.sigmoid(mm(4096 + GATE_COLS + i * 1024, 1024))


def _proj_sample(x2, g, w, rope):
    m = x2.shape[0]
    outs = [((m, 512), F32), ((m, 1024), BF16), ((m, 1024), BF16), ((m, 512), F32), ((m, 512), F32),
            ((m, 512), F32), ((KV_HEADS, m, 128), F32), ((m, 512), F32), ((m, 3072), F32)]
    full = lambda shp: pl.BlockSpec(shp, lambda i: (0,) * len(shp))
    return pl.pallas_call(
        _proj_sample_kernel,
        grid=(1,),
        in_specs=[full(x2.shape), _const_spec(g.shape), _const_spec(w.shape), full(rope.shape)],
        out_specs=[full(s) for s, _ in outs],
        out_shape=[jax.ShapeDtypeStruct(s, d) for s, d in outs],
        compiler_params=_cparams(("arbitrary",)),
        name="proj_sample",
    )(x2, g, w, rope)


def _rope_t(z, cos, sin, n_heads):
    out = []
    for h in range(n_heads):
        x1 = z[h * 64:h * 64 + ROPE_HALF]
        x2 = z[h * 64 + ROPE_HALF:h * 64 + ROPE_DIM]
        out += [x1 * cos - x2 * sin, x2 * cos + x1 * sin, z[h * 64 + ROPE_DIM:(h + 1) * 64]]
    return jnp.concatenate(out, axis=0)


def _proj_prompt_kernel(x_ref, g_ref, w_ref, wt_ref, rope_ref, u_ref, a_ref, gm_ref, kvc_ref, ks_ref, kw_ref,
                        qt_ref, qrt_ref, kvct_ref, kvst_ref, kvwt_ref, vst_ref, vwt_ref, gt_ref):
    hb = _rms(x_ref[...], g_ref[...]).astype(BF16)

    def mm(c0, n):
        return _mm(hb, w_ref[:, c0:c0 + n])

    def mt(r0, n):
        return _nt(wt_ref[r0:r0 + n, :], hb)

    u_ref[...] = mm(0, 512)
    kvc_ref[...] = mm(512, 512)
    ci = mm(1024, 1024)
    a_ref[...] = ci[:, :512] * jax.nn.sigmoid(ci[:, 512:])
    for i in range(3):
        gm_ref[:, i * 1024:(i + 1) * 1024] = jax.nn.sigmoid(mm(2048 + i * 1024, 1024))
    cos = rope_ref[0:ROPE_HALF, :]
    sin = rope_ref[ROPE_HALF:ROPE_DIM, :]
    qt = mt(0, 1024) * (SM_SCALE * LOG2E)
    qt_ref[...] = qt.astype(BF16)
    qrt_ref[...] = _rope_t(qt, cos, sin, N_HEADS).astype(BF16)
    kvct_ref[...] = mt(1024, 512)
    for off, kvt_ref, vt_ref, k_ref in ((1536, kvst_ref, vst_ref, ks_ref), (2048, kvwt_ref, vwt_ref, kw_ref)):
        kt = _rope_t(mt(off, 256), cos, sin, KV_HEADS)
        vt = mt(off + 256, 256)
        kvt_ref[0:256, :] = kt
        kvt_ref[256:512, :] = vt
        vt_ref[...] = vt.astype(BF16)
        k_ref[...] = kt.T.astype(BF16)
    gt_ref[...] = jax.nn.sigmoid(mt(2560, KV_HEADS * GATE_ROWS))


def _proj_prompt(x2, g, w, wt, rope_t, bsz, seq, tm):
    m = x2.shape[0]
    nt = seq // tm
    row = lambda c: pl.BlockSpec((tm, c), lambda i: (i, 0))
    fm = lambda r: pl.BlockSpec((r, tm), lambda i: (i // nt, i % nt))
    outs = [((m, 512), F32, row(512)), ((m, 512), F32, row(512)), ((m, 3072), F32, row(3072)),
            ((m, 512), F32, row(512)), ((m, 256), BF16, row(256)), ((m, 256), BF16, row(256)),
            ((bsz * 1024, seq), BF16, fm(1024)), ((bsz * 1024, seq), BF16, fm(1024)),
            ((bsz * 512, seq), F32, fm(512)), ((bsz * 512, seq), F32, fm(512)), ((bsz * 512, seq), F32, fm(512)),
            ((bsz * 256, seq), BF16, fm(256)), ((bsz * 256, seq), BF16, fm(256)),
            ((bsz * KV_HEADS * GATE_ROWS, seq), F32, fm(KV_HEADS * GATE_ROWS))]
    return pl.pallas_call(
        _proj_prompt_kernel,
        grid=(m // tm,),
        in_specs=[row(1024), _const_spec(g.shape), _const_spec(w.shape), _const_spec(wt.shape),
                  pl.BlockSpec((ROPE_DIM, tm), lambda i: (0, i % nt))],
        out_specs=[o[2] for o in outs],
        out_shape=[jax.ShapeDtypeStruct(o[0], o[1]) for o in outs],
        compiler_params=_cparams(("parallel",)),
        name="proj_prompt",
    )(x2, g, w, wt, rope_t)


def _cmp_lohi_matmuls(load, w_ref, o_ref):
    for c in range(2):
        for pr in range(2):
            slab = c * 2 + pr
            lhs = jnp.concatenate([load(slab, s) for s in range(CMP_STRIDE)], axis=1)
            o_ref[:, slab * 512:(slab + 1) * 512] = _mm(lhs.astype(BF16), w_ref[c])


def _cmp_lohi_dense_kernel(x_ref, w_ref, o_ref, *, nch):
    _cmp_lohi_matmuls(lambda slab, s: x_ref[pl.ds(4 * s + slab, nch, stride=4 * CMP_STRIDE), :], w_ref, o_ref)


def _cmp_lohi_dense(x2, w1p, ntok):
    m = x2.shape[0]
    nch = ntok // CMP_STRIDE
    return pl.pallas_call(
        functools.partial(_cmp_lohi_dense_kernel, nch=nch),
        grid=(m // ntok,),
        in_specs=[pl.BlockSpec((ntok * 4, 128), lambda i: (i, 0)), _const_spec(w1p.shape)],
        out_specs=pl.BlockSpec((nch, 2048), lambda i: (i, 0)),
        out_shape=jax.ShapeDtypeStruct((m // CMP_STRIDE, 2048), F32),
        compiler_params=_cparams(("parallel",)),
        name="cmp_lohi",
    )(x2.reshape(m * 4, 128), w1p)


def _cmp_lohi_paged_kernel(pt_ref, *refs, n_pages):
    del pt_ref
    pages = refs[:n_pages]
    w_ref, o_ref, x_scr = refs[n_pages:]
    for i, p in enumerate(pages):
        for c in range(2):
            xt = p[c].reshape(KV_HEADS * HEAD_DIM, PAGE_SIZE).T
            for pr in range(2):
                x_scr[c * 2 + pr, i * PAGE_SIZE:(i + 1) * PAGE_SIZE, :] = xt[:, pr * 128:(pr + 1) * 128]
    nch = n_pages * (PAGE_SIZE // CMP_STRIDE)
    _cmp_lohi_matmuls(lambda slab, s: x_scr[slab, pl.ds(s, nch, stride=CMP_STRIDE), :], w_ref, o_ref)


def _cmp_lohi_paged(cache_t, layer, page_table, w1p, pg):
    b, n_pages = page_table.shape
    nch = pg * (PAGE_SIZE // CMP_STRIDE)

    def page_spec(i):
        return pl.BlockSpec((None, None, 2, KV_HEADS, HEAD_DIM, PAGE_SIZE),
                            lambda bb, s, pt: (layer, pt[bb, s * pg + i], 0, 0, 0, 0))

    return pl.pallas_call(
        functools.partial(_cmp_lohi_paged_kernel, n_pages=pg),
        grid_spec=pltpu.PrefetchScalarGridSpec(
            num_scalar_prefetch=1,
            grid=(b, n_pages // pg),
            in_specs=[page_spec(i) for i in range(pg)] + [_const_spec(w1p.shape)],
            out_specs=pl.BlockSpec((None, nch, 2048), lambda bb, s, pt: (bb, s, 0)),
            scratch_shapes=[pltpu.VMEM((4, pg * PAGE_SIZE, 128), F32)],
        ),
        out_shape=jax.ShapeDtypeStruct((b, n_pages * (PAGE_SIZE // CMP_STRIDE), 2048), F32),
        compiler_params=_cparams(("parallel", "parallel")),
        name="cmp_lohi_paged",
    )(page_table, *([cache_t] * pg), w1p)


def _cmp_fin_kernel(lohi_ref, nxt_ref, bias_ref, w2_ref, o_ref, vt_ref, *, nch):
    row = lax.broadcasted_iota(jnp.int32, (nch, 128), 0)
    for c in range(2):
        for pr in range(2):
            base = (c * 2 + pr) * 512
            hs = []
            for kk in range(2):
                lo = lohi_ref[:, base + kk * 256:base + kk * 256 + 128]
                hi = lohi_ref[:, base + kk * 256 + 128:base + kk * 256 + 256]
                nx = nxt_ref[:, base + kk * 256 + 128:base + kk * 256 + 256]
                hin = jnp.where(row == nch - 1, nx, pltpu.roll(hi, nch - 1, 0))
                hs.append(jax.nn.gelu(lo + hin + bias_ref[c:c + 1, :]))
            hid = jnp.concatenate(hs, axis=1).astype(BF16)
            out = _mm(hid, w2_ref[c])
            o_ref[:, c * 256 + pr * 128:c * 256 + (pr + 1) * 128] = out.astype(BF16)
            if c == 1:
                vt_ref[pr * 128:(pr + 1) * 128, :] = out.T.astype(BF16)


def _cmp_fin(lohi, nxt, bias, w2p):
    b, nch, _ = lohi.shape
    return pl.pallas_call(
        functools.partial(_cmp_fin_kernel, nch=nch),
        grid=(b,),
        in_specs=[pl.BlockSpec((None, nch, 2048), lambda i: (i, 0, 0)),
                  pl.BlockSpec((None, 1, 2048), lambda i: (i, 0, 0)),
                  _const_spec(bias.shape), _const_spec(w2p.shape)],
        out_specs=[pl.BlockSpec((None, nch, 512), lambda i: (i, 0, 0)),
                   pl.BlockSpec((None, 256, nch), lambda i: (i, 0, 0))],
        out_shape=[jax.ShapeDtypeStruct((b, nch, 512), BF16), jax.ShapeDtypeStruct((b, 256, nch), BF16)],
        compiler_params=_cparams(("parallel",)),
        name="cmp_fin",
    )(lohi, nxt, bias, w2p)


def _split3(x):
    x1 = x.astype(BF16)
    r1 = x - x1.astype(F32)
    x2 = r1.astype(BF16)
    x3 = (r1 - x2.astype(F32)).astype(BF16)
    return x1, x2, x3


def _block_rank_select(imp_ref, n_blocks, valid):
    x = imp_ref[...]
    jrow = lax.broadcasted_iota(jnp.int32, x.shape, 0)
    assert x.shape[0] % RANK_UNROLL == 0

    def body(i, cnt):
        for u in range(RANK_UNROLL):
            jp = i * RANK_UNROLL + u
            r = imp_ref[pl.ds(jp, 1), :]
            cnt = cnt + jnp.where(jrow > jp, jnp.where(r >= x, 1.0, 0.0), jnp.where(r > x, 1.0, 0.0))
        return cnt

    cnt = lax.fori_loop(0, (n_blocks + RANK_UNROLL - 1) // RANK_UNROLL, body, jnp.zeros(x.shape, F32))
    return jnp.where((cnt < SLC_TOP) & valid, 1.0, 0.0)


def _update_t(m_ref, acc_ref, sl, s, v_ext):
    m_old = m_ref[:, sl]
    m_new = jnp.maximum(m_old, s.max(axis=0, keepdims=True))
    alpha = jnp.exp2(m_old - m_new)
    p = jnp.exp2(s - m_new).astype(BF16)
    acc_ref[:, sl] = acc_ref[:, sl] * alpha + _mm(v_ext, p)
    m_ref[:, sl] = m_new


def _finish_t(acc_ref, sl):
    return acc_ref[0:128, sl] * (1.0 / jnp.maximum(acc_ref[128:129, sl], 1e-30))


def _attn_prompt_kernel(qt_ref, qrt_ref, gt_ref, kc_ref, vct_ref, ks_ref, vst_ref, kw_ref, vwt_ref, ct_ref, et_ref,
                        o_ref, imp_scr, lim_scr, qa_scr, oc_scr, m_scr, acc_scr, mw_scr, accw_scr, sa_scr, sb_scr,
                        *, tq, nch, n_slc, kc_sel, kc_win):
    k = pl.program_id(1)
    t0 = pl.program_id(2) * tq
    par = k % 2
    r = GROUP * tq

    def stack(ref):
        z = jnp.zeros((HEAD_DIM, tq), BF16)
        cols = []
        for g in range(GROUP):
            qg = ref[g * HEAD_DIM:(g + 1) * HEAD_DIM, :]
            cols.append(jnp.where(par == 0, jnp.concatenate([qg, z], axis=0), jnp.concatenate([z, qg], axis=0)))
        return jnp.concatenate(cols, axis=1)

    q4 = stack(qt_ref)
    qr4 = stack(qrt_ref)
    pos = t0 + lax.broadcasted_iota(jnp.int32, (1, tq), 1)

    vis = lax.broadcasted_iota(jnp.int32, (nch, tq), 0) <= ((pos - (CMP_BLOCK - 1)) >> 4)
    kc = kc_ref[...]
    vct = vct_ref[...]
    p_sum = None
    s_c = _mm(kc, q4)
    es, invs = [], []
    for g in range(GROUP):
        sg = jnp.where(vis, s_c[:, g * tq:(g + 1) * tq], NEG)
        e = jnp.where(vis, jnp.exp2(sg - sg.max(axis=0, keepdims=True)), 0.0)
        inv = 1.0 / jnp.maximum(e.sum(axis=0, keepdims=True), 1e-30)
        p_sum = e * inv if p_sum is None else p_sum + e * inv
        es.append(e.astype(BF16))
        invs.append(inv)
    oc_scr[...] = _mm(vct, jnp.concatenate(es, axis=1)) * jnp.concatenate(invs, axis=1)

    ct = ct_ref[...]
    imp = sum(_mm(ct, piece) for piece in _split3(p_sum))
    nb = imp.shape[0]
    jb = lax.broadcasted_iota(jnp.int32, (nb, tq), 0)
    cur = pos >> 6
    forced = (jb == 0) | (jb == cur) | (jb == cur - 1)
    valid = jb <= cur
    posb = jnp.broadcast_to(pos, (nb, tq))
    imp_scr[...] = jnp.where(valid, jnp.where(forced, jnp.inf, imp), -jnp.inf)

    @pl.when(t0 + tq <= SLC_TOP * SLC_BLOCK)
    def _():
        lim_scr[...] = jnp.where(valid, posb, -1)

    @pl.when(t0 + tq > SLC_TOP * SLC_BLOCK)
    def _():
        sel = _block_rank_select(imp_scr, jnp.minimum(n_slc, (t0 + tq - 1) // SLC_BLOCK + 1), valid)
        lim_scr[...] = jnp.where(sel > 0.5, posb, -1)

    bias = jnp.where(lim_scr[...] >= 0, 0.0, NEG).astype(BF16)
    bias = jnp.concatenate([bias, jnp.zeros((128 - nb, tq), BF16)], axis=0)
    qa_scr[0:128, :] = qr4
    qa_scr[128:256, :] = jnp.concatenate([bias] * GROUP, axis=1)
    m_scr[...] = jnp.full((1, r), NEG, F32)
    acc_scr[...] = jnp.zeros(acc_scr.shape, F32)
    ones_s = jnp.ones((16, kc_sel), BF16)

    def scores(c):
        k0 = pl.multiple_of(c * kc_sel, kc_sel)
        k_aug = jnp.concatenate([ks_ref[pl.ds(k0, kc_sel), :], et_ref[pl.ds(k0, kc_sel), :]], axis=1)
        return _mm(k_aug, qa_scr[...])

    def soft_pv(s_ref, c, causal):
        k0 = pl.multiple_of(c * kc_sel, kc_sel)
        s = s_ref[...]
        if causal:
            al = (k0 + lax.broadcasted_iota(jnp.int32, (kc_sel, tq), 0)) <= pos
            s = jnp.concatenate([jnp.where(al, s[:, g * tq:(g + 1) * tq], NEG) for g in range(GROUP)], axis=1)
        _update_t(m_scr, acc_scr, slice(None), s, jnp.concatenate([vst_ref[:, pl.ds(k0, kc_sel)], ones_s], axis=0))

    c_last = t0 // kc_sel
    sa_scr[...] = scores(0)

    def pair_body(i, carry):
        sb_scr[...] = scores(2 * i + 1)
        soft_pv(sa_scr, 2 * i, False)
        sa_scr[...] = scores(2 * i + 2)
        soft_pv(sb_scr, 2 * i + 1, False)
        return carry

    lax.fori_loop(0, c_last // 2, pair_body, 0)

    @pl.when(c_last % 2 == 0)
    def _():
        soft_pv(sa_scr, c_last, True)

    @pl.when(c_last % 2 == 1)
    def _():
        sb_scr[...] = scores(c_last)
        soft_pv(sa_scr, c_last - 1, False)
        soft_pv(sb_scr, c_last, True)

    mw_scr[...] = jnp.full((1, r), NEG, F32)
    accw_scr[...] = jnp.zeros(accw_scr.shape, F32)
    ones_w = jnp.ones((16, kc_win), BF16)
    k0w = jnp.maximum(t0 - WINDOW, 0)
    for i in range((WINDOW + tq) // kc_win):
        kk = pl.multiple_of(k0w + i * kc_win, tq)
        dp = pos - (kk + lax.broadcasted_iota(jnp.int32, (kc_win, tq), 0))
        al = lax.bitcast_convert_type(dp, jnp.uint32) < WINDOW
        s = _mm(kw_ref[pl.ds(kk, kc_win), :], qa_scr[0:128, :])
        s = jnp.concatenate([jnp.where(al, s[:, g * tq:(g + 1) * tq], NEG) for g in range(GROUP)], axis=1)
        _update_t(mw_scr, accw_scr, slice(None), s,
                  jnp.concatenate([vwt_ref[:, pl.ds(kk, kc_win)], ones_w], axis=0))

    gt = gt_ref[...]
    for g in range(GROUP):
        sl = slice(g * tq, (g + 1) * tq)
        og = gt[g:g + 1] * oc_scr[:, sl] + gt[GROUP + g:GROUP + g + 1] * _finish_t(acc_scr, sl) \
            + gt[2 * GROUP + g:2 * GROUP + g + 1] * _finish_t(accw_scr, sl)
        o_ref[g * HEAD_DIM:(g + 1) * HEAD_DIM, :] = jnp.where(par == 0, og[0:HEAD_DIM], og[HEAD_DIM:]).astype(BF16)


def _attn_prompt(qt, qrt, gt, kvc, vct, ks, vst, kw, vwt, ct, et, bsz, seq, tq, kc_sel, kc_win):
    nch = kvc.shape[1]
    nt = seq // tq
    n_slc = seq // SLC_BLOCK
    nb = ct.shape[0]
    r = GROUP * tq
    qspec = pl.BlockSpec((256, tq), lambda b, k, t: (b * KV_HEADS + k, t))
    kspec = pl.BlockSpec((seq, 128), lambda b, k, t: (b, k // 2))
    vspec = pl.BlockSpec((128, seq), lambda b, k, t: (b * 2 + k // 2, 0))
    return pl.pallas_call(
        functools.partial(_attn_prompt_kernel, tq=tq, nch=nch, n_slc=n_slc, kc_sel=kc_sel, kc_win=kc_win),
        grid=(bsz, KV_HEADS, nt),
        in_specs=[qspec, qspec,
                  pl.BlockSpec((GATE_ROWS, tq), lambda b, k, t: (b * KV_HEADS + k, t)),
                  pl.BlockSpec((None, nch, 128), lambda b, k, t: (b, 0, k // 2)),
                  pl.BlockSpec((None, 128, nch), lambda b, k, t: (b, k // 2, 0)),
                  kspec, vspec, kspec, vspec, _const_spec(ct.shape), _const_spec(et.shape)],
        out_specs=pl.BlockSpec((256, tq), lambda b, k, t: (b * KV_HEADS + k, t)),
        out_shape=jax.ShapeDtypeStruct((bsz * 1024, seq), BF16),
        scratch_shapes=[pltpu.VMEM((nb, tq), F32), pltpu.VMEM((nb, tq), jnp.int32), pltpu.VMEM((256, r), BF16),
                        pltpu.VMEM((128, r), F32),
                        pltpu.VMEM((1, r), F32), pltpu.VMEM((128 + 16, r), F32),
                        pltpu.VMEM((1, r), F32), pltpu.VMEM((128 + 16, r), F32),
                        pltpu.VMEM((kc_sel, r), F32), pltpu.VMEM((kc_sel, r), F32)],
        compiler_params=_cparams(("parallel", "parallel", "arbitrary")),
        name="attn_prompt",
    )(qt, qrt, gt, kvc, vct, ks, vst, kw, vwt, ct, et)


def _pad_q(qf, par, g, lane):
    slab = qf[:, (g // 2) * 128:(g // 2 + 1) * 128]
    rolled = pltpu.roll(slab, 64, 1)
    src = slab if par == g % 2 else rolled
    return jnp.where((lane >> 6) == par, src, 0.0)


def _stack_q(q_blk, par):
    qf = q_blk.astype(F32)
    lane = lax.broadcasted_iota(jnp.int32, (qf.shape[0], 128), 1)
    return jnp.concatenate([_pad_q(qf, par, g, lane) for g in range(GROUP)], axis=0).astype(BF16)


def _softmax_parts(parts, r):
    ss = []
    for s, al, _ in parts:
        ss.append(jnp.where(al[None], s.reshape(GROUP, r, s.shape[-1]), NEG))
    m = ss[0].max(axis=-1, keepdims=True)
    for s in ss[1:]:
        m = jnp.maximum(m, s.max(axis=-1, keepdims=True))
    l = jnp.zeros_like(m)
    acc = jnp.zeros((GROUP, r, 128), F32)
    ps = []
    for (_, al, pv), s in zip(parts, ss):
        p = jnp.where(al[None], jnp.exp(s - m), 0.0)
        ps.append(p)
        l = l + p.sum(axis=-1, keepdims=True)
        acc = acc + pv(p.reshape(GROUP * r, p.shape[-1]).astype(BF16)).reshape(GROUP, r, 128)
    inv = 1.0 / jnp.maximum(l, 1e-30)
    return acc * inv, [p * inv for p in ps]


def _online_update(m_ref, l_ref, acc_ref, s, allowed, pv, r):
    n = s.shape[-1]
    s3 = jnp.where(allowed[None], s.reshape(GROUP, r, n), NEG)
    m_old = m_ref[...]
    m_new = jnp.maximum(m_old, s3.max(axis=-1, keepdims=True))
    alpha = jnp.exp(m_old - m_new)
    p = jnp.where(allowed[None], jnp.exp(s3 - m_new), 0.0)
    l_ref[...] = alpha * l_ref[...] + p.sum(axis=-1, keepdims=True)
    acc_ref[...] = alpha * acc_ref[...] + pv(p.reshape(GROUP * r, n).astype(BF16)).reshape(GROUP, r, 128)
    m_ref[...] = m_new


def _place_heads(o4, par, r):
    lane = lax.broadcasted_iota(jnp.int32, (r, 128), 1)
    outs = []
    for pr in range(2):
        lo, hi = o4[2 * pr], o4[2 * pr + 1]
        lo = lo if par == 0 else pltpu.roll(lo, 64, 1)
        hi = hi if par == 1 else pltpu.roll(hi, 64, 1)
        outs.append(jnp.where(lane < 64, lo, hi))
    return jnp.concatenate(outs, axis=1)


def _gate(gn, br, r):
    return jnp.stack([gn[:, br * GROUP + g:br * GROUP + g + 1] for g in range(GROUP)], axis=0)


def _pair_t(ref, c, pr):
    return ref[c, 2 * pr:2 * pr + 2].reshape(2 * HEAD_DIM, ref.shape[-1]).astype(BF16)


def _samp_a_kernel(q_ref, qr_ref, gn_ref, kvc_ref, win_ref, kvwn_ref, ct_ref, o_ref, sel_ref, imp_scr,
                   *, t, nch, n_slc, past, wb):
    nb = ct_ref.shape[0]
    p_sums = []
    pos_c = past + lax.broadcasted_iota(jnp.int32, (t, nch), 0)
    cend = lax.broadcasted_iota(jnp.int32, (t, nch), 1) * CMP_STRIDE + (CMP_BLOCK - 1)
    tt = lax.broadcasted_iota(jnp.int32, (t, wb), 0)
    ii = lax.broadcasted_iota(jnp.int32, (t, wb), 1)
    dp_buf = wb + tt - ii
    al_buf = (dp_buf >= 0) & (dp_buf < WINDOW) & (past - wb + ii >= 0)
    nn = kvwn_ref.shape[0]
    dp_new = lax.broadcasted_iota(jnp.int32, (t, nn), 0) - lax.broadcasted_iota(jnp.int32, (t, nn), 1)
    al_new = (dp_new >= 0) & (dp_new < WINDOW)
    for k in range(KV_HEADS):
        par, pr = k % 2, k // 2
        q4 = _stack_q(q_ref[:, k * 256:(k + 1) * 256], par)
        qr4 = _stack_q(qr_ref[:, k * 256:(k + 1) * 256], par)
        ksl = slice(pr * 128, (pr + 1) * 128)
        vsl = slice(256 + pr * 128, 256 + (pr + 1) * 128)
        kc, vc = kvc_ref[:, ksl], kvc_ref[:, vsl]
        o_cmp, (p_cmp,) = _softmax_parts([(_nt(q4, kc), cend <= pos_c, lambda p, vc=vc: _mm(p, vc))], t)
        p_sums.append(p_cmp.sum(axis=0))
        kwt, vwt = _pair_t(win_ref, 0, pr), _pair_t(win_ref, 1, pr)
        kn, vn = kvwn_ref[:, ksl].astype(BF16), kvwn_ref[:, vsl].astype(BF16)
        o_win, _ = _softmax_parts([(_mm(qr4, kwt), al_buf, lambda p, vwt=vwt: _nt(p, vwt)),
                                   (_nt(qr4, kn), al_new, lambda p, vn=vn: _mm(p, vn))], t)
        gn = gn_ref[k]
        o4 = _gate(gn, 0, t) * o_cmp + _gate(gn, 2, t) * o_win
        o_ref[:, k * 256:(k + 1) * 256] = _place_heads(o4, par, t)
    p_all = jnp.concatenate(p_sums + [jnp.zeros((128 - KV_HEADS * t, nch), F32)], axis=0)
    ct = ct_ref[...]
    imp = sum(_nt(ct, piece) for piece in _split3(p_all))
    jb = lax.broadcasted_iota(jnp.int32, (nb, 128), 0)
    col = lax.broadcasted_iota(jnp.int32, (nb, 128), 1)
    cur = (past + (col & (t - 1))) >> 6
    forced = (jb == 0) | (jb == cur) | (jb == cur - 1)
    valid = jb <= cur
    imp_scr[...] = jnp.where(valid, jnp.where(forced, jnp.inf, imp), -jnp.inf)
    sel_ref[...] = _block_rank_select(imp_scr, n_slc, valid)


def _samp_a(q3, qr3, gn4, kvc, win_t, layer, kvwn3, ct, past):
    b, t, _ = q3.shape
    nch = kvc.shape[1]
    nb = ct.shape[0]
    wb = win_t.shape[-1]
    n_slc = (past + t + SLC_BLOCK - 1) // SLC_BLOCK
    per_b = lambda r, c: pl.BlockSpec((None, r, c), lambda i: (i, 0, 0))
    return pl.pallas_call(
        functools.partial(_samp_a_kernel, t=t, nch=nch, n_slc=n_slc, past=past, wb=wb),
        grid=(b,),
        in_specs=[per_b(t, 1024), per_b(t, 1024),
                  pl.BlockSpec((KV_HEADS, None, t, 128), lambda i: (0, i, 0, 0)),
                  per_b(nch, 512),
                  pl.BlockSpec((None, None, 2, KV_HEADS, HEAD_DIM, wb), lambda i: (layer, i, 0, 0, 0, 0)),
                  per_b(kvwn3.shape[1], 512), _const_spec(ct.shape)],
        out_specs=[per_b(t, 1024), per_b(nb, 128)],
        out_shape=[jax.ShapeDtypeStruct((b, t, 1024), F32), jax.ShapeDtypeStruct((b, nb, 128), F32)],
        scratch_shapes=[pltpu.VMEM((nb, 128), F32)],
        compiler_params=_cparams(("parallel",)),
        name="samp_a",
    )(q3, qr3, gn4, kvc, win_t, kvwn3, ct)


def _samp_b_kernel(pt_ref, *refs, pg, t):
    del pt_ref
    pages = refs[:pg]
    qr_ref, sel_ref, seln_ref, e_ref, gn_ref, kvn_ref, op_ref, o_ref, m_scr, l_scr, acc_scr = refs[pg:]
    s = pl.program_id(1)
    ns = pl.num_programs(1)
    nbc = sel_ref.shape[0]

    @pl.when(s == 0)
    def _():
        m_scr[...] = jnp.full(m_scr.shape, NEG, F32)
        l_scr[...] = jnp.zeros(l_scr.shape, F32)
        acc_scr[...] = jnp.zeros(acc_scr.shape, F32)

    selc = jnp.concatenate([sel_ref[...], jnp.zeros((128 - nbc, 128), F32)], axis=0).T
    msk_all = _mm(selc[:KV_HEADS * t].astype(BF16), e_ref[...])
    for k in range(KV_HEADS):
        par, pr = k % 2, k // 2
        qr4 = _stack_q(qr_ref[:, k * 256:(k + 1) * 256], par)
        kt = jnp.concatenate([_pair_t(p, 0, pr) for p in pages], axis=1)
        vt = jnp.concatenate([_pair_t(p, 1, pr) for p in pages], axis=1)
        _online_update(m_scr.at[k], l_scr.at[k], acc_scr.at[k], _mm(qr4, kt), msk_all[k * t:(k + 1) * t] > 0.5,
                       lambda p, vt=vt: _nt(p, vt), t)

    @pl.when(s == ns - 1)
    def _():
        nn = kvn_ref.shape[0]
        dp = lax.broadcasted_iota(jnp.int32, (t, nn), 0) - lax.broadcasted_iota(jnp.int32, (t, nn), 1)
        seln = jnp.concatenate([seln_ref[...], jnp.zeros((128 - 8, 128), F32)], axis=0).T
        for k in range(KV_HEADS):
            par, pr = k % 2, k // 2
            qr4 = _stack_q(qr_ref[:, k * 256:(k + 1) * 256], par)
            kk = kvn_ref[:, pr * 128:(pr + 1) * 128].astype(BF16)
            vv = kvn_ref[:, 256 + pr * 128:256 + (pr + 1) * 128].astype(BF16)
            al = (dp >= 0) & (seln[k * t:(k + 1) * t, 0:1] > 0.5)
            _online_update(m_scr.at[k], l_scr.at[k], acc_scr.at[k], _nt(qr4, kk), al, lambda p, vv=vv: _mm(p, vv), t)
            o_sel = acc_scr[k] * (1.0 / jnp.maximum(l_scr[k], 1e-30))
            o4 = _gate(gn_ref[k], 1, t) * o_sel
            o_ref[:, k * 256:(k + 1) * 256] = (op_ref[:, k * 256:(k + 1) * 256] + _place_heads(o4, par, t)).astype(BF16)


def _samp_b(cache_t, layer, page_table, qr3, sel, e, gn4, kvn3, o_part, pg):
    b, n_pages = page_table.shape
    t = qr3.shape[1]
    ns = n_pages // pg
    nbc = pg * (PAGE_SIZE // SLC_BLOCK)
    n_past_blk = n_pages * (PAGE_SIZE // SLC_BLOCK)

    def page_spec(i):
        return pl.BlockSpec((None, None, 2, KV_HEADS, HEAD_DIM, PAGE_SIZE),
                            lambda bb, s, pt: (layer, pt[bb, s * pg + i], 0, 0, 0, 0))

    per_b = lambda r, c: pl.BlockSpec((None, r, c), lambda bb, s, pt: (bb, 0, 0))
    return pl.pallas_call(
        functools.partial(_samp_b_kernel, pg=pg, t=t),
        grid_spec=pltpu.PrefetchScalarGridSpec(
            num_scalar_prefetch=1,
            grid=(b, ns),
            in_specs=[page_spec(i) for i in range(pg)] + [
                per_b(t, 1024),
                pl.BlockSpec((None, nbc, 128), lambda bb, s, pt: (bb, s, 0)),
                pl.BlockSpec((None, 8, 128), lambda bb, s, pt: (bb, n_past_blk // 8, 0)),
                _const_spec(e.shape),
                pl.BlockSpec((KV_HEADS, None, t, 128), lambda bb, s, pt: (0, bb, 0, 0)),
                per_b(kvn3.shape[1], 512), per_b(t, 1024)],
            out_specs=per_b(t, 1024),
            scratch_shapes=[pltpu.VMEM((KV_HEADS, GROUP, t, 1), F32), pltpu.VMEM((KV_HEADS, GROUP, t, 1), F32),
                            pltpu.VMEM((KV_HEADS, GROUP, t, 128), F32)],
        ),
        out_shape=jax.ShapeDtypeStruct((b, t, 1024), BF16),
        compiler_params=_cparams(("parallel", "arbitrary")),
        name="samp_b",
    )(page_table, *([cache_t] * pg), qr3, sel, sel, e, gn4, kvn3, o_part)


def _mixpre_kernel(u_ref, up_ref, a_ref, ap_ref, cw_ref, cb_ref, lg_ref, lb_ref, pool_ref, cact_ref,
                   u_scr, a_scr, *, tm, tiles_per_seq, pos_base, has_state):
    ti = pl.program_id(0) % tiles_per_seq
    pos0 = pos_base + ti * tm
    keep = 1.0 if has_state else jnp.where(ti == 0, 0.0, 1.0)
    u = u_ref[...]
    u_scr[0:POOL_PREFIX, :] = up_ref[...] * keep
    u_scr[POOL_PREFIX:, :] = u
    a_scr[0:CONV_PREFIX, :] = ap_ref[...] * keep
    a_scr[CONV_PREFIX:, :] = a_ref[...]
    pos = pos0 + lax.broadcasted_iota(jnp.int32, (tm, 128), 0)
    for gi, w in enumerate(POOL_WINDOWS):
        sl = slice(gi * 128, (gi + 1) * 128)
        acc = u[:, sl]
        for d in range(1, w):
            acc = acc + u_scr[pl.ds(POOL_PREFIX - d, tm), sl]
        cnt = jnp.minimum(pos + 1, w).astype(F32)
        pool_ref[:, sl] = (acc / cnt - u[:, sl]).astype(BF16)
    y = jnp.zeros((tm, 512), F32) + cb_ref[...]
    for j in range(CONV_WIDTH):
        y = y + a_scr[pl.ds(CONV_PREFIX - (CONV_WIDTH - 1) + j, tm), :] * cw_ref[j:j + 1, :]
    mu = jnp.mean(y, axis=-1, keepdims=True)
    yc = y - mu
    var = jnp.mean(yc * yc, axis=-1, keepdims=True)
    yn = yc * lax.rsqrt(var + 1e-5) * lg_ref[...] + lb_ref[...]
    cact_ref[...] = (yn * jax.nn.sigmoid(yn)).astype(BF16)


def _mixpre(u, u_state, a, a_state, cw, cb, lg, lb, tm, tiles_per_seq, pos_base):
    has_state = u_state is not None
    if has_state:
        up, ap = u_state, a_state
        n_tiles = u.shape[0]
        up_spec = pl.BlockSpec((None, POOL_PREFIX, 512), lambda i: (i, 0, 0))
        ap_spec = pl.BlockSpec((None, CONV_PREFIX, 512), lambda i: (i, 0, 0))
        row = pl.BlockSpec((None, tm, 512), lambda i: (i, 0, 0))
        out_shape = (n_tiles, tm, 512)
    else:
        up, ap = u, a
        n_tiles = u.shape[0] // tm
        up_spec = pl.BlockSpec((POOL_PREFIX, 512), lambda i: (jnp.maximum(i * (tm // POOL_PREFIX) - 1, 0), 0))
        ap_spec = pl.BlockSpec((CONV_PREFIX, 512), lambda i: (jnp.maximum(i * (tm // CONV_PREFIX) - 1, 0), 0))
        row = pl.BlockSpec((tm, 512), lambda i: (i, 0))
        out_shape = (u.shape[0], 512)
    return pl.pallas_call(
        functools.partial(_mixpre_kernel, tm=tm, tiles_per_seq=tiles_per_seq, pos_base=pos_base, has_state=has_state),
        grid=(n_tiles,),
        in_specs=[row, up_spec, row, ap_spec, _const_spec(cw.shape), _const_spec(cb.shape),
                  _const_spec(lg.shape), _const_spec(lb.shape)],
        out_specs=[row, row],
        out_shape=[jax.ShapeDtypeStruct(out_shape, BF16), jax.ShapeDtypeStruct(out_shape, BF16)],
        scratch_shapes=[pltpu.VMEM((tm + POOL_PREFIX, 512), F32), pltpu.VMEM((tm + CONV_PREFIX, 512), F32)],
        compiler_params=_cparams(("parallel",)),
        name="mixpre",
    )(u, up, a, ap, cw, cb, lg, lb)


def _dense_kernel(x_ref, pool_ref, onsa_ref, cact_ref, gm_ref, pw_ref, ps_ref, wbp_ref, wbn_ref, wbc_ref, wo_ref,
                  nf_ref, wg_ref, wu_ref, wd_ref, nfin_ref, o_ref, *, final, ff_chunk, onsa_t):
    op = jnp.concatenate([_mm(pool_ref[:, g * 128:(g + 1) * 128], pw_ref[g]) for g in range(4)], axis=1)
    op = (op * ps_ref[...]).astype(BF16)
    onsa = onsa_ref[...].astype(F32).T.astype(BF16) if onsa_t else onsa_ref[...]
    m = gm_ref[:, 0:1024] * _mm(op, wbp_ref[...])
    m = m + gm_ref[:, 1024:2048] * _mm(onsa, wbn_ref[...])
    m = m + gm_ref[:, 2048:3072] * _mm(cact_ref[...], wbc_ref[...])
    x1 = x_ref[...] + _mm(m.astype(BF16), wo_ref[...])
    hb = _rms(x1, nf_ref[...]).astype(BF16)
    dff = wg_ref.shape[1]
    acc = x1
    for c0 in range(0, dff, ff_chunk):
        gt = _mm(hb, wg_ref[:, c0:c0 + ff_chunk])
        up = _mm(hb, wu_ref[:, c0:c0 + ff_chunk])
        acc = acc + _mm((gt * jax.nn.sigmoid(gt) * up).astype(BF16), wd_ref[c0:c0 + ff_chunk, :])
    o_ref[...] = _rms(acc, nfin_ref[...]) if final else acc


def _dense(x2, pooled, onsa, cact, gm, weights, tm, final, seq=None):
    m = x2.shape[0]
    row = lambda c: pl.BlockSpec((tm, c), lambda i: (i, 0))
    if seq is None:
        onsa_spec = row(1024)
    else:
        nt = seq // tm
        onsa_spec = pl.BlockSpec((1024, tm), lambda i: (i // nt, i % nt))
    dff = weights[8].shape[1]
    ff_chunk = dff // 2 if (dff // 2) % 128 == 0 else dff
    return pl.pallas_call(
        functools.partial(_dense_kernel, final=final, ff_chunk=ff_chunk, onsa_t=seq is not None),
        grid=(m // tm,),
        in_specs=[row(1024), row(512), onsa_spec, row(512), row(3072)] + [_const_spec(w.shape) for w in weights],
        out_specs=row(1024),
        out_shape=jax.ShapeDtypeStruct((m, 1024), F32),
        compiler_params=_cparams(("parallel",)),
        name="dense",
    )(x2, pooled, onsa, cact, gm, *weights)


def _rope_angles(pos):
    inv = ROPE_THETA ** (-jnp.arange(ROPE_HALF, dtype=F32) * (2.0 / ROPE_DIM))
    return pos.astype(F32)[:, None] * inv[None, :]


def _rope_table(pos):
    ang = _rope_angles(pos)
    cos, sin = jnp.cos(ang), jnp.sin(ang)
    t = pos.shape[0]
    zeros = jnp.zeros((t, HEAD_DIM - ROPE_DIM), F32)
    zh = jnp.zeros((t, ROPE_HALF), F32)
    c = jnp.concatenate([cos, cos, zeros + 1.0], axis=1)
    s1 = jnp.concatenate([-sin, zh, zeros], axis=1)
    s2 = jnp.concatenate([zh, sin, zeros], axis=1)
    return jnp.concatenate([c, c, s1, s1, s2, s2], axis=1)


def _rope_table_t(pos):
    ang = _rope_angles(pos)
    return jnp.concatenate([jnp.cos(ang).T, jnp.sin(ang).T], axis=0)


def _cmp_to_slc_t(n_cmp_rows, nb):
    i0 = jnp.arange(n_cmp_rows)[None, :] * CMP_STRIDE
    j0 = jnp.arange(nb)[:, None] * SLC_BLOCK
    ov = jnp.clip(jnp.minimum(i0 + CMP_BLOCK, j0 + SLC_BLOCK) - jnp.maximum(i0, j0), 0, None)
    return (ov.astype(F32) / CMP_BLOCK).astype(BF16)


def _expand_matrix(n_rows, n_keys):
    return (jnp.arange(n_rows)[:, None] == jnp.arange(n_keys)[None, :] // SLC_BLOCK).astype(BF16)


def _layer_weights(l, norm_mix, w_in, pool_w, pool_scale, cmp_w1, cmp_pe, cmp_w2, conv_w, conv_b, conv_ln_g,
                   conv_ln_b, w_br_pool, w_br_nsa, w_br_conv, w_out, norm_ffn, w_gate, w_up, w_down, norm_final):
    w = w_in[l]
    s_q, s_kv, s_gate = 512, 512 + 1024, 512 + 1024 + 1536
    n_g = 3 * N_HEADS
    order = [(k, br, g) for k in range(KV_HEADS) for br in range(3) for g in range(GROUP)]
    idx = jnp.array([s_gate + (k * GROUP + g) * 3 + br for k, br, g in order])
    w_s = jnp.concatenate([w[:, :s_gate], w[:, idx], jnp.zeros((w.shape[0], GATE_COLS - n_g), w.dtype),
                           w[:, s_gate + n_g:]], axis=1).astype(BF16)
    w_rm = jnp.concatenate([w[:, :s_q], w[:, s_kv:s_kv + 512], w[:, s_gate + n_g:]], axis=1).astype(BF16)
    gate_rows = jnp.zeros((KV_HEADS * GATE_ROWS, w.shape[0]), w.dtype)
    gate_rows = gate_rows.at[jnp.array([k * GATE_ROWS + br * GROUP + g for k, br, g in order])].set(w[:, idx].T)
    w_fm = jnp.concatenate([w[:, s_q:s_gate].T, gate_rows], axis=0).astype(BF16)
    w1 = cmp_w1[l]
    eye2 = jnp.eye(2, dtype=F32)
    w1h = w1.reshape(2, 2, CMP_STRIDE, HEAD_DIM, 128)
    w1p = jnp.einsum('chsde,kq->cskdqhe', w1h, eye2).reshape(2, CMP_STRIDE * 128, 512).astype(BF16)
    w2p = jnp.einsum('ced,kq->ckeqd', cmp_w2[l], eye2).reshape(2, 256, 128).astype(BF16)
    bias = jnp.einsum('cpd,cpde->ce', cmp_pe[l], w1, precision=lax.Precision.HIGHEST)
    cw = jnp.concatenate([conv_w[l], jnp.zeros((1, conv_w.shape[2]), F32)], axis=0)
    dense_w = (pool_w[l].astype(BF16), pool_scale[l][None], w_br_pool[l].astype(BF16), w_br_nsa[l].astype(BF16),
               w_br_conv[l].astype(BF16), w_out[l].astype(BF16), norm_ffn[l][None], w_gate[l].astype(BF16),
               w_up[l].astype(BF16), w_down[l].astype(BF16), norm_final[None])
    return dict(g=norm_mix[l][None], w_s=w_s, w_rm=w_rm, w_fm=w_fm, w1p=w1p, w2p=w2p, bias=bias, cw=cw,
                cb=conv_b[l][None], lg=conv_ln_g[l][None], lb=conv_ln_b[l][None], dense=dense_w)


def kernel(x_prompt, x_sample, cache_cmp_kv, cache_slc_kv, state_win_kv, state_pool, state_conv, page_table,
           norm_mix, w_in, pool_w, pool_scale, cmp_w1, cmp_pe, cmp_w2, conv_w, conv_b, conv_ln_g, conv_ln_b,
           w_br_pool, w_br_nsa, w_br_conv, w_out, norm_ffn, w_gate, w_up, w_down, norm_final):
    bp, sp, d = x_prompt.shape
    bs, ts, _ = x_sample.shape
    depth = w_in.shape[0]
    n_pages = page_table.shape[1]
    past = n_pages * PAGE_SIZE
    wb = state_win_kv.shape[2]
    assert d == 1024 and sp % 512 == 0 and sp >= WINDOW + 256 and ts == 8 and n_pages % 4 == 0

    tm = 256
    tq = 256
    kc_sel = 512
    kc_win = WINDOW + tq
    pg = 16 if n_pages % 16 == 0 else 4
    pg_cmp = 32 if n_pages % 32 == 0 else pg
    mp, ms = bp * sp, bs * ts

    rope_pt = _rope_table_t(jnp.arange(sp))
    rope_s = jnp.tile(_rope_table(past + jnp.arange(ts)), (bs, 1))
    nch_p = sp // CMP_STRIDE
    nb_p = ((sp // SLC_BLOCK + 7) // 8) * 8
    ct_p = _cmp_to_slc_t(nch_p, nb_p)
    nch_s = past // CMP_STRIDE
    n_slc_s = (past + ts + SLC_BLOCK - 1) // SLC_BLOCK
    nb_s = ((n_slc_s + 7) // 8) * 8
    ct_s = _cmp_to_slc_t(nch_s, nb_s)
    e_s = _expand_matrix(128, pg * PAGE_SIZE)
    et_p = _expand_matrix(128, sp).T

    fm = lambda z: jnp.transpose(z, (0, 1, 3, 4, 5, 2))
    cache_cmp_t, cache_slc_t, win_t = fm(cache_cmp_kv), fm(cache_slc_kv), fm(state_win_kv)

    xp = x_prompt.reshape(mp, d)
    xs = x_sample.reshape(ms, d)
    outs = {n: [] for n in ("cmp_p", "cmp_s", "slc_p", "slc_s", "win_p", "win_s", "pool_p", "pool_s", "conv_p", "conv_s")}
    kvshape = (2, KV_HEADS, HEAD_DIM)
    tok_major = lambda z: jnp.transpose(z.reshape(bp, *kvshape, z.shape[-1]), (0, 4, 1, 2, 3))
    for l in range(depth):
        lw = _layer_weights(l, norm_mix, w_in, pool_w, pool_scale, cmp_w1, cmp_pe, cmp_w2, conv_w, conv_b, conv_ln_g,
                            conv_ln_b, w_br_pool, w_br_nsa, w_br_conv, w_out, norm_ffn, w_gate, w_up, w_down, norm_final)
        final = l == depth - 1

        (u, a, gm, kvc, ks, kw, qt, qrt, kvct, kvst, kvwt, vst, vwt, gt) = _proj_prompt(
            xp, lw["g"], lw["w_rm"], lw["w_fm"], rope_pt, bp, sp, tm)
        lohi = _cmp_lohi_dense(kvc, lw["w1p"], sp).reshape(bp, nch_p, 2048)
        kvcb, vct = _cmp_fin(lohi, jnp.zeros((bp, 1, 2048), F32), lw["bias"], lw["w2p"])
        onsa_t = _attn_prompt(qt, qrt, gt, kvcb, vct, ks, vst, kw, vwt, ct_p, et_p, bp, sp, tq, kc_sel, kc_win)
        pooled, cact = _mixpre(u, None, a, None, lw["cw"], lw["cb"], lw["lg"], lw["lb"], tm, sp // tm, 0)
        xp = _dense(xp, pooled, onsa_t, cact, gm, lw["dense"], tm, final, seq=sp)
        outs["cmp_p"].append(tok_major(kvct))
        outs["slc_p"].append(tok_major(kvst))
        outs["win_p"].append(tok_major(kvwt[:, -min(WINDOW, sp):]))
        outs["pool_p"].append(u.reshape(bp, sp, 512)[:, -(POOL_PREFIX - 1):])
        outs["conv_p"].append(a.reshape(bp, sp, 512)[:, -(CONV_WIDTH - 1):])

        u, q, qr, kvc, kvs, kvw, gn, a, gm = _proj_sample(xs, lw["g"], lw["w_s"], rope_s)
        pad8 = lambda z: jnp.concatenate([z.reshape(bs, ts, 512), jnp.zeros((bs, 16 - ts, 512), F32)], axis=1)
        lohi = _cmp_lohi_paged(cache_cmp_t, l, page_table, lw["w1p"], pg_cmp)
        lohi_new = _cmp_lohi_dense(pad8(kvc).reshape(bs * 16, 512), lw["w1p"], bs * 16)
        kvcb, _ = _cmp_fin(lohi, lohi_new.reshape(bs, 1, 2048), lw["bias"], lw["w2p"])
        q3, qr3 = q.reshape(bs, ts, 1024), qr.reshape(bs, ts, 1024)
        gn4 = gn.reshape(KV_HEADS, bs, ts, 128)
        o_part, sel = _samp_a(q3, qr3, gn4, kvcb, win_t, l, pad8(kvw), ct_s, past)
        onsa = _samp_b(cache_slc_t, l, page_table, qr3, sel, e_s, gn4, pad8(kvs), o_part, pg)
        u_state = jnp.concatenate([jnp.zeros((bs, 1, 512), F32), state_pool[l]], axis=1)
        a_state = jnp.concatenate([jnp.zeros((bs, 2, 512), F32), state_conv[l]], axis=1)
        pooled, cact = _mixpre(u.reshape(bs, ts, 512), u_state, a.reshape(bs, ts, 512), a_state,
                               lw["cw"], lw["cb"], lw["lg"], lw["lb"], ts, 1, past)
        xs = _dense(xs, pooled.reshape(ms, 512), onsa.reshape(ms, 1024), cact.reshape(ms, 512), gm, lw["dense"],
                    ms, final)
        outs["cmp_s"].append(kvc.reshape(bs, ts, *kvshape))
        outs["slc_s"].append(kvs.reshape(bs, ts, *kvshape))
        kvw_t = jnp.transpose(kvw.reshape(bs, ts, *kvshape), (0, 2, 3, 4, 1))
        win_new_t = jnp.concatenate([win_t[l], kvw_t], axis=-1)[..., -wb:]
        outs["win_s"].append(jnp.transpose(win_new_t, (0, 4, 1, 2, 3)))
        outs["pool_s"].append(jnp.concatenate([state_pool[l], u.reshape(bs, ts, 512)], axis=1)[:, -(POOL_PREFIX - 1):])
        outs["conv_s"].append(jnp.concatenate([state_conv[l], a.reshape(bs, ts, 512)], axis=1)[:, -(CONV_WIDTH - 1):])

    st = lambda n: jnp.stack(outs[n])
    return (xp.reshape(bp, sp, d), xs.reshape(bs, ts, d), st("cmp_p"), st("cmp_s"), st("slc_p"), st("slc_s"),
            st("win_p"), st("win_s"), st("pool_p"), st("pool_s"), st("conv_p"), st("conv_s"))
```

```python
import functools

import jax
import jax.numpy as jnp
from jax import lax
from jax.experimental import pallas as pl
from jax.experimental.pallas import tpu as pltpu

F32 = jnp.float32
BF16 = jnp.bfloat16

HEAD_DIM = 64
N_HEADS = 16
KV_HEADS = 4
GROUP = N_HEADS // KV_HEADS
ROPE_DIM = HEAD_DIM // 4
ROPE_HALF = ROPE_DIM // 2
ROPE_THETA = 500000.0
CMP_BLOCK = 32
CMP_STRIDE = 16
SLC_BLOCK = 64
SLC_TOP = 16
WINDOW = 512
PAGE_SIZE = 128
POOL_WINDOWS = (2, 4, 8, 16)
POOL_PREFIX = 16
CONV_WIDTH = 31
CONV_PREFIX = 32
SM_SCALE = HEAD_DIM ** -0.5
LOG2E = 1.4426950408889634
NEG = -1e30
RANK_UNROLL = 4
CHUNK_PITCH = 24
LANES = 128
GATE_COLS = 128
GATE_ROWS = 16
VMEM_LIMIT = 56 * 1024 * 1024


def _cparams(sem):
    return pltpu.CompilerParams(dimension_semantics=sem, vmem_limit_bytes=VMEM_LIMIT)


def _const_spec(shape):
    n = len(shape)
    return pl.BlockSpec(shape, lambda *a: (0,) * n, pipeline_mode=pl.Buffered(1))


def _nt(a, b):
    return lax.dot_general(a, b, (((1,), (1,)), ((), ())), preferred_element_type=F32)


def _mm(a, b):
    return jnp.dot(a, b, preferred_element_type=F32)


def _rms(x, g):
    return x * lax.rsqrt(jnp.mean(x * x, axis=-1, keepdims=True) + 1e-6) * g


def _rope128(slab, c, s1, s2):
    return slab * c + pltpu.roll(slab, LANES - ROPE_HALF, 1) * s1 + pltpu.roll(slab, ROPE_HALF, 1) * s2


def _proj_sample_kernel(x_ref, g_ref, w_ref, rope_ref, u_ref, q_ref, qr_ref, kvc_ref, kvs_ref, kvw_ref,
                        gn_ref, a_ref, gm_ref):
    x = x_ref[...]
    hb = _rms(x, g_ref[...]).astype(BF16)

    def mm(c0, n):
        return _mm(hb, w_ref[:, c0:c0 + n])

    c = rope_ref[:, 0:128]
    s1 = rope_ref[:, 128:256]
    s2 = rope_ref[:, 256:384]
    u_ref[...] = mm(0, 512)
    q = mm(512, 1024)
    q_ref[...] = (q * SM_SCALE).astype(BF16)
    for i in range(8):
        sl = slice(i * 128, (i + 1) * 128)
        qr_ref[:, sl] = (_rope128(q[:, sl], c, s1, s2) * SM_SCALE).astype(BF16)
    kv = mm(1536, 1536)
    kvc_ref[...] = kv[:, 0:512]
    for off, o_ref in ((512, kvs_ref), (1024, kvw_ref)):
        for i in range(2):
            o_ref[:, i * 128:(i + 1) * 128] = _rope128(kv[:, off + i * 128:off + (i + 1) * 128], c, s1, s2)
        o_ref[:, 256:512] = kv[:, off + 256:off + 512]
    gz = jax.nn.sigmoid(mm(3072, GATE_COLS))
    gn_ref[0] = gz
    for k in range(1, KV_HEADS):
        gn_ref[k] = pltpu.roll(gz, LANES - 3 * GROUP * k, 1)
    ci = mm(3072 + GATE_COLS, 1024)
    a_ref[...] = ci[:, :512] * jax.nn.sigmoid(ci[:, 512:])
    for i in range(3):
        gm_ref[:, i * 1024:(i + 1) * 1024] = jax.nn.sigmoid(mm(4096 + GATE_COLS + i * 1024, 1024))


def _proj_sample(x2, g, w, rope):
    m = x2.shape[0]
    outs = [((m, 512), F32), ((m, 1024), BF16), ((m, 1024), BF16), ((m, 512), F32), ((m, 512), F32),
            ((m, 512), F32), ((KV_HEADS, m, 128), F32), ((m, 512), F32), ((m, 3072), F32)]
    full = lambda shp: pl.BlockSpec(shp, lambda i: (0,) * len(shp))
    return pl.pallas_call(
        _proj_sample_kernel,
        grid=(1,),
        in_specs=[full(x2.shape), _const_spec(g.shape), _const_spec(w.shape), full(rope.shape)],
        out_specs=[full(s) for s, _ in outs],
        out_shape=[jax.ShapeDtypeStruct(s, d) for s, d in outs],
        compiler_params=_cparams(("arbitrary",)),
        name="proj_sample",
    )(x2, g, w, rope)


def _rope_t(z, cos, sin, n_heads):
    out = []
    for h in range(n_heads):
        x1 = z[h * 64:h * 64 + ROPE_HALF]
        x2 = z[h * 64 + ROPE_HALF:h * 64 + ROPE_DIM]
        out += [x1 * cos - x2 * sin, x2 * cos + x1 * sin, z[h * 64 + ROPE_DIM:(h + 1) * 64]]
    return jnp.concatenate(out, axis=0)


def _proj_prompt_kernel(x_ref, g_ref, w_ref, wt_ref, rope_ref, u_ref, a_ref, gm_ref, kvc_ref, ks_ref, kw_ref,
                        qt_ref, qrt_ref, kvct_ref, kvst_ref, kvwt_ref, vst_ref, vwt_ref, gt_ref):
    hb = _rms(x_ref[...], g_ref[...]).astype(BF16)

    def mm(c0, n):
        return _mm(hb, w_ref[:, c0:c0 + n])

    def mt(r0, n):
        return _nt(wt_ref[r0:r0 + n, :], hb)

    u_ref[...] = mm(0, 512)
    kvc_ref[...] = mm(512, 512)
    ci = mm(1024, 1024)
    a_ref[...] = ci[:, :512] * jax.nn.sigmoid(ci[:, 512:])
    for i in range(3):
        gm_ref[:, i * 1024:(i + 1) * 1024] = jax.nn.sigmoid(mm(2048 + i * 1024, 1024))
    cos = rope_ref[0:ROPE_HALF, :]
    sin = rope_ref[ROPE_HALF:ROPE_DIM, :]
    qt = mt(0, 1024) * (SM_SCALE * LOG2E)
    qt_ref[...] = qt.astype(BF16)
    qrt_ref[...] = _rope_t(qt, cos, sin, N_HEADS).astype(BF16)
    kvct_ref[...] = mt(1024, 512)
    for off, kvt_ref, vt_ref, k_ref in ((1536, kvst_ref, vst_ref, ks_ref), (2048, kvwt_ref, vwt_ref, kw_ref)):
        kt = _rope_t(mt(off, 256), cos, sin, KV_HEADS)
        vt = mt(off + 256, 256)
        kvt_ref[0:256, :] = kt
        kvt_ref[256:512, :] = vt
        vt_ref[...] = vt.astype(BF16)
        k_ref[...] = kt.T.astype(BF16)
    gt_ref[...] = jax.nn.sigmoid(mt(2560, KV_HEADS * GATE_ROWS))


def _proj_prompt(x2, g, w, wt, rope_t, bsz, seq, tm):
    m = x2.shape[0]
    nt = seq // tm
    row = lambda c: pl.BlockSpec((tm, c), lambda i: (i, 0))
    fm = lambda r: pl.BlockSpec((r, tm), lambda i: (i // nt, i % nt))
    outs = [((m, 512), F32, row(512)), ((m, 512), F32, row(512)), ((m, 3072), F32, row(3072)),
            ((m, 512), F32, row(512)), ((m, 256), BF16, row(256)), ((m, 256), BF16, row(256)),
            ((bsz * 1024, seq), BF16, fm(1024)), ((bsz * 1024, seq), BF16, fm(1024)),
            ((bsz * 512, seq), F32, fm(512)), ((bsz * 512, seq), F32, fm(512)), ((bsz * 512, seq), F32, fm(512)),
            ((bsz * 256, seq), BF16, fm(256)), ((bsz * 256, seq), BF16, fm(256)),
            ((bsz * KV_HEADS * GATE_ROWS, seq), F32, fm(KV_HEADS * GATE_ROWS))]
    return pl.pallas_call(
        _proj_prompt_kernel,
        grid=(m // tm,),
        in_specs=[row(1024), _const_spec(g.shape), _const_spec(w.shape), _const_spec(wt.shape),
                  pl.BlockSpec((ROPE_DIM, tm), lambda i: (0, i % nt))],
        out_specs=[o[2] for o in outs],
        out_shape=[jax.ShapeDtypeStruct(o[0], o[1]) for o in outs],
        compiler_params=_cparams(("parallel",)),
        name="proj_prompt",
    )(x2, g, w, wt, rope_t)


def _cmp_lohi_matmuls(load, w_ref, o_ref):
    for c in range(2):
        for pr in range(2):
            slab = c * 2 + pr
            lhs = jnp.concatenate([load(slab, s) for s in range(CMP_STRIDE)], axis=1)
            o_ref[:, slab * 512:(slab + 1) * 512] = _mm(lhs.astype(BF16), w_ref[c])


def _cmp_lohi_dense_kernel(x_ref, w_ref, o_ref, *, nch):
    _cmp_lohi_matmuls(lambda slab, s: x_ref[pl.ds(4 * s + slab, nch, stride=4 * CMP_STRIDE), :], w_ref, o_ref)


def _cmp_lohi_dense(x2, w1p, ntok):
    m = x2.shape[0]
    nch = ntok // CMP_STRIDE
    return pl.pallas_call(
        functools.partial(_cmp_lohi_dense_kernel, nch=nch),
        grid=(m // ntok,),
        in_specs=[pl.BlockSpec((ntok * 4, 128), lambda i: (i, 0)), _const_spec(w1p.shape)],
        out_specs=pl.BlockSpec((nch, 2048), lambda i: (i, 0)),
        out_shape=jax.ShapeDtypeStruct((m // CMP_STRIDE, 2048), F32),
        compiler_params=_cparams(("parallel",)),
        name="cmp_lohi",
    )(x2.reshape(m * 4, 128), w1p)


def _cmp_lohi_paged_kernel(pt_ref, *refs, n_pages):
    del pt_ref
    pages = refs[:n_pages]
    w_ref, o_ref, x_scr = refs[n_pages:]
    cpp = PAGE_SIZE // CMP_STRIDE
    for i, p in enumerate(pages):
        for c in range(2):
            xt = p[c].reshape(KV_HEADS * HEAD_DIM, PAGE_SIZE).T
            for pr in range(2):
                for n in range(cpp):
                    r0 = (i * cpp + n) * CHUNK_PITCH
                    x_scr[c * 2 + pr, r0:r0 + CMP_STRIDE, :] = xt[n * CMP_STRIDE:(n + 1) * CMP_STRIDE,
                                                                  pr * 128:(pr + 1) * 128]
    nch = n_pages * cpp
    _cmp_lohi_matmuls(lambda slab, s: x_scr[slab, pl.ds(s, nch, stride=CHUNK_PITCH), :], w_ref, o_ref)


def _cmp_lohi_paged(cache_t, layer, page_table, w1p, pg):
    b, n_pages = page_table.shape
    nch = pg * (PAGE_SIZE // CMP_STRIDE)

    def page_spec(i):
        return pl.BlockSpec((None, None, 2, KV_HEADS, HEAD_DIM, PAGE_SIZE),
                            lambda bb, s, pt: (layer, pt[bb, s * pg + i], 0, 0, 0, 0))

    return pl.pallas_call(
        functools.partial(_cmp_lohi_paged_kernel, n_pages=pg),
        grid_spec=pltpu.PrefetchScalarGridSpec(
            num_scalar_prefetch=1,
            grid=(b, n_pages // pg),
            in_specs=[page_spec(i) for i in range(pg)] + [_const_spec(w1p.shape)],
            out_specs=pl.BlockSpec((None, nch, 2048), lambda bb, s, pt: (bb, s, 0)),
            scratch_shapes=[pltpu.VMEM((4, nch * CHUNK_PITCH, 128), F32)],
        ),
        out_shape=jax.ShapeDtypeStruct((b, n_pages * (PAGE_SIZE // CMP_STRIDE), 2048), F32),
        compiler_params=_cparams(("parallel", "parallel")),
        name="cmp_lohi_paged",
    )(page_table, *([cache_t] * pg), w1p)


def _cmp_fin_kernel(lohi_ref, nxt_ref, bias_ref, w2_ref, o_ref, vt_ref, *, nch):
    row = lax.broadcasted_iota(jnp.int32, (nch, 128), 0)
    for c in range(2):
        for pr in range(2):
            base = (c * 2 + pr) * 512
            hs = []
            for kk in range(2):
                lo = lohi_ref[:, base + kk * 256:base + kk * 256 + 128]
                hi = lohi_ref[:, base + kk * 256 + 128:base + kk * 256 + 256]
                nx = nxt_ref[:, base + kk * 256 + 128:base + kk * 256 + 256]
                hin = jnp.where(row == nch - 1, nx, pltpu.roll(hi, nch - 1, 0))
                hs.append(jax.nn.gelu(lo + hin + bias_ref[c:c + 1, :]))
            hid = jnp.concatenate(hs, axis=1).astype(BF16)
            out = _mm(hid, w2_ref[c])
            o_ref[:, c * 256 + pr * 128:c * 256 + (pr + 1) * 128] = out.astype(BF16)
            if c == 1:
                vt_ref[pr * 128:(pr + 1) * 128, :] = out.T.astype(BF16)


def _cmp_fin(lohi, nxt, bias, w2p):
    b, nch, _ = lohi.shape
    return pl.pallas_call(
        functools.partial(_cmp_fin_kernel, nch=nch),
        grid=(b,),
        in_specs=[pl.BlockSpec((None, nch, 2048), lambda i: (i, 0, 0)),
                  pl.BlockSpec((None, 1, 2048), lambda i: (i, 0, 0)),
                  _const_spec(bias.shape), _const_spec(w2p.shape)],
        out_specs=[pl.BlockSpec((None, nch, 512), lambda i: (i, 0, 0)),
                   pl.BlockSpec((None, 256, nch), lambda i: (i, 0, 0))],
        out_shape=[jax.ShapeDtypeStruct((b, nch, 512), BF16), jax.ShapeDtypeStruct((b, 256, nch), BF16)],
        compiler_params=_cparams(("parallel",)),
        name="cmp_fin",
    )(lohi, nxt, bias, w2p)


def _split3(x):
    x1 = x.astype(BF16)
    r1 = x - x1.astype(F32)
    x2 = r1.astype(BF16)
    x3 = (r1 - x2.astype(F32)).astype(BF16)
    return x1, x2, x3


def _block_rank_select(imp_ref, n_blocks, valid):
    x = imp_ref[...]
    jrow = lax.broadcasted_iota(jnp.int32, x.shape, 0)
    assert x.shape[0] % RANK_UNROLL == 0

    def body(i, cnt):
        for u in range(RANK_UNROLL):
            jp = i * RANK_UNROLL + u
            r = imp_ref[pl.ds(jp, 1), :]
            cnt = cnt + jnp.where(jrow > jp, jnp.where(r >= x, 1.0, 0.0), jnp.where(r > x, 1.0, 0.0))
        return cnt

    cnt = lax.fori_loop(0, (n_blocks + RANK_UNROLL - 1) // RANK_UNROLL, body, jnp.zeros(x.shape, F32))
    return jnp.where((cnt < SLC_TOP) & valid, 1.0, 0.0)


def _update_t(m_ref, acc_ref, sl, s, v_ext):
    m_old = m_ref[:, sl]
    m_new = jnp.maximum(m_old, s.max(axis=0, keepdims=True))
    alpha = jnp.exp2(m_old - m_new)
    p = jnp.exp2(s - m_new).astype(BF16)
    acc_ref[:, sl] = acc_ref[:, sl] * alpha + _mm(v_ext, p)
    m_ref[:, sl] = m_new


def _finish_t(acc_ref, sl):
    return acc_ref[0:128, sl] * (1.0 / jnp.maximum(acc_ref[128:129, sl], 1e-30))


def _attn_prompt_kernel(qt_ref, qrt_ref, gt_ref, kc_ref, vct_ref, ks_ref, vst_ref, kw_ref, vwt_ref, ct_ref, et_ref,
                        o_ref, imp_scr, lim_scr, qa_scr, oc_scr, m_scr, acc_scr, mw_scr, accw_scr, sa_scr, sb_scr,
                        *, tq, nch, n_slc, kc_sel, kc_win):
    k = pl.program_id(1)
    t0 = pl.program_id(2) * tq
    par = k % 2
    r = GROUP * tq

    def stack(ref):
        z = jnp.zeros((HEAD_DIM, tq), BF16)
        cols = []
        for g in range(GROUP):
            qg = ref[g * HEAD_DIM:(g + 1) * HEAD_DIM, :]
            cols.append(jnp.where(par == 0, jnp.concatenate([qg, z], axis=0), jnp.concatenate([z, qg], axis=0)))
        return jnp.concatenate(cols, axis=1)

    q4 = stack(qt_ref)
    qr4 = stack(qrt_ref)
    pos = t0 + lax.broadcasted_iota(jnp.int32, (1, tq), 1)

    vis = lax.broadcasted_iota(jnp.int32, (nch, tq), 0) <= ((pos - (CMP_BLOCK - 1)) >> 4)
    kc = kc_ref[...]
    vct = vct_ref[...]
    p_sum = None
    s_c = _mm(kc, q4)
    es, invs = [], []
    for g in range(GROUP):
        sg = jnp.where(vis, s_c[:, g * tq:(g + 1) * tq], NEG)
        e = jnp.where(vis, jnp.exp2(sg - sg.max(axis=0, keepdims=True)), 0.0)
        inv = 1.0 / jnp.maximum(e.sum(axis=0, keepdims=True), 1e-30)
        p_sum = e * inv if p_sum is None else p_sum + e * inv
        es.append(e.astype(BF16))
        invs.append(inv)
    oc_scr[...] = _mm(vct, jnp.concatenate(es, axis=1)) * jnp.concatenate(invs, axis=1)

    ct = ct_ref[...]
    imp = sum(_mm(ct, piece) for piece in _split3(p_sum))
    nb = imp.shape[0]
    jb = lax.broadcasted_iota(jnp.int32, (nb, tq), 0)
    cur = pos >> 6
    forced = (jb == 0) | (jb == cur) | (jb == cur - 1)
    valid = jb <= cur
    posb = jnp.broadcast_to(pos, (nb, tq))
    imp_scr[...] = jnp.where(valid, jnp.where(forced, jnp.inf, imp), -jnp.inf)

    @pl.when(t0 + tq <= SLC_TOP * SLC_BLOCK)
    def _():
        lim_scr[...] = jnp.where(valid, posb, -1)

    @pl.when(t0 + tq > SLC_TOP * SLC_BLOCK)
    def _():
        sel = _block_rank_select(imp_scr, jnp.minimum(n_slc, (t0 + tq - 1) // SLC_BLOCK + 1), valid)
        lim_scr[...] = jnp.where(sel > 0.5, posb, -1)

    bias = jnp.where(lim_scr[...] >= 0, 0.0, NEG).astype(BF16)
    bias = jnp.concatenate([bias, jnp.zeros((128 - nb, tq), BF16)], axis=0)
    qa_scr[0:128, :] = qr4
    qa_scr[128:256, :] = jnp.concatenate([bias] * GROUP, axis=1)
    m_scr[...] = jnp.full((1, r), NEG, F32)
    acc_scr[...] = jnp.zeros(acc_scr.shape, F32)
    ones_s = jnp.ones((16, kc_sel), BF16)

    def scores(c):
        k0 = pl.multiple_of(c * kc_sel, kc_sel)
        k_aug = jnp.concatenate([ks_ref[pl.ds(k0, kc_sel), :], et_ref[pl.ds(k0, kc_sel), :]], axis=1)
        return _mm(k_aug, qa_scr[...])

    def soft_pv(s_ref, c, causal):
        k0 = pl.multiple_of(c * kc_sel, kc_sel)
        s = s_ref[...]
        if causal:
            al = (k0 + lax.broadcasted_iota(jnp.int32, (kc_sel, tq), 0)) <= pos
            s = jnp.concatenate([jnp.where(al, s[:, g * tq:(g + 1) * tq], NEG) for g in range(GROUP)], axis=1)
        _update_t(m_scr, acc_scr, slice(None), s, jnp.concatenate([vst_ref[:, pl.ds(k0, kc_sel)], ones_s], axis=0))

    c_last = t0 // kc_sel
    sa_scr[...] = scores(0)

    def pair_body(i, carry):
        sb_scr[...] = scores(2 * i + 1)
        soft_pv(sa_scr, 2 * i, False)
        sa_scr[...] = scores(2 * i + 2)
        soft_pv(sb_scr, 2 * i + 1, False)
        return carry

    lax.fori_loop(0, c_last // 2, pair_body, 0)

    @pl.when(c_last % 2 == 0)
    def _():
        soft_pv(sa_scr, c_last, True)

    @pl.when(c_last % 2 == 1)
    def _():
        sb_scr[...] = scores(c_last)
        soft_pv(sa_scr, c_last - 1, False)
        soft_pv(sb_scr, c_last, True)

    mw_scr[...] = jnp.full((1, r), NEG, F32)
    accw_scr[...] = jnp.zeros(accw_scr.shape, F32)
    ones_w = jnp.ones((16, kc_win), BF16)
    k0w = jnp.maximum(t0 - WINDOW, 0)
    for i in range((WINDOW + tq) // kc_win):
        kk = pl.multiple_of(k0w + i * kc_win, tq)
        dp = pos - (kk + lax.broadcasted_iota(jnp.int32, (kc_win, tq), 0))
        al = lax.bitcast_convert_type(dp, jnp.uint32) < WINDOW
        s = _mm(kw_ref[pl.ds(kk, kc_win), :], qa_scr[0:128, :])
        s = jnp.concatenate([jnp.where(al, s[:, g * tq:(g + 1) * tq], NEG) for g in range(GROUP)], axis=1)
        _update_t(mw_scr, accw_scr, slice(None), s,
                  jnp.concatenate([vwt_ref[:, pl.ds(kk, kc_win)], ones_w], axis=0))

    gt = gt_ref[...]
    for g in range(GROUP):
        sl = slice(g * tq, (g + 1) * tq)
        og = gt[g:g + 1] * oc_scr[:, sl] + gt[GROUP + g:GROUP + g + 1] * _finish_t(acc_scr, sl) \
            + gt[2 * GROUP + g:2 * GROUP + g + 1] * _finish_t(accw_scr, sl)
        o_ref[g * HEAD_DIM:(g + 1) * HEAD_DIM, :] = jnp.where(par == 0, og[0:HEAD_DIM], og[HEAD_DIM:]).astype(BF16)


def _attn_prompt(qt, qrt, gt, kvc, vct, ks, vst, kw, vwt, ct, et, bsz, seq, tq, kc_sel, kc_win):
    nch = kvc.shape[1]
    nt = seq // tq
    n_slc = seq // SLC_BLOCK
    nb = ct.shape[0]
    r = GROUP * tq
    qspec = pl.BlockSpec((256, tq), lambda b, k, t: (b * KV_HEADS + k, t))
    kspec = pl.BlockSpec((seq, 128), lambda b, k, t: (b, k // 2))
    vspec = pl.BlockSpec((128, seq), lambda b, k, t: (b * 2 + k // 2, 0))
    return pl.pallas_call(
        functools.partial(_attn_prompt_kernel, tq=tq, nch=nch, n_slc=n_slc, kc_sel=kc_sel, kc_win=kc_win),
        grid=(bsz, KV_HEADS, nt),
        in_specs=[qspec, qspec,
                  pl.BlockSpec((GATE_ROWS, tq), lambda b, k, t: (b * KV_HEADS + k, t)),
                  pl.BlockSpec((None, nch, 128), lambda b, k, t: (b, 0, k // 2)),
                  pl.BlockSpec((None, 128, nch), lambda b, k, t: (b, k // 2, 0)),
                  kspec, vspec, kspec, vspec, _const_spec(ct.shape), _const_spec(et.shape)],
        out_specs=pl.BlockSpec((256, tq), lambda b, k, t: (b * KV_HEADS + k, t)),
        out_shape=jax.ShapeDtypeStruct((bsz * 1024, seq), BF16),
        scratch_shapes=[pltpu.VMEM((nb, tq), F32), pltpu.VMEM((nb, tq), jnp.int32), pltpu.VMEM((256, r), BF16),
                        pltpu.VMEM((128, r), F32),
                        pltpu.VMEM((1, r), F32), pltpu.VMEM((128 + 16, r), F32),
                        pltpu.VMEM((1, r), F32), pltpu.VMEM((128 + 16, r), F32),
                        pltpu.VMEM((kc_sel, r), F32), pltpu.VMEM((kc_sel, r), F32)],
        compiler_params=_cparams(("parallel", "parallel", "arbitrary")),
        name="attn_prompt",
    )(qt, qrt, gt, kvc, vct, ks, vst, kw, vwt, ct, et)


def _pad_q(qf, par, g, lane):
    slab = qf[:, (g // 2) * 128:(g // 2 + 1) * 128]
    rolled = pltpu.roll(slab, 64, 1)
    src = slab if par == g % 2 else rolled
    return jnp.where((lane >> 6) == par, src, 0.0)


def _stack_q(q_blk, par):
    qf = q_blk.astype(F32)
    lane = lax.broadcasted_iota(jnp.int32, (qf.shape[0], 128), 1)
    return jnp.concatenate([_pad_q(qf, par, g, lane) for g in range(GROUP)], axis=0).astype(BF16)


def _softmax_parts(parts, r):
    ss = []
    for s, al, _ in parts:
        ss.append(jnp.where(al[None], s.reshape(GROUP, r, s.shape[-1]), NEG))
    m = ss[0].max(axis=-1, keepdims=True)
    for s in ss[1:]:
        m = jnp.maximum(m, s.max(axis=-1, keepdims=True))
    l = jnp.zeros_like(m)
    acc = jnp.zeros((GROUP, r, 128), F32)
    ps = []
    for (_, al, pv), s in zip(parts, ss):
        p = jnp.where(al[None], jnp.exp(s - m), 0.0)
        ps.append(p)
        l = l + p.sum(axis=-1, keepdims=True)
        acc = acc + pv(p.reshape(GROUP * r, p.shape[-1]).astype(BF16)).reshape(GROUP, r, 128)
    inv = 1.0 / jnp.maximum(l, 1e-30)
    return acc * inv, [p * inv for p in ps]


def _online_update(m_ref, l_ref, acc_ref, s, allowed, pv, r):
    n = s.shape[-1]
    s3 = jnp.where(allowed[None], s.reshape(GROUP, r, n), NEG)
    m_old = m_ref[...]
    m_new = jnp.maximum(m_old, s3.max(axis=-1, keepdims=True))
    alpha = jnp.exp(m_old - m_new)
    p = jnp.where(allowed[None], jnp.exp(s3 - m_new), 0.0)
    l_ref[...] = alpha * l_ref[...] + p.sum(axis=-1, keepdims=True)
    acc_ref[...] = alpha * acc_ref[...] + pv(p.reshape(GROUP * r, n).astype(BF16)).reshape(GROUP, r, 128)
    m_ref[...] = m_new


def _place_heads(o4, par, r):
    lane = lax.broadcasted_iota(jnp.int32, (r, 128), 1)
    outs = []
    for pr in range(2):
        lo, hi = o4[2 * pr], o4[2 * pr + 1]
        lo = lo if par == 0 else pltpu.roll(lo, 64, 1)
        hi = hi if par == 1 else pltpu.roll(hi, 64, 1)
        outs.append(jnp.where(lane < 64, lo, hi))
    return jnp.concatenate(outs, axis=1)


def _gate(gn, br, r):
    return jnp.stack([gn[:, br * GROUP + g:br * GROUP + g + 1] for g in range(GROUP)], axis=0)


def _pair_t(ref, c, pr):
    return ref[c, 2 * pr:2 * pr + 2].reshape(2 * HEAD_DIM, ref.shape[-1]).astype(BF16)


def _samp_a_kernel(q_ref, qr_ref, gn_ref, kvc_ref, win_ref, kvwn_ref, ct_ref, o_ref, sel_ref, imp_scr,
                   *, t, nch, n_slc, past, wb):
    nb = ct_ref.shape[0]
    p_sums = []
    pos_c = past + lax.broadcasted_iota(jnp.int32, (t, nch), 0)
    cend = lax.broadcasted_iota(jnp.int32, (t, nch), 1) * CMP_STRIDE + (CMP_BLOCK - 1)
    tt = lax.broadcasted_iota(jnp.int32, (t, wb), 0)
    ii = lax.broadcasted_iota(jnp.int32, (t, wb), 1)
    dp_buf = wb + tt - ii
    al_buf = (dp_buf >= 0) & (dp_buf < WINDOW) & (past - wb + ii >= 0)
    nn = kvwn_ref.shape[0]
    dp_new = lax.broadcasted_iota(jnp.int32, (t, nn), 0) - lax.broadcasted_iota(jnp.int32, (t, nn), 1)
    al_new = (dp_new >= 0) & (dp_new < WINDOW)
    for k in range(KV_HEADS):
        par, pr = k % 2, k // 2
        q4 = _stack_q(q_ref[:, k * 256:(k + 1) * 256], par)
        qr4 = _stack_q(qr_ref[:, k * 256:(k + 1) * 256], par)
        ksl = slice(pr * 128, (pr + 1) * 128)
        vsl = slice(256 + pr * 128, 256 + (pr + 1) * 128)
        kc, vc = kvc_ref[:, ksl], kvc_ref[:, vsl]
        o_cmp, (p_cmp,) = _softmax_parts([(_nt(q4, kc), cend <= pos_c, lambda p, vc=vc: _mm(p, vc))], t)
        p_sums.append(p_cmp.sum(axis=0))
        kwt, vwt = _pair_t(win_ref, 0, pr), _pair_t(win_ref, 1, pr)
        kn, vn = kvwn_ref[:, ksl].astype(BF16), kvwn_ref[:, vsl].astype(BF16)
        o_win, _ = _softmax_parts([(_mm(qr4, kwt), al_buf, lambda p, vwt=vwt: _nt(p, vwt)),
                                   (_nt(qr4, kn), al_new, lambda p, vn=vn: _mm(p, vn))], t)
        gn = gn_ref[k]
        o4 = _gate(gn, 0, t) * o_cmp + _gate(gn, 2, t) * o_win
        o_ref[:, k * 256:(k + 1) * 256] = _place_heads(o4, par, t)
    p_all = jnp.concatenate(p_sums + [jnp.zeros((128 - KV_HEADS * t, nch), F32)], axis=0)
    ct = ct_ref[...]
    imp = sum(_nt(ct, piece) for piece in _split3(p_all))
    jb = lax.broadcasted_iota(jnp.int32, (nb, 128), 0)
    col = lax.broadcasted_iota(jnp.int32, (nb, 128), 1)
    cur = (past + (col & (t - 1))) >> 6
    forced = (jb == 0) | (jb == cur) | (jb == cur - 1)
    valid = jb <= cur
    imp_scr[...] = jnp.where(valid, jnp.where(forced, jnp.inf, imp), -jnp.inf)
    sel_ref[...] = _block_rank_select(imp_scr, n_slc, valid)


def _samp_a(q3, qr3, gn4, kvc, win_t, layer, kvwn3, ct, past):
    b, t, _ = q3.shape
    nch = kvc.shape[1]
    nb = ct.shape[0]
    wb = win_t.shape[-1]
    n_slc = (past + t + SLC_BLOCK - 1) // SLC_BLOCK
    per_b = lambda r, c: pl.BlockSpec((None, r, c), lambda i: (i, 0, 0))
    return pl.pallas_call(
        functools.partial(_samp_a_kernel, t=t, nch=nch, n_slc=n_slc, past=past, wb=wb),
        grid=(b,),
        in_specs=[per_b(t, 1024), per_b(t, 1024),
                  pl.BlockSpec((KV_HEADS, None, t, 128), lambda i: (0, i, 0, 0)),
                  per_b(nch, 512),
                  pl.BlockSpec((None, None, 2, KV_HEADS, HEAD_DIM, wb), lambda i: (layer, i, 0, 0, 0, 0)),
                  per_b(kvwn3.shape[1], 512), _const_spec(ct.shape)],
        out_specs=[per_b(t, 1024), per_b(nb, 128)],
        out_shape=[jax.ShapeDtypeStruct((b, t, 1024), F32), jax.ShapeDtypeStruct((b, nb, 128), F32)],
        scratch_shapes=[pltpu.VMEM((nb, 128), F32)],
        compiler_params=_cparams(("parallel",)),
        name="samp_a",
    )(q3, qr3, gn4, kvc, win_t, kvwn3, ct)


def _samp_b_kernel(pt_ref, *refs, pg, t):
    del pt_ref
    pages = refs[:pg]
    qr_ref, sel_ref, seln_ref, e_ref, gn_ref, kvn_ref, op_ref, o_ref, m_scr, l_scr, acc_scr, s_scr = refs[pg:]
    s = pl.program_id(1)
    ns = pl.num_programs(1)
    nbc = sel_ref.shape[0]

    @pl.when(s == 0)
    def _():
        m_scr[...] = jnp.full(m_scr.shape, NEG, F32)
        l_scr[...] = jnp.zeros(l_scr.shape, F32)
        acc_scr[...] = jnp.zeros(acc_scr.shape, F32)

    selc = jnp.concatenate([sel_ref[...], jnp.zeros((128 - nbc, 128), F32)], axis=0).T
    msk_all = _mm(selc[:KV_HEADS * t].astype(BF16), e_ref[...])
    for pr in range(2):
        q_pair = jnp.concatenate([_stack_q(qr_ref[:, k * 256:(k + 1) * 256], k % 2) for k in (2 * pr, 2 * pr + 1)],
                                 axis=0)
        s_scr[pr] = _mm(q_pair, jnp.concatenate([_pair_t(p, 0, pr) for p in pages], axis=1))
    rows = GROUP * t
    for pr in range(2):
        vt = jnp.concatenate([_pair_t(p, 1, pr) for p in pages], axis=1)
        ps, alphas = [], []
        for hh in range(2):
            k = 2 * pr + hh
            al = (msk_all[k * t:(k + 1) * t] > 0.5)[None]
            s3 = jnp.where(al, s_scr[pr, hh * rows:(hh + 1) * rows, :].reshape(GROUP, t, -1), NEG)
            m_old = m_scr[k]
            m_new = jnp.maximum(m_old, s3.max(axis=-1, keepdims=True))
            alpha = jnp.exp(m_old - m_new)
            p = jnp.where(al, jnp.exp(s3 - m_new), 0.0)
            l_scr[k] = alpha * l_scr[k] + p.sum(axis=-1, keepdims=True)
            m_scr[k] = m_new
            ps.append(p.reshape(rows, -1).astype(BF16))
            alphas.append(alpha)
        pv = _nt(jnp.concatenate(ps, axis=0), vt)
        for hh in range(2):
            k = 2 * pr + hh
            acc_scr[k] = alphas[hh] * acc_scr[k] + pv[hh * rows:(hh + 1) * rows].reshape(GROUP, t, 128)

    @pl.when(s == ns - 1)
    def _():
        nn = kvn_ref.shape[0]
        dp = lax.broadcasted_iota(jnp.int32, (t, nn), 0) - lax.broadcasted_iota(jnp.int32, (t, nn), 1)
        seln = jnp.concatenate([seln_ref[...], jnp.zeros((128 - 8, 128), F32)], axis=0).T
        for k in range(KV_HEADS):
            par, pr = k % 2, k // 2
            qr4 = _stack_q(qr_ref[:, k * 256:(k + 1) * 256], par)
            kk = kvn_ref[:, pr * 128:(pr + 1) * 128].astype(BF16)
            vv = kvn_ref[:, 256 + pr * 128:256 + (pr + 1) * 128].astype(BF16)
            al = (dp >= 0) & (seln[k * t:(k + 1) * t, 0:1] > 0.5)
            _online_update(m_scr.at[k], l_scr.at[k], acc_scr.at[k], _nt(qr4, kk), al, lambda p, vv=vv: _mm(p, vv), t)
            o_sel = acc_scr[k] * (1.0 / jnp.maximum(l_scr[k], 1e-30))
            o4 = _gate(gn_ref[k], 1, t) * o_sel
            o_ref[:, k * 256:(k + 1) * 256] = (op_ref[:, k * 256:(k + 1) * 256] + _place_heads(o4, par, t)).astype(BF16)


def _samp_b(cache_t, layer, page_table, qr3, sel, e, gn4, kvn3, o_part, pg):
    b, n_pages = page_table.shape
    t = qr3.shape[1]
    ns = n_pages // pg
    nbc = pg * (PAGE_SIZE // SLC_BLOCK)
    n_past_blk = n_pages * (PAGE_SIZE // SLC_BLOCK)

    def page_spec(i):
        return pl.BlockSpec((None, None, 2, KV_HEADS, HEAD_DIM, PAGE_SIZE),
                            lambda bb, s, pt: (layer, pt[bb, s * pg + i], 0, 0, 0, 0))

    per_b = lambda r, c: pl.BlockSpec((None, r, c), lambda bb, s, pt: (bb, 0, 0))
    return pl.pallas_call(
        functools.partial(_samp_b_kernel, pg=pg, t=t),
        grid_spec=pltpu.PrefetchScalarGridSpec(
            num_scalar_prefetch=1,
            grid=(b, ns),
            in_specs=[page_spec(i) for i in range(pg)] + [
                per_b(t, 1024),
                pl.BlockSpec((None, nbc, 128), lambda bb, s, pt: (bb, s, 0)),
                pl.BlockSpec((None, 8, 128), lambda bb, s, pt: (bb, n_past_blk // 8, 0)),
                _const_spec(e.shape),
                pl.BlockSpec((KV_HEADS, None, t, 128), lambda bb, s, pt: (0, bb, 0, 0)),
                per_b(kvn3.shape[1], 512), per_b(t, 1024)],
            out_specs=per_b(t, 1024),
            scratch_shapes=[pltpu.VMEM((KV_HEADS, GROUP, t, 1), F32), pltpu.VMEM((KV_HEADS, GROUP, t, 1), F32),
                            pltpu.VMEM((KV_HEADS, GROUP, t, 128), F32),
                            pltpu.VMEM((2, 2 * GROUP * t, pg * PAGE_SIZE), F32)],
        ),
        out_shape=jax.ShapeDtypeStruct((b, t, 1024), BF16),
        compiler_params=_cparams(("parallel", "arbitrary")),
        name="samp_b",
    )(page_table, *([cache_t] * pg), qr3, sel, sel, e, gn4, kvn3, o_part)


def _mixpre_kernel(u_ref, up_ref, a_ref, ap_ref, cw_ref, cb_ref, lg_ref, lb_ref, pool_ref, cact_ref,
                   u_scr, a_scr, *, tm, tiles_per_seq, pos_base, has_state):
    ti = pl.program_id(0) % tiles_per_seq
    pos0 = pos_base + ti * tm
    keep = 1.0 if has_state else jnp.where(ti == 0, 0.0, 1.0)
    u = u_ref[...]
    u_scr[0:POOL_PREFIX, :] = up_ref[...] * keep
    u_scr[POOL_PREFIX:, :] = u
    a_scr[0:CONV_PREFIX, :] = ap_ref[...] * keep
    a_scr[CONV_PREFIX:, :] = a_ref[...]
    pos = pos0 + lax.broadcasted_iota(jnp.int32, (tm, 128), 0)
    for gi, w in enumerate(POOL_WINDOWS):
        sl = slice(gi * 128, (gi + 1) * 128)
        acc = u[:, sl]
        for d in range(1, w):
            acc = acc + u_scr[pl.ds(POOL_PREFIX - d, tm), sl]
        cnt = jnp.minimum(pos + 1, w).astype(F32)
        pool_ref[:, sl] = (acc / cnt - u[:, sl]).astype(BF16)
    y = jnp.zeros((tm, 512), F32) + cb_ref[...]
    for j in range(CONV_WIDTH):
        y = y + a_scr[pl.ds(CONV_PREFIX - (CONV_WIDTH - 1) + j, tm), :] * cw_ref[j:j + 1, :]
    mu = jnp.mean(y, axis=-1, keepdims=True)
    yc = y - mu
    var = jnp.mean(yc * yc, axis=-1, keepdims=True)
    yn = yc * lax.rsqrt(var + 1e-5) * lg_ref[...] + lb_ref[...]
    cact_ref[...] = (yn * jax.nn.sigmoid(yn)).astype(BF16)


def _mixpre(u, u_state, a, a_state, cw, cb, lg, lb, tm, tiles_per_seq, pos_base):
    has_state = u_state is not None
    if has_state:
        up, ap = u_state, a_state
        n_tiles = u.shape[0]
        up_spec = pl.BlockSpec((None, POOL_PREFIX, 512), lambda i: (i, 0, 0))
        ap_spec = pl.BlockSpec((None, CONV_PREFIX, 512), lambda i: (i, 0, 0))
        row = pl.BlockSpec((None, tm, 512), lambda i: (i, 0, 0))
        out_shape = (n_tiles, tm, 512)
    else:
        up, ap = u, a
        n_tiles = u.shape[0] // tm
        up_spec = pl.BlockSpec((POOL_PREFIX, 512), lambda i: (jnp.maximum(i * (tm // POOL_PREFIX) - 1, 0), 0))
        ap_spec = pl.BlockSpec((CONV_PREFIX, 512), lambda i: (jnp.maximum(i * (tm // CONV_PREFIX) - 1, 0), 0))
        row = pl.BlockSpec((tm, 512), lambda i: (i, 0))
        out_shape = (u.shape[0], 512)
    return pl.pallas_call(
        functools.partial(_mixpre_kernel, tm=tm, tiles_per_seq=tiles_per_seq, pos_base=pos_base, has_state=has_state),
        grid=(n_tiles,),
        in_specs=[row, up_spec, row, ap_spec, _const_spec(cw.shape), _const_spec(cb.shape),
                  _const_spec(lg.shape), _const_spec(lb.shape)],
        out_specs=[row, row],
        out_shape=[jax.ShapeDtypeStruct(out_shape, BF16), jax.ShapeDtypeStruct(out_shape, BF16)],
        scratch_shapes=[pltpu.VMEM((tm + POOL_PREFIX, 512), F32), pltpu.VMEM((tm + CONV_PREFIX, 512), F32)],
        compiler_params=_cparams(("parallel",)),
        name="mixpre",
    )(u, up, a, ap, cw, cb, lg, lb)


def _dense_kernel(x_ref, pool_ref, onsa_ref, cact_ref, gm_ref, pw_ref, ps_ref, wbp_ref, wbn_ref, wbc_ref, wo_ref,
                  nf_ref, wg_ref, wu_ref, wd_ref, nfin_ref, o_ref, *, final, ff_chunk, onsa_t):
    op = jnp.concatenate([_mm(pool_ref[:, g * 128:(g + 1) * 128], pw_ref[g]) for g in range(4)], axis=1)
    op = (op * ps_ref[...]).astype(BF16)
    onsa = onsa_ref[...].astype(F32).T.astype(BF16) if onsa_t else onsa_ref[...]
    m = gm_ref[:, 0:1024] * _mm(op, wbp_ref[...])
    m = m + gm_ref[:, 1024:2048] * _mm(onsa, wbn_ref[...])
    m = m + gm_ref[:, 2048:3072] * _mm(cact_ref[...], wbc_ref[...])
    x1 = x_ref[...] + _mm(m.astype(BF16), wo_ref[...])
    hb = _rms(x1, nf_ref[...]).astype(BF16)
    dff = wg_ref.shape[1]
    acc = x1
    for c0 in range(0, dff, ff_chunk):
        gt = _mm(hb, wg_ref[:, c0:c0 + ff_chunk])
        up = _mm(hb, wu_ref[:, c0:c0 + ff_chunk])
        acc = acc + _mm((gt * jax.nn.sigmoid(gt) * up).astype(BF16), wd_ref[c0:c0 + ff_chunk, :])
    o_ref[...] = _rms(acc, nfin_ref[...]) if final else acc


def _dense(x2, pooled, onsa, cact, gm, weights, tm, final, seq=None):
    m = x2.shape[0]
    row = lambda c: pl.BlockSpec((tm, c), lambda i: (i, 0))
    if seq is None:
        onsa_spec = row(1024)
    else:
        nt = seq // tm
        onsa_spec = pl.BlockSpec((1024, tm), lambda i: (i // nt, i % nt))
    dff = weights[8].shape[1]
    ff_chunk = dff // 2 if (dff // 2) % 128 == 0 else dff
    return pl.pallas_call(
        functools.partial(_dense_kernel, final=final, ff_chunk=ff_chunk, onsa_t=seq is not None),
        grid=(m // tm,),
        in_specs=[row(1024), row(512), onsa_spec, row(512), row(3072)] + [_const_spec(w.shape) for w in weights],
        out_specs=row(1024),
        out_shape=jax.ShapeDtypeStruct((m, 1024), F32),
        compiler_params=_cparams(("parallel",)),
        name="dense",
    )(x2, pooled, onsa, cact, gm, *weights)


def _rope_angles(pos):
    inv = ROPE_THETA ** (-jnp.arange(ROPE_HALF, dtype=F32) * (2.0 / ROPE_DIM))
    return pos.astype(F32)[:, None] * inv[None, :]


def _rope_table(pos):
    ang = _rope_angles(pos)
    cos, sin = jnp.cos(ang), jnp.sin(ang)
    t = pos.shape[0]
    zeros = jnp.zeros((t, HEAD_DIM - ROPE_DIM), F32)
    zh = jnp.zeros((t, ROPE_HALF), F32)
    c = jnp.concatenate([cos, cos, zeros + 1.0], axis=1)
    s1 = jnp.concatenate([-sin, zh, zeros], axis=1)
    s2 = jnp.concatenate([zh, sin, zeros], axis=1)
    return jnp.concatenate([c, c, s1, s1, s2, s2], axis=1)


def _rope_table_t(pos):
    ang = _rope_angles(pos)
    return jnp.concatenate([jnp.cos(ang).T, jnp.sin(ang).T], axis=0)


def _cmp_to_slc_t(n_cmp_rows, nb):
    i0 = jnp.arange(n_cmp_rows)[None, :] * CMP_STRIDE
    j0 = jnp.arange(nb)[:, None] * SLC_BLOCK
    ov = jnp.clip(jnp.minimum(i0 + CMP_BLOCK, j0 + SLC_BLOCK) - jnp.maximum(i0, j0), 0, None)
    return (ov.astype(F32) / CMP_BLOCK).astype(BF16)


def _expand_matrix(n_rows, n_keys):
    return (jnp.arange(n_rows)[:, None] == jnp.arange(n_keys)[None, :] // SLC_BLOCK).astype(BF16)


def _layer_weights(l, norm_mix, w_in, pool_w, pool_scale, cmp_w1, cmp_pe, cmp_w2, conv_w, conv_b, conv_ln_g,
                   conv_ln_b, w_br_pool, w_br_nsa, w_br_conv, w_out, norm_ffn, w_gate, w_up, w_down, norm_final):
    w = w_in[l]
    s_q, s_kv, s_gate = 512, 512 + 1024, 512 + 1024 + 1536
    n_g = 3 * N_HEADS
    order = [(k, br, g) for k in range(KV_HEADS) for br in range(3) for g in range(GROUP)]
    idx = jnp.array([s_gate + (k * GROUP + g) * 3 + br for k, br, g in order])
    w_s = jnp.concatenate([w[:, :s_gate], w[:, idx], jnp.zeros((w.shape[0], GATE_COLS - n_g), w.dtype),
                           w[:, s_gate + n_g:]], axis=1).astype(BF16)
    w_rm = jnp.concatenate([w[:, :s_q], w[:, s_kv:s_kv + 512], w[:, s_gate + n_g:]], axis=1).astype(BF16)
    gate_rows = jnp.zeros((KV_HEADS * GATE_ROWS, w.shape[0]), w.dtype)
    gate_rows = gate_rows.at[jnp.array([k * GATE_ROWS + br * GROUP + g for k, br, g in order])].set(w[:, idx].T)
    w_fm = jnp.concatenate([w[:, s_q:s_gate].T, gate_rows], axis=0).astype(BF16)
    w1 = cmp_w1[l]
    eye2 = jnp.eye(2, dtype=F32)
    w1h = w1.reshape(2, 2, CMP_STRIDE, HEAD_DIM, 128)
    w1p = jnp.einsum('chsde,kq->cskdqhe', w1h, eye2).reshape(2, CMP_STRIDE * 128, 512).astype(BF16)
    w2p = jnp.einsum('ced,kq->ckeqd', cmp_w2[l], eye2).reshape(2, 256, 128).astype(BF16)
    bias = jnp.einsum('cpd,cpde->ce', cmp_pe[l], w1, precision=lax.Precision.HIGHEST)
    cw = jnp.concatenate([conv_w[l], jnp.zeros((1, conv_w.shape[2]), F32)], axis=0)
    dense_w = (pool_w[l].astype(BF16), pool_scale[l][None], w_br_pool[l].astype(BF16), w_br_nsa[l].astype(BF16),
               w_br_conv[l].astype(BF16), w_out[l].astype(BF16), norm_ffn[l][None], w_gate[l].astype(BF16),
               w_up[l].astype(BF16), w_down[l].astype(BF16), norm_final[None])
    return dict(g=norm_mix[l][None], w_s=w_s, w_rm=w_rm, w_fm=w_fm, w1p=w1p, w2p=w2p, bias=bias, cw=cw,
                cb=conv_b[l][None], lg=conv_ln_g[l][None], lb=conv_ln_b[l][None], dense=dense_w)


def kernel(x_prompt, x_sample, cache_cmp_kv, cache_slc_kv, state_win_kv, state_pool, state_conv, page_table,
           norm_mix, w_in, pool_w, pool_scale, cmp_w1, cmp_pe, cmp_w2, conv_w, conv_b, conv_ln_g, conv_ln_b,
           w_br_pool, w_br_nsa, w_br_conv, w_out, norm_ffn, w_gate, w_up, w_down, norm_final):
    bp, sp, d = x_prompt.shape
    bs, ts, _ = x_sample.shape
    depth = w_in.shape[0]
    n_pages = page_table.shape[1]
    past = n_pages * PAGE_SIZE
    wb = state_win_kv.shape[2]
    assert d == 1024 and sp % 512 == 0 and sp >= WINDOW + 256 and ts == 8 and n_pages % 4 == 0

    tm = 256
    tq = 256
    kc_sel = 512
    kc_win = WINDOW + tq
    pg = 16 if n_pages % 16 == 0 else 4
    pg_cmp = 32 if n_pages % 32 == 0 else pg
    mp, ms = bp * sp, bs * ts

    rope_pt = _rope_table_t(jnp.arange(sp))
    rope_s = jnp.tile(_rope_table(past + jnp.arange(ts)), (bs, 1))
    nch_p = sp // CMP_STRIDE
    nb_p = ((sp // SLC_BLOCK + 7) // 8) * 8
    ct_p = _cmp_to_slc_t(nch_p, nb_p)
    nch_s = past // CMP_STRIDE
    n_slc_s = (past + ts + SLC_BLOCK - 1) // SLC_BLOCK
    nb_s = ((n_slc_s + 7) // 8) * 8
    ct_s = _cmp_to_slc_t(nch_s, nb_s)
    e_s = _expand_matrix(128, pg * PAGE_SIZE)
    et_p = _expand_matrix(128, sp).T

    fm = lambda z: jnp.transpose(z, (0, 1, 3, 4, 5, 2))
    cache_cmp_t, cache_slc_t, win_t = fm(cache_cmp_kv), fm(cache_slc_kv), fm(state_win_kv)

    xp = x_prompt.reshape(mp, d)
    xs = x_sample.reshape(ms, d)
    outs = {n: [] for n in ("cmp_p", "cmp_s", "slc_p", "slc_s", "win_p", "win_s", "pool_p", "pool_s", "conv_p", "conv_s")}
    kvshape = (2, KV_HEADS, HEAD_DIM)
    tok_major = lambda z: jnp.transpose(z.reshape(bp, *kvshape, z.shape[-1]), (0, 4, 1, 2, 3))
    for l in range(depth):
        lw = _layer_weights(l, norm_mix, w_in, pool_w, pool_scale, cmp_w1, cmp_pe, cmp_w2, conv_w, conv_b, conv_ln_g,
                            conv_ln_b, w_br_pool, w_br_nsa, w_br_conv, w_out, norm_ffn, w_gate, w_up, w_down, norm_final)
        final = l == depth - 1

        (u, a, gm, kvc, ks, kw, qt, qrt, kvct, kvst, kvwt, vst, vwt, gt) = _proj_prompt(
            xp, lw["g"], lw["w_rm"], lw["w_fm"], rope_pt, bp, sp, tm)
        lohi = _cmp_lohi_dense(kvc, lw["w1p"], sp).reshape(bp, nch_p, 2048)
        kvcb, vct = _cmp_fin(lohi, jnp.zeros((bp, 1, 2048), F32), lw["bias"], lw["w2p"])
        onsa_t = _attn_prompt(qt, qrt, gt, kvcb, vct, ks, vst, kw, vwt, ct_p, et_p, bp, sp, tq, kc_sel, kc_win)
        pooled, cact = _mixpre(u, None, a, None, lw["cw"], lw["cb"], lw["lg"], lw["lb"], tm, sp // tm, 0)
        xp = _dense(xp, pooled, onsa_t, cact, gm, lw["dense"], tm, final, seq=sp)
        outs["cmp_p"].append(tok_major(kvct))
        outs["slc_p"].append(tok_major(kvst))
        outs["win_p"].append(tok_major(kvwt[:, -min(WINDOW, sp):]))
        outs["pool_p"].append(u.reshape(bp, sp, 512)[:, -(POOL_PREFIX - 1):])
        outs["conv_p"].append(a.reshape(bp, sp, 512)[:, -(CONV_WIDTH - 1):])

        u, q, qr, kvc, kvs, kvw, gn, a, gm = _proj_sample(xs, lw["g"], lw["w_s"], rope_s)
        pad8 = lambda z: jnp.concatenate([z.reshape(bs, ts, 512), jnp.zeros((bs, 16 - ts, 512), F32)], axis=1)
        lohi = _cmp_lohi_paged(cache_cmp_t, l, page_table, lw["w1p"], pg_cmp)
        lohi_new = _cmp_lohi_dense(pad8(kvc).reshape(bs * 16, 512), lw["w1p"], bs * 16)
        kvcb, _ = _cmp_fin(lohi, lohi_new.reshape(bs, 1, 2048), lw["bias"], lw["w2p"])
        q3, qr3 = q.reshape(bs, ts, 1024), qr.reshape(bs, ts, 1024)
        gn4 = gn.reshape(KV_HEADS, bs, ts, 128)
        o_part, sel = _samp_a(q3, qr3, gn4, kvcb, win_t, l, pad8(kvw), ct_s, past)
        onsa = _samp_b(cache_slc_t, l, page_table, qr3, sel, e_s, gn4, pad8(kvs), o_part, pg)
        u_state = jnp.concatenate([jnp.zeros((bs, 1, 512), F32), state_pool[l]], axis=1)
        a_state = jnp.concatenate([jnp.zeros((bs, 2, 512), F32), state_conv[l]], axis=1)
        pooled, cact = _mixpre(u.reshape(bs, ts, 512), u_state, a.reshape(bs, ts, 512), a_state,
                               lw["cw"], lw["cb"], lw["lg"], lw["lb"], ts, 1, past)
        xs = _dense(xs, pooled.reshape(ms, 512), onsa.reshape(ms, 1024), cact.reshape(ms, 512), gm, lw["dense"],
                    ms, final)
        outs["cmp_s"].append(kvc.reshape(bs, ts, *kvshape))
        outs["slc_s"].append(kvs.reshape(bs, ts, *kvshape))
        kvw_t = jnp.transpose(kvw.reshape(bs, ts, *kvshape), (0, 2, 3, 4, 1))
        win_new_t = jnp.concatenate([win_t[l], kvw_t], axis=-1)[..., -wb:]
        outs["win_s"].append(jnp.transpose(win_new_t, (0, 4, 1, 2, 3)))
        outs["pool_s"].append(jnp.concatenate([state_pool[l], u.reshape(bs, ts, 512)], axis=1)[:, -(POOL_PREFIX - 1):])
        outs["conv_s"].append(jnp.concatenate([state_conv[l], a.reshape(bs, ts, 512)], axis=1)[:, -(CONV_WIDTH - 1):])

    st = lambda n: jnp.stack(outs[n])
    return (xp.reshape(bp, sp, d), xs.reshape(bs, ts, d), st("cmp_p"), st("cmp_s"), st("slc_p"), st("slc_s"),
            st("win_p"), st("win_s"), st("pool_p"), st("pool_s"), st("conv_p"), st("conv_s"))
```

```python
import functools

import jax
import jax.numpy as jnp
from jax import lax
from jax.experimental import pallas as pl
from jax.experimental.pallas import tpu as pltpu

F32 = jnp.float32
BF16 = jnp.bfloat16

HEAD_DIM = 64
N_HEADS = 16
KV_HEADS = 4
GROUP = N_HEADS // KV_HEADS
ROPE_DIM = HEAD_DIM // 4
ROPE_HALF = ROPE_DIM // 2
ROPE_THETA = 500000.0
CMP_BLOCK = 32
CMP_STRIDE = 16
SLC_BLOCK = 64
SLC_TOP = 16
WINDOW = 512
PAGE_SIZE = 128
POOL_WINDOWS = (2, 4, 8, 16)
POOL_PREFIX = 16
CONV_WIDTH = 31
CONV_PREFIX = 32
SM_SCALE = HEAD_DIM ** -0.5
LOG2E = 1.4426950408889634
NEG = -1e30
RANK_UNROLL = 4
CHUNK_PITCH = 24
LANES = 128
GATE_COLS = 128
GATE_ROWS = 16
VMEM_LIMIT = 56 * 1024 * 1024


def _cparams(sem):
    return pltpu.CompilerParams(dimension_semantics=sem, vmem_limit_bytes=VMEM_LIMIT)


def _const_spec(shape):
    n = len(shape)
    return pl.BlockSpec(shape, lambda *a: (0,) * n, pipeline_mode=pl.Buffered(1))


def _nt(a, b):
    return lax.dot_general(a, b, (((1,), (1,)), ((), ())), preferred_element_type=F32)


def _mm(a, b):
    return jnp.dot(a, b, preferred_element_type=F32)


def _rms(x, g):
    return x * lax.rsqrt(jnp.mean(x * x, axis=-1, keepdims=True) + 1e-6) * g


def _rope128(slab, c, s1, s2):
    return slab * c + pltpu.roll(slab, LANES - ROPE_HALF, 1) * s1 + pltpu.roll(slab, ROPE_HALF, 1) * s2


def _proj_sample_kernel(x_ref, g_ref, w_ref, rope_ref, u_ref, q_ref, qr_ref, kvc_ref, kvs_ref, kvw_ref,
                        gn_ref, a_ref, gm_ref):
    x = x_ref[...]
    hb = _rms(x, g_ref[...]).astype(BF16)

    def mm(c0, n):
        return _mm(hb, w_ref[:, c0:c0 + n])

    c = rope_ref[:, 0:128]
    s1 = rope_ref[:, 128:256]
    s2 = rope_ref[:, 256:384]
    u_ref[...] = mm(0, 512)
    q = mm(512, 1024)
    q_ref[...] = (q * SM_SCALE).astype(BF16)
    for i in range(8):
        sl = slice(i * 128, (i + 1) * 128)
        qr_ref[:, sl] = (_rope128(q[:, sl], c, s1, s2) * SM_SCALE).astype(BF16)
    kv = mm(1536, 1536)
    kvc_ref[...] = kv[:, 0:512]
    for off, o_ref in ((512, kvs_ref), (1024, kvw_ref)):
        for i in range(2):
            o_ref[:, i * 128:(i + 1) * 128] = _rope128(kv[:, off + i * 128:off + (i + 1) * 128], c, s1, s2)
        o_ref[:, 256:512] = kv[:, off + 256:off + 512]
    gz = jax.nn.sigmoid(mm(3072, GATE_COLS))
    gn_ref[0] = gz
    for k in range(1, KV_HEADS):
        gn_ref[k] = pltpu.roll(gz, LANES - 3 * GROUP * k, 1)
    ci = mm(3072 + GATE_COLS, 1024)
    a_ref[...] = ci[:, :512] * jax.nn.sigmoid(ci[:, 512:])
    for i in range(3):
        gm_ref[:, i * 1024:(i + 1) * 1024] = jax.nn.sigmoid(mm(4096 + GATE_COLS + i * 1024, 1024))


def _proj_sample(x2, g, w, rope):
    m = x2.shape[0]
    outs = [((m, 512), F32), ((m, 1024), BF16), ((m, 1024), BF16), ((m, 512), F32), ((m, 512), F32),
            ((m, 512), F32), ((KV_HEADS, m, 128), F32), ((m, 512), F32), ((m, 3072), F32)]
    full = lambda shp: pl.BlockSpec(shp, lambda i: (0,) * len(shp))
    return pl.pallas_call(
        _proj_sample_kernel,
        grid=(1,),
        in_specs=[full(x2.shape), _const_spec(g.shape), _const_spec(w.shape), full(rope.shape)],
        out_specs=[full(s) for s, _ in outs],
        out_shape=[jax.ShapeDtypeStruct(s, d) for s, d in outs],
        compiler_params=_cparams(("arbitrary",)),
        name="proj_sample",
    )(x2, g, w, rope)


def _rope_t(z, cos, sin, n_heads):
    out = []
    for h in range(n_heads):
        x1 = z[h * 64:h * 64 + ROPE_HALF]
        x2 = z[h * 64 + ROPE_HALF:h * 64 + ROPE_DIM]
        out += [x1 * cos - x2 * sin, x2 * cos + x1 * sin, z[h * 64 + ROPE_DIM:(h + 1) * 64]]
    return jnp.concatenate(out, axis=0)


def _proj_prompt_kernel(x_ref, g_ref, w_ref, wt_ref, rope_ref, u_ref, a_ref, gm_ref, kvc_ref, ks_ref, kw_ref,
                        qt_ref, qrt_ref, kvct_ref, kvst_ref, kvwt_ref, vst_ref, vwt_ref, gt_ref):
    hb = _rms(x_ref[...], g_ref[...]).astype(BF16)

    def mm(c0, n):
        return _mm(hb, w_ref[:, c0:c0 + n])

    def mt(r0, n):
        return _nt(wt_ref[r0:r0 + n, :], hb)

    u_ref[...] = mm(0, 512)
    kvc_ref[...] = mm(512, 512)
    ci = mm(1024, 1024)
    a_ref[...] = ci[:, :512] * jax.nn.sigmoid(ci[:, 512:])
    for i in range(3):
        gm_ref[:, i * 1024:(i + 1) * 1024] = jax.nn.sigmoid(mm(2048 + i * 1024, 1024))
    cos = rope_ref[0:ROPE_HALF, :]
    sin = rope_ref[ROPE_HALF:ROPE_DIM, :]
    qt = mt(0, 1024) * (SM_SCALE * LOG2E)
    qt_ref[...] = qt.astype(BF16)
    qrt_ref[...] = _rope_t(qt, cos, sin, N_HEADS).astype(BF16)
    kvct_ref[...] = mt(1024, 512)
    for off, kvt_ref, vt_ref, k_ref in ((1536, kvst_ref, vst_ref, ks_ref), (2048, kvwt_ref, vwt_ref, kw_ref)):
        kt = _rope_t(mt(off, 256), cos, sin, KV_HEADS)
        vt = mt(off + 256, 256)
        kvt_ref[0:256, :] = kt
        kvt_ref[256:512, :] = vt
        vt_ref[...] = vt.astype(BF16)
        k_ref[...] = kt.T.astype(BF16)
    gt_ref[...] = jax.nn.sigmoid(mt(2560, KV_HEADS * GATE_ROWS))


def _proj_prompt(x2, g, w, wt, rope_t, bsz, seq, tm):
    m = x2.shape[0]
    nt = seq // tm
    row = lambda c: pl.BlockSpec((tm, c), lambda i: (i, 0))
    fm = lambda r: pl.BlockSpec((r, tm), lambda i: (i // nt, i % nt))
    outs = [((m, 512), F32, row(512)), ((m, 512), F32, row(512)), ((m, 3072), F32, row(3072)),
            ((m, 512), F32, row(512)), ((m, 256), BF16, row(256)), ((m, 256), BF16, row(256)),
            ((bsz * 1024, seq), BF16, fm(1024)), ((bsz * 1024, seq), BF16, fm(1024)),
            ((bsz * 512, seq), F32, fm(512)), ((bsz * 512, seq), F32, fm(512)), ((bsz * 512, seq), F32, fm(512)),
            ((bsz * 256, seq), BF16, fm(256)), ((bsz * 256, seq), BF16, fm(256)),
            ((bsz * KV_HEADS * GATE_ROWS, seq), F32, fm(KV_HEADS * GATE_ROWS))]
    return pl.pallas_call(
        _proj_prompt_kernel,
        grid=(m // tm,),
        in_specs=[row(1024), _const_spec(g.shape), _const_spec(w.shape), _const_spec(wt.shape),
                  pl.BlockSpec((ROPE_DIM, tm), lambda i: (0, i % nt))],
        out_specs=[o[2] for o in outs],
        out_shape=[jax.ShapeDtypeStruct(o[0], o[1]) for o in outs],
        compiler_params=_cparams(("parallel",)),
        name="proj_prompt",
    )(x2, g, w, wt, rope_t)


def _cmp_lohi_matmuls(load, w_ref, o_ref):
    for c in range(2):
        for pr in range(2):
            slab = c * 2 + pr
            lhs = jnp.concatenate([load(slab, s) for s in range(CMP_STRIDE)], axis=1)
            o_ref[:, slab * 512:(slab + 1) * 512] = _mm(lhs.astype(BF16), w_ref[c])


def _cmp_lohi_dense_kernel(x_ref, w_ref, o_ref, *, nch):
    _cmp_lohi_matmuls(lambda slab, s: x_ref[pl.ds(4 * s + slab, nch, stride=4 * CMP_STRIDE), :], w_ref, o_ref)


def _cmp_lohi_dense(x2, w1p, ntok):
    m = x2.shape[0]
    nch = ntok // CMP_STRIDE
    return pl.pallas_call(
        functools.partial(_cmp_lohi_dense_kernel, nch=nch),
        grid=(m // ntok,),
        in_specs=[pl.BlockSpec((ntok * 4, 128), lambda i: (i, 0)), _const_spec(w1p.shape)],
        out_specs=pl.BlockSpec((nch, 2048), lambda i: (i, 0)),
        out_shape=jax.ShapeDtypeStruct((m // CMP_STRIDE, 2048), F32),
        compiler_params=_cparams(("parallel",)),
        name="cmp_lohi",
    )(x2.reshape(m * 4, 128), w1p)


def _cmp_lohi_paged_kernel(pt_ref, *refs, n_pages):
    del pt_ref
    pages = refs[:n_pages]
    w_ref, o_ref, x_scr = refs[n_pages:]
    cpp = PAGE_SIZE // CMP_STRIDE
    for i, p in enumerate(pages):
        for c in range(2):
            xt = p[c].reshape(KV_HEADS * HEAD_DIM, PAGE_SIZE).T
            for pr in range(2):
                for n in range(cpp):
                    r0 = (i * cpp + n) * CHUNK_PITCH
                    x_scr[c * 2 + pr, r0:r0 + CMP_STRIDE, :] = xt[n * CMP_STRIDE:(n + 1) * CMP_STRIDE,
                                                                  pr * 128:(pr + 1) * 128]
    nch = n_pages * cpp
    _cmp_lohi_matmuls(lambda slab, s: x_scr[slab, pl.ds(s, nch, stride=CHUNK_PITCH), :], w_ref, o_ref)


def _cmp_lohi_paged(cache_t, layer, page_table, w1p, pg):
    b, n_pages = page_table.shape
    nch = pg * (PAGE_SIZE // CMP_STRIDE)

    def page_spec(i):
        return pl.BlockSpec((None, None, 2, KV_HEADS, HEAD_DIM, PAGE_SIZE),
                            lambda bb, s, pt: (layer, pt[bb, s * pg + i], 0, 0, 0, 0))

    return pl.pallas_call(
        functools.partial(_cmp_lohi_paged_kernel, n_pages=pg),
        grid_spec=pltpu.PrefetchScalarGridSpec(
            num_scalar_prefetch=1,
            grid=(b, n_pages // pg),
            in_specs=[page_spec(i) for i in range(pg)] + [_const_spec(w1p.shape)],
            out_specs=pl.BlockSpec((None, nch, 2048), lambda bb, s, pt: (bb, s, 0)),
            scratch_shapes=[pltpu.VMEM((4, nch * CHUNK_PITCH, 128), F32)],
        ),
        out_shape=jax.ShapeDtypeStruct((b, n_pages * (PAGE_SIZE // CMP_STRIDE), 2048), F32),
        compiler_params=_cparams(("parallel", "parallel")),
        name="cmp_lohi_paged",
    )(page_table, *([cache_t] * pg), w1p)


def _cmp_fin_kernel(lohi_ref, nxt_ref, bias_ref, w2_ref, o_ref, vt_ref, *, nch):
    row = lax.broadcasted_iota(jnp.int32, (nch, 128), 0)
    for c in range(2):
        for pr in range(2):
            base = (c * 2 + pr) * 512
            hs = []
            for kk in range(2):
                lo = lohi_ref[:, base + kk * 256:base + kk * 256 + 128]
                hi = lohi_ref[:, base + kk * 256 + 128:base + kk * 256 + 256]
                nx = nxt_ref[:, base + kk * 256 + 128:base + kk * 256 + 256]
                hin = jnp.where(row == nch - 1, nx, pltpu.roll(hi, nch - 1, 0))
                hs.append(jax.nn.gelu(lo + hin + bias_ref[c:c + 1, :]))
            hid = jnp.concatenate(hs, axis=1).astype(BF16)
            out = _mm(hid, w2_ref[c])
            o_ref[:, c * 256 + pr * 128:c * 256 + (pr + 1) * 128] = out.astype(BF16)
            if c == 1:
                vt_ref[pr * 128:(pr + 1) * 128, :] = out.T.astype(BF16)


def _cmp_fin(lohi, nxt, bias, w2p):
    b, nch, _ = lohi.shape
    return pl.pallas_call(
        functools.partial(_cmp_fin_kernel, nch=nch),
        grid=(b,),
        in_specs=[pl.BlockSpec((None, nch, 2048), lambda i: (i, 0, 0)),
                  pl.BlockSpec((None, 1, 2048), lambda i: (i, 0, 0)),
                  _const_spec(bias.shape), _const_spec(w2p.shape)],
        out_specs=[pl.BlockSpec((None, nch, 512), lambda i: (i, 0, 0)),
                   pl.BlockSpec((None, 256, nch), lambda i: (i, 0, 0))],
        out_shape=[jax.ShapeDtypeStruct((b, nch, 512), BF16), jax.ShapeDtypeStruct((b, 256, nch), BF16)],
        compiler_params=_cparams(("parallel",)),
        name="cmp_fin",
    )(lohi, nxt, bias, w2p)


def _split3(x):
    x1 = x.astype(BF16)
    r1 = x - x1.astype(F32)
    x2 = r1.astype(BF16)
    x3 = (r1 - x2.astype(F32)).astype(BF16)
    return x1, x2, x3


def _block_rank_select(imp_ref, n_blocks, valid):
    x = imp_ref[...]
    jrow = lax.broadcasted_iota(jnp.int32, x.shape, 0)
    assert x.shape[0] % RANK_UNROLL == 0

    def body(i, cnt):
        for u in range(RANK_UNROLL):
            jp = i * RANK_UNROLL + u
            r = imp_ref[pl.ds(jp, 1), :]
            cnt = cnt + jnp.where(jrow > jp, jnp.where(r >= x, 1.0, 0.0), jnp.where(r > x, 1.0, 0.0))
        return cnt

    cnt = lax.fori_loop(0, (n_blocks + RANK_UNROLL - 1) // RANK_UNROLL, body, jnp.zeros(x.shape, F32))
    return jnp.where((cnt < SLC_TOP) & valid, 1.0, 0.0)


def _update_t(m_ref, acc_ref, sl, s, v_ext):
    m_old = m_ref[:, sl]
    m_new = jnp.maximum(m_old, s.max(axis=0, keepdims=True))
    alpha = jnp.exp2(m_old - m_new)
    p = jnp.exp2(s - m_new).astype(BF16)
    acc_ref[:, sl] = acc_ref[:, sl] * alpha + _mm(v_ext, p)
    m_ref[:, sl] = m_new


def _finish_t(acc_ref, sl):
    return acc_ref[0:128, sl] * (1.0 / jnp.maximum(acc_ref[128:129, sl], 1e-30))


def _attn_prompt_kernel(qt_ref, qrt_ref, gt_ref, kc_ref, vct_ref, ks_ref, vst_ref, kw_ref, vwt_ref, ct_ref, et_ref,
                        o_ref, imp_scr, lim_scr, qa_scr, oc_scr, m_scr, acc_scr, mw_scr, accw_scr, sa_scr, sb_scr,
                        *, tq, nch, n_slc, kc_sel, kc_win):
    k = pl.program_id(1)
    t0 = pl.program_id(2) * tq
    par = k % 2
    r = GROUP * tq

    def stack(ref):
        z = jnp.zeros((HEAD_DIM, tq), BF16)
        cols = []
        for g in range(GROUP):
            qg = ref[g * HEAD_DIM:(g + 1) * HEAD_DIM, :]
            cols.append(jnp.where(par == 0, jnp.concatenate([qg, z], axis=0), jnp.concatenate([z, qg], axis=0)))
        return jnp.concatenate(cols, axis=1)

    q4 = stack(qt_ref)
    qr4 = stack(qrt_ref)
    pos = t0 + lax.broadcasted_iota(jnp.int32, (1, tq), 1)

    vis = lax.broadcasted_iota(jnp.int32, (nch, tq), 0) <= ((pos - (CMP_BLOCK - 1)) >> 4)
    kc = kc_ref[...]
    vct = vct_ref[...]
    p_sum = None
    s_c = _mm(kc, q4)
    es, invs = [], []
    for g in range(GROUP):
        sg = jnp.where(vis, s_c[:, g * tq:(g + 1) * tq], NEG)
        e = jnp.where(vis, jnp.exp2(sg - sg.max(axis=0, keepdims=True)), 0.0)
        inv = 1.0 / jnp.maximum(e.sum(axis=0, keepdims=True), 1e-30)
        p_sum = e * inv if p_sum is None else p_sum + e * inv
        es.append(e.astype(BF16))
        invs.append(inv)
    oc_scr[...] = _mm(vct, jnp.concatenate(es, axis=1)) * jnp.concatenate(invs, axis=1)

    ct = ct_ref[...]
    imp = sum(_mm(ct, piece) for piece in _split3(p_sum))
    nb = imp.shape[0]
    jb = lax.broadcasted_iota(jnp.int32, (nb, tq), 0)
    cur = pos >> 6
    forced = (jb == 0) | (jb == cur) | (jb == cur - 1)
    valid = jb <= cur
    posb = jnp.broadcast_to(pos, (nb, tq))
    imp_scr[...] = jnp.where(valid, jnp.where(forced, jnp.inf, imp), -jnp.inf)

    @pl.when(t0 + tq <= SLC_TOP * SLC_BLOCK)
    def _():
        lim_scr[...] = jnp.where(valid, posb, -1)

    @pl.when(t0 + tq > SLC_TOP * SLC_BLOCK)
    def _():
        sel = _block_rank_select(imp_scr, jnp.minimum(n_slc, (t0 + tq - 1) // SLC_BLOCK + 1), valid)
        lim_scr[...] = jnp.where(sel > 0.5, posb, -1)

    bias = jnp.where(lim_scr[...] >= 0, 0.0, NEG).astype(BF16)
    bias = jnp.concatenate([bias, jnp.zeros((128 - nb, tq), BF16)], axis=0)
    qa_scr[0:128, :] = qr4
    qa_scr[128:256, :] = jnp.concatenate([bias] * GROUP, axis=1)
    m_scr[...] = jnp.full((1, r), NEG, F32)
    acc_scr[...] = jnp.zeros(acc_scr.shape, F32)
    ones_s = jnp.ones((16, kc_sel), BF16)

    def scores(c):
        k0 = pl.multiple_of(c * kc_sel, kc_sel)
        k_aug = jnp.concatenate([ks_ref[pl.ds(k0, kc_sel), :], et_ref[pl.ds(k0, kc_sel), :]], axis=1)
        return _mm(k_aug, qa_scr[...])

    def soft_pv(s_ref, c, causal):
        k0 = pl.multiple_of(c * kc_sel, kc_sel)
        s = s_ref[...]
        if causal:
            al = (k0 + lax.broadcasted_iota(jnp.int32, (kc_sel, tq), 0)) <= pos
            s = jnp.concatenate([jnp.where(al, s[:, g * tq:(g + 1) * tq], NEG) for g in range(GROUP)], axis=1)
        _update_t(m_scr, acc_scr, slice(None), s, jnp.concatenate([vst_ref[:, pl.ds(k0, kc_sel)], ones_s], axis=0))

    c_last = t0 // kc_sel
    sa_scr[...] = scores(0)

    def pair_body(i, carry):
        sb_scr[...] = scores(2 * i + 1)
        soft_pv(sa_scr, 2 * i, False)
        sa_scr[...] = scores(2 * i + 2)
        soft_pv(sb_scr, 2 * i + 1, False)
        return carry

    lax.fori_loop(0, c_last // 2, pair_body, 0)

    @pl.when(c_last % 2 == 0)
    def _():
        soft_pv(sa_scr, c_last, True)

    @pl.when(c_last % 2 == 1)
    def _():
        sb_scr[...] = scores(c_last)
        soft_pv(sa_scr, c_last - 1, False)
        soft_pv(sb_scr, c_last, True)

    mw_scr[...] = jnp.full((1, r), NEG, F32)
    accw_scr[...] = jnp.zeros(accw_scr.shape, F32)
    ones_w = jnp.ones((16, kc_win), BF16)
    k0w = jnp.maximum(t0 - WINDOW, 0)
    for i in range((WINDOW + tq) // kc_win):
        kk = pl.multiple_of(k0w + i * kc_win, tq)
        dp = pos - (kk + lax.broadcasted_iota(jnp.int32, (kc_win, tq), 0))
        al = lax.bitcast_convert_type(dp, jnp.uint32) < WINDOW
        s = _mm(kw_ref[pl.ds(kk, kc_win), :], qa_scr[0:128, :])
        s = jnp.concatenate([jnp.where(al, s[:, g * tq:(g + 1) * tq], NEG) for g in range(GROUP)], axis=1)
        _update_t(mw_scr, accw_scr, slice(None), s,
                  jnp.concatenate([vwt_ref[:, pl.ds(kk, kc_win)], ones_w], axis=0))

    gt = gt_ref[...]
    for g in range(GROUP):
        sl = slice(g * tq, (g + 1) * tq)
        og = gt[g:g + 1] * oc_scr[:, sl] + gt[GROUP + g:GROUP + g + 1] * _finish_t(acc_scr, sl) \
            + gt[2 * GROUP + g:2 * GROUP + g + 1] * _finish_t(accw_scr, sl)
        o_ref[g * HEAD_DIM:(g + 1) * HEAD_DIM, :] = jnp.where(par == 0, og[0:HEAD_DIM], og[HEAD_DIM:]).astype(BF16)


def _attn_prompt(qt, qrt, gt, kvc, vct, ks, vst, kw, vwt, ct, et, bsz, seq, tq, kc_sel, kc_win):
    nch = kvc.shape[1]
    nt = seq // tq
    n_slc = seq // SLC_BLOCK
    nb = ct.shape[0]
    r = GROUP * tq
    qspec = pl.BlockSpec((256, tq), lambda b, k, t: (b * KV_HEADS + k, t))
    kspec = pl.BlockSpec((seq, 128), lambda b, k, t: (b, k // 2))
    vspec = pl.BlockSpec((128, seq), lambda b, k, t: (b * 2 + k // 2, 0))
    return pl.pallas_call(
        functools.partial(_attn_prompt_kernel, tq=tq, nch=nch, n_slc=n_slc, kc_sel=kc_sel, kc_win=kc_win),
        grid=(bsz, KV_HEADS, nt),
        in_specs=[qspec, qspec,
                  pl.BlockSpec((GATE_ROWS, tq), lambda b, k, t: (b * KV_HEADS + k, t)),
                  pl.BlockSpec((None, nch, 128), lambda b, k, t: (b, 0, k // 2)),
                  pl.BlockSpec((None, 128, nch), lambda b, k, t: (b, k // 2, 0)),
                  kspec, vspec, kspec, vspec, _const_spec(ct.shape), _const_spec(et.shape)],
        out_specs=pl.BlockSpec((256, tq), lambda b, k, t: (b * KV_HEADS + k, t)),
        out_shape=jax.ShapeDtypeStruct((bsz * 1024, seq), BF16),
        scratch_shapes=[pltpu.VMEM((nb, tq), F32), pltpu.VMEM((nb, tq), jnp.int32), pltpu.VMEM((256, r), BF16),
                        pltpu.VMEM((128, r), F32),
                        pltpu.VMEM((1, r), F32), pltpu.VMEM((128 + 16, r), F32),
                        pltpu.VMEM((1, r), F32), pltpu.VMEM((128 + 16, r), F32),
                        pltpu.VMEM((kc_sel, r), F32), pltpu.VMEM((kc_sel, r), F32)],
        compiler_params=_cparams(("parallel", "parallel", "arbitrary")),
        name="attn_prompt",
    )(qt, qrt, gt, kvc, vct, ks, vst, kw, vwt, ct, et)


def _pad_q(qf, par, g, lane):
    slab = qf[:, (g // 2) * 128:(g // 2 + 1) * 128]
    rolled = pltpu.roll(slab, 64, 1)
    src = slab if par == g % 2 else rolled
    return jnp.where((lane >> 6) == par, src, 0.0)


def _stack_q(q_blk, par):
    qf = q_blk.astype(F32)
    lane = lax.broadcasted_iota(jnp.int32, (qf.shape[0], 128), 1)
    return jnp.concatenate([_pad_q(qf, par, g, lane) for g in range(GROUP)], axis=0).astype(BF16)


def _softmax_parts(parts, r):
    ss = []
    for s, al, _ in parts:
        ss.append(jnp.where(al[None], s.reshape(GROUP, r, s.shape[-1]), NEG))
    m = ss[0].max(axis=-1, keepdims=True)
    for s in ss[1:]:
        m = jnp.maximum(m, s.max(axis=-1, keepdims=True))
    l = jnp.zeros_like(m)
    acc = jnp.zeros((GROUP, r, 128), F32)
    ps = []
    for (_, al, pv), s in zip(parts, ss):
        p = jnp.where(al[None], jnp.exp(s - m), 0.0)
        ps.append(p)
        l = l + p.sum(axis=-1, keepdims=True)
        acc = acc + pv(p.reshape(GROUP * r, p.shape[-1]).astype(BF16)).reshape(GROUP, r, 128)
    inv = 1.0 / jnp.maximum(l, 1e-30)
    return acc * inv, [p * inv for p in ps]


def _online_update(m_ref, l_ref, acc_ref, s, allowed, pv, r):
    n = s.shape[-1]
    s3 = jnp.where(allowed[None], s.reshape(GROUP, r, n), NEG)
    m_old = m_ref[...]
    m_new = jnp.maximum(m_old, s3.max(axis=-1, keepdims=True))
    alpha = jnp.exp(m_old - m_new)
    p = jnp.where(allowed[None], jnp.exp(s3 - m_new), 0.0)
    l_ref[...] = alpha * l_ref[...] + p.sum(axis=-1, keepdims=True)
    acc_ref[...] = alpha * acc_ref[...] + pv(p.reshape(GROUP * r, n).astype(BF16)).reshape(GROUP, r, 128)
    m_ref[...] = m_new


def _place_heads(o4, par, r):
    lane = lax.broadcasted_iota(jnp.int32, (r, 128), 1)
    outs = []
    for pr in range(2):
        lo, hi = o4[2 * pr], o4[2 * pr + 1]
        lo = lo if par == 0 else pltpu.roll(lo, 64, 1)
        hi = hi if par == 1 else pltpu.roll(hi, 64, 1)
        outs.append(jnp.where(lane < 64, lo, hi))
    return jnp.concatenate(outs, axis=1)


def _gate(gn, br, r):
    return jnp.stack([gn[:, br * GROUP + g:br * GROUP + g + 1] for g in range(GROUP)], axis=0)


def _pair_t(ref, c, pr):
    return ref[c, 2 * pr:2 * pr + 2].reshape(2 * HEAD_DIM, ref.shape[-1]).astype(BF16)


def _samp_a_kernel(q_ref, qr_ref, gn_ref, kvc_ref, win_ref, kvwn_ref, ct_ref, o_ref, sel_ref, imp_scr,
                   *, t, nch, n_slc, past, wb):
    nb = ct_ref.shape[0]
    p_sums = []
    pos_c = past + lax.broadcasted_iota(jnp.int32, (t, nch), 0)
    cend = lax.broadcasted_iota(jnp.int32, (t, nch), 1) * CMP_STRIDE + (CMP_BLOCK - 1)
    tt = lax.broadcasted_iota(jnp.int32, (t, wb), 0)
    ii = lax.broadcasted_iota(jnp.int32, (t, wb), 1)
    dp_buf = wb + tt - ii
    al_buf = (dp_buf >= 0) & (dp_buf < WINDOW) & (past - wb + ii >= 0)
    nn = kvwn_ref.shape[0]
    dp_new = lax.broadcasted_iota(jnp.int32, (t, nn), 0) - lax.broadcasted_iota(jnp.int32, (t, nn), 1)
    al_new = (dp_new >= 0) & (dp_new < WINDOW)
    for k in range(KV_HEADS):
        par, pr = k % 2, k // 2
        q4 = _stack_q(q_ref[:, k * 256:(k + 1) * 256], par)
        qr4 = _stack_q(qr_ref[:, k * 256:(k + 1) * 256], par)
        ksl = slice(pr * 128, (pr + 1) * 128)
        vsl = slice(256 + pr * 128, 256 + (pr + 1) * 128)
        kc, vc = kvc_ref[:, ksl], kvc_ref[:, vsl]
        o_cmp, (p_cmp,) = _softmax_parts([(_nt(q4, kc), cend <= pos_c, lambda p, vc=vc: _mm(p, vc))], t)
        p_sums.append(p_cmp.sum(axis=0))
        kwt, vwt = _pair_t(win_ref, 0, pr), _pair_t(win_ref, 1, pr)
        kn, vn = kvwn_ref[:, ksl].astype(BF16), kvwn_ref[:, vsl].astype(BF16)
        o_win, _ = _softmax_parts([(_mm(qr4, kwt), al_buf, lambda p, vwt=vwt: _nt(p, vwt)),
                                   (_nt(qr4, kn), al_new, lambda p, vn=vn: _mm(p, vn))], t)
        gn = gn_ref[k]
        o4 = _gate(gn, 0, t) * o_cmp + _gate(gn, 2, t) * o_win
        o_ref[:, k * 256:(k + 1) * 256] = _place_heads(o4, par, t)
    p_all = jnp.concatenate(p_sums + [jnp.zeros((128 - KV_HEADS * t, nch), F32)], axis=0)
    ct = ct_ref[...]
    imp = sum(_nt(ct, piece) for piece in _split3(p_all))
    jb = lax.broadcasted_iota(jnp.int32, (nb, 128), 0)
    col = lax.broadcasted_iota(jnp.int32, (nb, 128), 1)
    cur = (past + (col & (t - 1))) >> 6
    forced = (jb == 0) | (jb == cur) | (jb == cur - 1)
    valid = jb <= cur
    imp_scr[...] = jnp.where(valid, jnp.where(forced, jnp.inf, imp), -jnp.inf)
    sel_ref[...] = _block_rank_select(imp_scr, n_slc, valid)


def _samp_a(q3, qr3, gn4, kvc, win_t, layer, kvwn3, ct, past):
    b, t, _ = q3.shape
    nch = kvc.shape[1]
    nb = ct.shape[0]
    wb = win_t.shape[-1]
    n_slc = (past + t + SLC_BLOCK - 1) // SLC_BLOCK
    per_b = lambda r, c: pl.BlockSpec((None, r, c), lambda i: (i, 0, 0))
    return pl.pallas_call(
        functools.partial(_samp_a_kernel, t=t, nch=nch, n_slc=n_slc, past=past, wb=wb),
        grid=(b,),
        in_specs=[per_b(t, 1024), per_b(t, 1024),
                  pl.BlockSpec((KV_HEADS, None, t, 128), lambda i: (0, i, 0, 0)),
                  per_b(nch, 512),
                  pl.BlockSpec((None, None, 2, KV_HEADS, HEAD_DIM, wb), lambda i: (layer, i, 0, 0, 0, 0)),
                  per_b(kvwn3.shape[1], 512), _const_spec(ct.shape)],
        out_specs=[per_b(t, 1024), per_b(nb, 128)],
        out_shape=[jax.ShapeDtypeStruct((b, t, 1024), F32), jax.ShapeDtypeStruct((b, nb, 128), F32)],
        scratch_shapes=[pltpu.VMEM((nb, 128), F32)],
        compiler_params=_cparams(("parallel",)),
        name="samp_a",
    )(q3, qr3, gn4, kvc, win_t, kvwn3, ct)


def _samp_b_kernel(pt_ref, *refs, pg, t):
    del pt_ref
    pages = refs[:pg]
    qr_ref, sel_ref, seln_ref, e_ref, gn_ref, kvn_ref, op_ref, o_ref, m_scr, l_scr, acc_scr, s_scr = refs[pg:]
    s = pl.program_id(1)
    ns = pl.num_programs(1)
    nbc = sel_ref.shape[0]

    @pl.when(s == 0)
    def _():
        m_scr[...] = jnp.full(m_scr.shape, NEG, F32)
        l_scr[...] = jnp.zeros(l_scr.shape, F32)
        acc_scr[...] = jnp.zeros(acc_scr.shape, F32)

    selc = jnp.concatenate([sel_ref[...], jnp.zeros((128 - nbc, 128), F32)], axis=0).T
    msk_all = _mm(selc[:KV_HEADS * t].astype(BF16), e_ref[...])
    for pr in range(2):
        q_pair = jnp.concatenate([_stack_q(qr_ref[:, k * 256:(k + 1) * 256], k % 2) for k in (2 * pr, 2 * pr + 1)],
                                 axis=0)
        s_scr[pr] = _mm(q_pair, jnp.concatenate([_pair_t(p, 0, pr) for p in pages], axis=1))
    rows = GROUP * t
    for pr in range(2):
        vt = jnp.concatenate([_pair_t(p, 1, pr) for p in pages], axis=1)
        ps, alphas = [], []
        for hh in range(2):
            k = 2 * pr + hh
            al = (msk_all[k * t:(k + 1) * t] > 0.5)[None]
            s3 = jnp.where(al, s_scr[pr, hh * rows:(hh + 1) * rows, :].reshape(GROUP, t, -1), NEG)
            m_old = m_scr[k]
            m_new = jnp.maximum(m_old, s3.max(axis=-1, keepdims=True))
            alpha = jnp.exp(m_old - m_new)
            p = jnp.where(al, jnp.exp(s3 - m_new), 0.0)
            l_scr[k] = alpha * l_scr[k] + p.sum(axis=-1, keepdims=True)
            m_scr[k] = m_new
            ps.append(p.reshape(rows, -1).astype(BF16))
            alphas.append(alpha)
        pv = _nt(jnp.concatenate(ps, axis=0), vt)
        for hh in range(2):
            k = 2 * pr + hh
            acc_scr[k] = alphas[hh] * acc_scr[k] + pv[hh * rows:(hh + 1) * rows].reshape(GROUP, t, 128)

    @pl.when(s == ns - 1)
    def _():
        nn = kvn_ref.shape[0]
        dp = lax.broadcasted_iota(jnp.int32, (t, nn), 0) - lax.broadcasted_iota(jnp.int32, (t, nn), 1)
        seln = jnp.concatenate([seln_ref[...], jnp.zeros((128 - 8, 128), F32)], axis=0).T
        for k in range(KV_HEADS):
            par, pr = k % 2, k // 2
            qr4 = _stack_q(qr_ref[:, k * 256:(k + 1) * 256], par)
            kk = kvn_ref[:, pr * 128:(pr + 1) * 128].astype(BF16)
            vv = kvn_ref[:, 256 + pr * 128:256 + (pr + 1) * 128].astype(BF16)
            al = (dp >= 0) & (seln[k * t:(k + 1) * t, 0:1] > 0.5)
            _online_update(m_scr.at[k], l_scr.at[k], acc_scr.at[k], _nt(qr4, kk), al, lambda p, vv=vv: _mm(p, vv), t)
            o_sel = acc_scr[k] * (1.0 / jnp.maximum(l_scr[k], 1e-30))
            o4 = _gate(gn_ref[k], 1, t) * o_sel
            o_ref[:, k * 256:(k + 1) * 256] = (op_ref[:, k * 256:(k + 1) * 256] + _place_heads(o4, par, t)).astype(BF16)


def _samp_b(cache_t, layer, page_table, qr3, sel, e, gn4, kvn3, o_part, pg):
    b, n_pages = page_table.shape
    t = qr3.shape[1]
    ns = n_pages // pg
    nbc = pg * (PAGE_SIZE // SLC_BLOCK)
    n_past_blk = n_pages * (PAGE_SIZE // SLC_BLOCK)

    def page_spec(i):
        return pl.BlockSpec((None, None, 2, KV_HEADS, HEAD_DIM, PAGE_SIZE),
                            lambda bb, s, pt: (layer, pt[bb, s * pg + i], 0, 0, 0, 0))

    per_b = lambda r, c: pl.BlockSpec((None, r, c), lambda bb, s, pt: (bb, 0, 0))
    return pl.pallas_call(
        functools.partial(_samp_b_kernel, pg=pg, t=t),
        grid_spec=pltpu.PrefetchScalarGridSpec(
            num_scalar_prefetch=1,
            grid=(b, ns),
            in_specs=[page_spec(i) for i in range(pg)] + [
                per_b(t, 1024),
                pl.BlockSpec((None, nbc, 128), lambda bb, s, pt: (bb, s, 0)),
                pl.BlockSpec((None, 8, 128), lambda bb, s, pt: (bb, n_past_blk // 8, 0)),
                _const_spec(e.shape),
                pl.BlockSpec((KV_HEADS, None, t, 128), lambda bb, s, pt: (0, bb, 0, 0)),
                per_b(kvn3.shape[1], 512), per_b(t, 1024)],
            out_specs=per_b(t, 1024),
            scratch_shapes=[pltpu.VMEM((KV_HEADS, GROUP, t, 1), F32), pltpu.VMEM((KV_HEADS, GROUP, t, 1), F32),
                            pltpu.VMEM((KV_HEADS, GROUP, t, 128), F32),
                            pltpu.VMEM((2, 2 * GROUP * t, pg * PAGE_SIZE), F32)],
        ),
        out_shape=jax.ShapeDtypeStruct((b, t, 1024), BF16),
        compiler_params=_cparams(("parallel", "arbitrary")),
        name="samp_b",
    )(page_table, *([cache_t] * pg), qr3, sel, sel, e, gn4, kvn3, o_part)


def _mixpre_kernel(u_ref, up_ref, a_ref, ap_ref, cw_ref, cb_ref, lg_ref, lb_ref, pool_ref, cact_ref,
                   u_scr, a_scr, ph_scr, *, tm, tiles_per_seq, pos_base, has_state):
    ti = pl.program_id(0) % tiles_per_seq
    pos0 = pos_base + ti * tm
    keep = 1.0 if has_state else jnp.where(ti == 0, 0.0, 1.0)
    u = u_ref[...]
    u_scr[0:POOL_PREFIX, :] = up_ref[...] * keep
    u_scr[POOL_PREFIX:, :] = u
    a_scr[0:CONV_PREFIX, :] = ap_ref[...] * keep
    a_scr[CONV_PREFIX:, :] = a_ref[...]
    pos = pos0 + lax.broadcasted_iota(jnp.int32, (tm, 128), 0)
    for gi, w in enumerate(POOL_WINDOWS):
        sl = slice(gi * 128, (gi + 1) * 128)
        acc = u[:, sl]
        for d in range(1, w):
            acc = acc + u_scr[pl.ds(POOL_PREFIX - d, tm), sl]
        cnt = jnp.minimum(pos + 1, w).astype(F32)
        pool_ref[:, sl] = (acc / cnt - u[:, sl]).astype(BF16)
    y = jnp.zeros((tm, 512), F32) + cb_ref[...]
    base = CONV_PREFIX - (CONV_WIDTH - 1)
    span = tm + CONV_PREFIX - 8
    for ph in range(8):
        offs = [o for o in range(base, base + CONV_WIDTH) if o % 8 == ph]
        if ph:
            ph_scr[0:span, :] = a_scr[pl.ds(ph, span), :]
        for o in offs:
            src = a_scr[pl.ds(o, tm), :] if ph == 0 else ph_scr[pl.ds(o - ph, tm), :]
            y = y + src * cw_ref[o - base:o - base + 1, :]
    mu = jnp.mean(y, axis=-1, keepdims=True)
    yc = y - mu
    var = jnp.mean(yc * yc, axis=-1, keepdims=True)
    yn = yc * lax.rsqrt(var + 1e-5) * lg_ref[...] + lb_ref[...]
    cact_ref[...] = (yn * jax.nn.sigmoid(yn)).astype(BF16)


def _mixpre(u, u_state, a, a_state, cw, cb, lg, lb, tm, tiles_per_seq, pos_base):
    has_state = u_state is not None
    if has_state:
        up, ap = u_state, a_state
        n_tiles = u.shape[0]
        up_spec = pl.BlockSpec((None, POOL_PREFIX, 512), lambda i: (i, 0, 0))
        ap_spec = pl.BlockSpec((None, CONV_PREFIX, 512), lambda i: (i, 0, 0))
        row = pl.BlockSpec((None, tm, 512), lambda i: (i, 0, 0))
        out_shape = (n_tiles, tm, 512)
    else:
        up, ap = u, a
        n_tiles = u.shape[0] // tm
        up_spec = pl.BlockSpec((POOL_PREFIX, 512), lambda i: (jnp.maximum(i * (tm // POOL_PREFIX) - 1, 0), 0))
        ap_spec = pl.BlockSpec((CONV_PREFIX, 512), lambda i: (jnp.maximum(i * (tm // CONV_PREFIX) - 1, 0), 0))
        row = pl.BlockSpec((tm, 512), lambda i: (i, 0))
        out_shape = (u.shape[0], 512)
    return pl.pallas_call(
        functools.partial(_mixpre_kernel, tm=tm, tiles_per_seq=tiles_per_seq, pos_base=pos_base, has_state=has_state),
        grid=(n_tiles,),
        in_specs=[row, up_spec, row, ap_spec, _const_spec(cw.shape), _const_spec(cb.shape),
                  _const_spec(lg.shape), _const_spec(lb.shape)],
        out_specs=[row, row],
        out_shape=[jax.ShapeDtypeStruct(out_shape, BF16), jax.ShapeDtypeStruct(out_shape, BF16)],
        scratch_shapes=[pltpu.VMEM((tm + POOL_PREFIX, 512), F32), pltpu.VMEM((tm + CONV_PREFIX, 512), F32),
                        pltpu.VMEM((tm + CONV_PREFIX, 512), F32)],
        compiler_params=_cparams(("parallel",)),
        name="mixpre",
    )(u, up, a, ap, cw, cb, lg, lb)


def _dense_kernel(x_ref, pool_ref, onsa_ref, cact_ref, gm_ref, pw_ref, ps_ref, wbp_ref, wbn_ref, wbc_ref, wo_ref,
                  nf_ref, wg_ref, wu_ref, wd_ref, nfin_ref, o_ref, *, final, ff_chunk, onsa_t):
    op = jnp.concatenate([_mm(pool_ref[:, g * 128:(g + 1) * 128], pw_ref[g]) for g in range(4)], axis=1)
    op = (op * ps_ref[...]).astype(BF16)
    onsa = onsa_ref[...].astype(F32).T.astype(BF16) if onsa_t else onsa_ref[...]
    m = gm_ref[:, 0:1024] * _mm(op, wbp_ref[...])
    m = m + gm_ref[:, 1024:2048] * _mm(onsa, wbn_ref[...])
    m = m + gm_ref[:, 2048:3072] * _mm(cact_ref[...], wbc_ref[...])
    x1 = x_ref[...] + _mm(m.astype(BF16), wo_ref[...])
    hb = _rms(x1, nf_ref[...]).astype(BF16)
    dff = wg_ref.shape[1]
    acc = x1
    for c0 in range(0, dff, ff_chunk):
        gt = _mm(hb, wg_ref[:, c0:c0 + ff_chunk])
        up = _mm(hb, wu_ref[:, c0:c0 + ff_chunk])
        acc = acc + _mm((gt * jax.nn.sigmoid(gt) * up).astype(BF16), wd_ref[c0:c0 + ff_chunk, :])
    o_ref[...] = _rms(acc, nfin_ref[...]) if final else acc


def _dense(x2, pooled, onsa, cact, gm, weights, tm, final, seq=None):
    m = x2.shape[0]
    row = lambda c: pl.BlockSpec((tm, c), lambda i: (i, 0))
    if seq is None:
        onsa_spec = row(1024)
    else:
        nt = seq // tm
        onsa_spec = pl.BlockSpec((1024, tm), lambda i: (i // nt, i % nt))
    dff = weights[8].shape[1]
    ff_chunk = dff // 2 if (dff // 2) % 128 == 0 else dff
    return pl.pallas_call(
        functools.partial(_dense_kernel, final=final, ff_chunk=ff_chunk, onsa_t=seq is not None),
        grid=(m // tm,),
        in_specs=[row(1024), row(512), onsa_spec, row(512), row(3072)] + [_const_spec(w.shape) for w in weights],
        out_specs=row(1024),
        out_shape=jax.ShapeDtypeStruct((m, 1024), F32),
        compiler_params=_cparams(("parallel",)),
        name="dense",
    )(x2, pooled, onsa, cact, gm, *weights)


def _rope_angles(pos):
    inv = ROPE_THETA ** (-jnp.arange(ROPE_HALF, dtype=F32) * (2.0 / ROPE_DIM))
    return pos.astype(F32)[:, None] * inv[None, :]


def _rope_table(pos):
    ang = _rope_angles(pos)
    cos, sin = jnp.cos(ang), jnp.sin(ang)
    t = pos.shape[0]
    zeros = jnp.zeros((t, HEAD_DIM - ROPE_DIM), F32)
    zh = jnp.zeros((t, ROPE_HALF), F32)
    c = jnp.concatenate([cos, cos, zeros + 1.0], axis=1)
    s1 = jnp.concatenate([-sin, zh, zeros], axis=1)
    s2 = jnp.concatenate([zh, sin, zeros], axis=1)
    return jnp.concatenate([c, c, s1, s1, s2, s2], axis=1)


def _rope_table_t(pos):
    ang = _rope_angles(pos)
    return jnp.concatenate([jnp.cos(ang).T, jnp.sin(ang).T], axis=0)


def _cmp_to_slc_t(n_cmp_rows, nb):
    i0 = jnp.arange(n_cmp_rows)[None, :] * CMP_STRIDE
    j0 = jnp.arange(nb)[:, None] * SLC_BLOCK
    ov = jnp.clip(jnp.minimum(i0 + CMP_BLOCK, j0 + SLC_BLOCK) - jnp.maximum(i0, j0), 0, None)
    return (ov.astype(F32) / CMP_BLOCK).astype(BF16)


def _expand_matrix(n_rows, n_keys):
    return (jnp.arange(n_rows)[:, None] == jnp.arange(n_keys)[None, :] // SLC_BLOCK).astype(BF16)


def _layer_weights(l, norm_mix, w_in, pool_w, pool_scale, cmp_w1, cmp_pe, cmp_w2, conv_w, conv_b, conv_ln_g,
                   conv_ln_b, w_br_pool, w_br_nsa, w_br_conv, w_out, norm_ffn, w_gate, w_up, w_down, norm_final):
    w = w_in[l]
    s_q, s_kv, s_gate = 512, 512 + 1024, 512 + 1024 + 1536
    n_g = 3 * N_HEADS
    order = [(k, br, g) for k in range(KV_HEADS) for br in range(3) for g in range(GROUP)]
    idx = jnp.array([s_gate + (k * GROUP + g) * 3 + br for k, br, g in order])
    w_s = jnp.concatenate([w[:, :s_gate], w[:, idx], jnp.zeros((w.shape[0], GATE_COLS - n_g), w.dtype),
                           w[:, s_gate + n_g:]], axis=1).astype(BF16)
    w_rm = jnp.concatenate([w[:, :s_q], w[:, s_kv:s_kv + 512], w[:, s_gate + n_g:]], axis=1).astype(BF16)
    gate_rows = jnp.zeros((KV_HEADS * GATE_ROWS, w.shape[0]), w.dtype)
    gate_rows = gate_rows.at[jnp.array([k * GATE_ROWS + br * GROUP + g for k, br, g in order])].set(w[:, idx].T)
    w_fm = jnp.concatenate([w[:, s_q:s_gate].T, gate_rows], axis=0).astype(BF16)
    w1 = cmp_w1[l]
    eye2 = jnp.eye(2, dtype=F32)
    w1h = w1.reshape(2, 2, CMP_STRIDE, HEAD_DIM, 128)
    w1p = jnp.einsum('chsde,kq->cskdqhe', w1h, eye2).reshape(2, CMP_STRIDE * 128, 512).astype(BF16)
    w2p = jnp.einsum('ced,kq->ckeqd', cmp_w2[l], eye2).reshape(2, 256, 128).astype(BF16)
    bias = jnp.einsum('cpd,cpde->ce', cmp_pe[l], w1, precision=lax.Precision.HIGHEST)
    cw = jnp.concatenate([conv_w[l], jnp.zeros((1, conv_w.shape[2]), F32)], axis=0)
    dense_w = (pool_w[l].astype(BF16), pool_scale[l][None], w_br_pool[l].astype(BF16), w_br_nsa[l].astype(BF16),
               w_br_conv[l].astype(BF16), w_out[l].astype(BF16), norm_ffn[l][None], w_gate[l].astype(BF16),
               w_up[l].astype(BF16), w_down[l].astype(BF16), norm_final[None])
    return dict(g=norm_mix[l][None], w_s=w_s, w_rm=w_rm, w_fm=w_fm, w1p=w1p, w2p=w2p, bias=bias, cw=cw,
                cb=conv_b[l][None], lg=conv_ln_g[l][None], lb=conv_ln_b[l][None], dense=dense_w)


def kernel(x_prompt, x_sample, cache_cmp_kv, cache_slc_kv, state_win_kv, state_pool, state_conv, page_table,
           norm_mix, w_in, pool_w, pool_scale, cmp_w1, cmp_pe, cmp_w2, conv_w, conv_b, conv_ln_g, conv_ln_b,
           w_br_pool, w_br_nsa, w_br_conv, w_out, norm_ffn, w_gate, w_up, w_down, norm_final):
    bp, sp, d = x_prompt.shape
    bs, ts, _ = x_sample.shape
    depth = w_in.shape[0]
    n_pages = page_table.shape[1]
    past = n_pages * PAGE_SIZE
    wb = state_win_kv.shape[2]
    assert d == 1024 and sp % 512 == 0 and sp >= WINDOW + 256 and ts == 8 and n_pages % 4 == 0

    tm = 256
    tq = 256
    kc_sel = 512
    kc_win = WINDOW + tq
    pg = 16 if n_pages % 16 == 0 else 4
    pg_cmp = 32 if n_pages % 32 == 0 else pg
    mp, ms = bp * sp, bs * ts

    rope_pt = _rope_table_t(jnp.arange(sp))
    rope_s = jnp.tile(_rope_table(past + jnp.arange(ts)), (bs, 1))
    nch_p = sp // CMP_STRIDE
    nb_p = ((sp // SLC_BLOCK + 7) // 8) * 8
    ct_p = _cmp_to_slc_t(nch_p, nb_p)
    nch_s = past // CMP_STRIDE
    n_slc_s = (past + ts + SLC_BLOCK - 1) // SLC_BLOCK
    nb_s = ((n_slc_s + 7) // 8) * 8
    ct_s = _cmp_to_slc_t(nch_s, nb_s)
    e_s = _expand_matrix(128, pg * PAGE_SIZE)
    et_p = _expand_matrix(128, sp).T

    fm = lambda z: jnp.transpose(z, (0, 1, 3, 4, 5, 2))
    cache_cmp_t, cache_slc_t, win_t = fm(cache_cmp_kv), fm(cache_slc_kv), fm(state_win_kv)

    xp = x_prompt.reshape(mp, d)
    xs = x_sample.reshape(ms, d)
    outs = {n: [] for n in ("cmp_p", "cmp_s", "slc_p", "slc_s", "win_p", "win_s", "pool_p", "pool_s", "conv_p", "conv_s")}
    kvshape = (2, KV_HEADS, HEAD_DIM)
    tok_major = lambda z: jnp.transpose(z.reshape(bp, *kvshape, z.shape[-1]), (0, 4, 1, 2, 3))
    for l in range(depth):
        lw = _layer_weights(l, norm_mix, w_in, pool_w, pool_scale, cmp_w1, cmp_pe, cmp_w2, conv_w, conv_b, conv_ln_g,
                            conv_ln_b, w_br_pool, w_br_nsa, w_br_conv, w_out, norm_ffn, w_gate, w_up, w_down, norm_final)
        final = l == depth - 1

        (u, a, gm, kvc, ks, kw, qt, qrt, kvct, kvst, kvwt, vst, vwt, gt) = _proj_prompt(
            xp, lw["g"], lw["w_rm"], lw["w_fm"], rope_pt, bp, sp, tm)
        lohi = _cmp_lohi_dense(kvc, lw["w1p"], sp).reshape(bp, nch_p, 2048)
        kvcb, vct = _cmp_fin(lohi, jnp.zeros((bp, 1, 2048), F32), lw["bias"], lw["w2p"])
        onsa_t = _attn_prompt(qt, qrt, gt, kvcb, vct, ks, vst, kw, vwt, ct_p, et_p, bp, sp, tq, kc_sel, kc_win)
        pooled, cact = _mixpre(u, None, a, None, lw["cw"], lw["cb"], lw["lg"], lw["lb"], tm, sp // tm, 0)
        xp = _dense(xp, pooled, onsa_t, cact, gm, lw["dense"], tm, final, seq=sp)
        outs["cmp_p"].append(tok_major(kvct))
        outs["slc_p"].append(tok_major(kvst))
        outs["win_p"].append(tok_major(kvwt[:, -min(WINDOW, sp):]))
        outs["pool_p"].append(u.reshape(bp, sp, 512)[:, -(POOL_PREFIX - 1):])
        outs["conv_p"].append(a.reshape(bp, sp, 512)[:, -(CONV_WIDTH - 1):])

        u, q, qr, kvc, kvs, kvw, gn, a, gm = _proj_sample(xs, lw["g"], lw["w_s"], rope_s)
        pad8 = lambda z: jnp.concatenate([z.reshape(bs, ts, 512), jnp.zeros((bs, 16 - ts, 512), F32)], axis=1)
        lohi = _cmp_lohi_paged(cache_cmp_t, l, page_table, lw["w1p"], pg_cmp)
        lohi_new = _cmp_lohi_dense(pad8(kvc).reshape(bs * 16, 512), lw["w1p"], bs * 16)
        kvcb, _ = _cmp_fin(lohi, lohi_new.reshape(bs, 1, 2048), lw["bias"], lw["w2p"])
        q3, qr3 = q.reshape(bs, ts, 1024), qr.reshape(bs, ts, 1024)
        gn4 = gn.reshape(KV_HEADS, bs, ts, 128)
        o_part, sel = _samp_a(q3, qr3, gn4, kvcb, win_t, l, pad8(kvw), ct_s, past)
        onsa = _samp_b(cache_slc_t, l, page_table, qr3, sel, e_s, gn4, pad8(kvs), o_part, pg)
        u_state = jnp.concatenate([jnp.zeros((bs, 1, 512), F32), state_pool[l]], axis=1)
        a_state = jnp.concatenate([jnp.zeros((bs, 2, 512), F32), state_conv[l]], axis=1)
        pooled, cact = _mixpre(u.reshape(bs, ts, 512), u_state, a.reshape(bs, ts, 512), a_state,
                               lw["cw"], lw["cb"], lw["lg"], lw["lb"], ts, 1, past)
        xs = _dense(xs, pooled.reshape(ms, 512), onsa.reshape(ms, 1024), cact.reshape(ms, 512), gm, lw["dense"],
                    ms, final)
        outs["cmp_s"].append(kvc.reshape(bs, ts, *kvshape))
        outs["slc_s"].append(kvs.reshape(bs, ts, *kvshape))
        kvw_t = jnp.transpose(kvw.reshape(bs, ts, *kvshape), (0, 2, 3, 4, 1))
        win_new_t = jnp.concatenate([win_t[l], kvw_t], axis=-1)[..., -wb:]
        outs["win_s"].append(jnp.transpose(win_new_t, (0, 4, 1, 2, 3)))
        outs["pool_s"].append(jnp.concatenate([state_pool[l], u.reshape(bs, ts, 512)], axis=1)[:, -(POOL_PREFIX - 1):])
        outs["conv_s"].append(jnp.concatenate([state_conv[l], a.reshape(bs, ts, 512)], axis=1)[:, -(CONV_WIDTH - 1):])

    st = lambda n: jnp.stack(outs[n])
    return (xp.reshape(bp, sp, d), xs.reshape(bs, ts, d), st("cmp_p"), st("cmp_s"), st("slc_p"), st("slc_s"),
            st("win_p"), st("win_s"), st("pool_p"), st("pool_s"), st("conv_p"), st("conv_s"))
```

```python
import functools

import jax
import jax.numpy as jnp
from jax import lax
from jax.experimental import pallas as pl
from jax.experimental.pallas import tpu as pltpu

F32 = jnp.float32
BF16 = jnp.bfloat16

HEAD_DIM = 64
N_HEADS = 16
KV_HEADS = 4
GROUP = N_HEADS // KV_HEADS
ROPE_DIM = HEAD_DIM // 4
ROPE_HALF = ROPE_DIM // 2
ROPE_THETA = 500000.0
CMP_BLOCK = 32
CMP_STRIDE = 16
SLC_BLOCK = 64
SLC_TOP = 16
WINDOW = 512
PAGE_SIZE = 128
POOL_WINDOWS = (2, 4, 8, 16)
POOL_PREFIX = 16
CONV_WIDTH = 31
CONV_PREFIX = 32
SM_SCALE = HEAD_DIM ** -0.5
LOG2E = 1.4426950408889634
NEG = -1e30
RANK_UNROLL = 4
CHUNK_PITCH = 24
LANES = 128
GATE_COLS = 128
GATE_ROWS = 16
VMEM_LIMIT = 56 * 1024 * 1024


def _cparams(sem):
    return pltpu.CompilerParams(dimension_semantics=sem, vmem_limit_bytes=VMEM_LIMIT)


def _const_spec(shape):
    n = len(shape)
    return pl.BlockSpec(shape, lambda *a: (0,) * n, pipeline_mode=pl.Buffered(1))


def _nt(a, b):
    return lax.dot_general(a, b, (((1,), (1,)), ((), ())), preferred_element_type=F32)


def _mm(a, b):
    return jnp.dot(a, b, preferred_element_type=F32)


def _rms(x, g):
    return x * lax.rsqrt(jnp.mean(x * x, axis=-1, keepdims=True) + 1e-6) * g


def _rope128(slab, c, s1, s2):
    return slab * c + pltpu.roll(slab, LANES - ROPE_HALF, 1) * s1 + pltpu.roll(slab, ROPE_HALF, 1) * s2


def _proj_sample_kernel(x_ref, g_ref, w_ref, rope_ref, u_ref, q_ref, qr_ref, kvc_ref, kvs_ref, kvw_ref,
                        gn_ref, a_ref, gm_ref):
    x = x_ref[...]
    hb = _rms(x, g_ref[...]).astype(BF16)

    def mm(c0, n):
        return _mm(hb, w_ref[:, c0:c0 + n])

    c = rope_ref[:, 0:128]
    s1 = rope_ref[:, 128:256]
    s2 = rope_ref[:, 256:384]
    u_ref[...] = mm(0, 512)
    q = mm(512, 1024)
    q_ref[...] = (q * SM_SCALE).astype(BF16)
    for i in range(8):
        sl = slice(i * 128, (i + 1) * 128)
        qr_ref[:, sl] = (_rope128(q[:, sl], c, s1, s2) * SM_SCALE).astype(BF16)
    kv = mm(1536, 1536)
    kvc_ref[...] = kv[:, 0:512]
    for off, o_ref in ((512, kvs_ref), (1024, kvw_ref)):
        for i in range(2):
            o_ref[:, i * 128:(i + 1) * 128] = _rope128(kv[:, off + i * 128:off + (i + 1) * 128], c, s1, s2)
        o_ref[:, 256:512] = kv[:, off + 256:off + 512]
    gz = jax.nn.sigmoid(mm(3072, GATE_COLS))
    gn_ref[0] = gz
    for k in range(1, KV_HEADS):
        gn_ref[k] = pltpu.roll(gz, LANES - 3 * GROUP * k, 1)
    ci = mm(3072 + GATE_COLS, 1024)
    a_ref[...] = ci[:, :512] * jax.nn.sigmoid(ci[:, 512:])
    for i in range(3):
        gm_ref[:, i * 1024:(i + 1) * 1024] = jax.nn.sigmoid(mm(4096 + GATE_COLS + i * 1024, 1024))


def _proj_sample(x2, g, w, rope):
    m = x2.shape[0]
    outs = [((m, 512), F32), ((m, 1024), BF16), ((m, 1024), BF16), ((m, 512), F32), ((m, 512), F32),
            ((m, 512), F32), ((KV_HEADS, m, 128), F32), ((m, 512), F32), ((m, 3072), F32)]
    full = lambda shp: pl.BlockSpec(shp, lambda i: (0,) * len(shp))
    return pl.pallas_call(
        _proj_sample_kernel,
        grid=(1,),
        in_specs=[full(x2.shape), _const_spec(g.shape), _const_spec(w.shape), full(rope.shape)],
        out_specs=[full(s) for s, _ in outs],
        out_shape=[jax.ShapeDtypeStruct(s, d) for s, d in outs],
        compiler_params=_cparams(("arbitrary",)),
        name="proj_sample",
    )(x2, g, w, rope)


def _rope_t(z, cos, sin, n_heads):
    out = []
    for h in range(n_heads):
        x1 = z[h * 64:h * 64 + ROPE_HALF]
        x2 = z[h * 64 + ROPE_HALF:h * 64 + ROPE_DIM]
        out += [x1 * cos - x2 * sin, x2 * cos + x1 * sin, z[h * 64 + ROPE_DIM:(h + 1) * 64]]
    return jnp.concatenate(out, axis=0)


def _proj_prompt_kernel(x_ref, g_ref, w_ref, wt_ref, rope_ref, u_ref, a_ref, gm_ref, kvc_ref, ks_ref, kw_ref,
                        qt_ref, qrt_ref, kvct_ref, kvst_ref, kvwt_ref, vst_ref, vwt_ref, gt_ref):
    hb = _rms(x_ref[...], g_ref[...]).astype(BF16)

    def mm(c0, n):
        return _mm(hb, w_ref[:, c0:c0 + n])

    def mt(r0, n):
        return _nt(wt_ref[r0:r0 + n, :], hb)

    u_ref[...] = mm(0, 512)
    kvc_ref[...] = mm(512, 512)
    ci = mm(1024, 1024)
    a_ref[...] = ci[:, :512] * jax.nn.sigmoid(ci[:, 512:])
    for i in range(3):
        gm_ref[:, i * 1024:(i + 1) * 1024] = jax.nn.sigmoid(mm(2048 + i * 1024, 1024))
    cos = rope_ref[0:ROPE_HALF, :]
    sin = rope_ref[ROPE_HALF:ROPE_DIM, :]
    qt = mt(0, 1024) * (SM_SCALE * LOG2E)
    qt_ref[...] = qt.astype(BF16)
    qrt_ref[...] = _rope_t(qt, cos, sin, N_HEADS).astype(BF16)
    kvct_ref[...] = mt(1024, 512)
    for off, kvt_ref, vt_ref, k_ref in ((1536, kvst_ref, vst_ref, ks_ref), (2048, kvwt_ref, vwt_ref, kw_ref)):
        kt = _rope_t(mt(off, 256), cos, sin, KV_HEADS)
        vt = mt(off + 256, 256)
        kvt_ref[0:256, :] = kt
        kvt_ref[256:512, :] = vt
        vt_ref[...] = vt.astype(BF16)
        k_ref[...] = kt.T.astype(BF16)
    gt_ref[...] = jax.nn.sigmoid(mt(2560, KV_HEADS * GATE_ROWS))


def _proj_prompt(x2, g, w, wt, rope_t, bsz, seq, tm):
    m = x2.shape[0]
    nt = seq // tm
    row = lambda c: pl.BlockSpec((tm, c), lambda i: (i, 0))
    fm = lambda r: pl.BlockSpec((r, tm), lambda i: (i // nt, i % nt))
    outs = [((m, 512), F32, row(512)), ((m, 512), F32, row(512)), ((m, 3072), F32, row(3072)),
            ((m, 512), F32, row(512)), ((m, 256), BF16, row(256)), ((m, 256), BF16, row(256)),
            ((bsz * 1024, seq), BF16, fm(1024)), ((bsz * 1024, seq), BF16, fm(1024)),
            ((bsz * 512, seq), F32, fm(512)), ((bsz * 512, seq), F32, fm(512)), ((bsz * 512, seq), F32, fm(512)),
            ((bsz * 256, seq), BF16, fm(256)), ((bsz * 256, seq), BF16, fm(256)),
            ((bsz * KV_HEADS * GATE_ROWS, seq), F32, fm(KV_HEADS * GATE_ROWS))]
    return pl.pallas_call(
        _proj_prompt_kernel,
        grid=(m // tm,),
        in_specs=[row(1024), _const_spec(g.shape), _const_spec(w.shape), _const_spec(wt.shape),
                  pl.BlockSpec((ROPE_DIM, tm), lambda i: (0, i % nt))],
        out_specs=[o[2] for o in outs],
        out_shape=[jax.ShapeDtypeStruct(o[0], o[1]) for o in outs],
        compiler_params=_cparams(("parallel",)),
        name="proj_prompt",
    )(x2, g, w, wt, rope_t)


def _cmp_lohi_matmuls(load, w_ref, o_ref):
    for c in range(2):
        for pr in range(2):
            slab = c * 2 + pr
            lhs = jnp.concatenate([load(slab, s) for s in range(CMP_STRIDE)], axis=1)
            o_ref[:, slab * 512:(slab + 1) * 512] = _mm(lhs.astype(BF16), w_ref[c])


def _cmp_lohi_dense_kernel(x_ref, w_ref, o_ref, *, nch):
    _cmp_lohi_matmuls(lambda slab, s: x_ref[pl.ds(4 * s + slab, nch, stride=4 * CMP_STRIDE), :], w_ref, o_ref)


def _cmp_lohi_dense(x2, w1p, ntok):
    m = x2.shape[0]
    nch = ntok // CMP_STRIDE
    return pl.pallas_call(
        functools.partial(_cmp_lohi_dense_kernel, nch=nch),
        grid=(m // ntok,),
        in_specs=[pl.BlockSpec((ntok * 4, 128), lambda i: (i, 0)), _const_spec(w1p.shape)],
        out_specs=pl.BlockSpec((nch, 2048), lambda i: (i, 0)),
        out_shape=jax.ShapeDtypeStruct((m // CMP_STRIDE, 2048), F32),
        compiler_params=_cparams(("parallel",)),
        name="cmp_lohi",
    )(x2.reshape(m * 4, 128), w1p)


def _cmp_lohi_paged_kernel(pt_ref, *refs, n_pages):
    del pt_ref
    pages = refs[:n_pages]
    w_ref, o_ref, x_scr = refs[n_pages:]
    cpp = PAGE_SIZE // CMP_STRIDE
    for i, p in enumerate(pages):
        for c in range(2):
            xt = p[c].reshape(KV_HEADS * HEAD_DIM, PAGE_SIZE).T
            for pr in range(2):
                for n in range(cpp):
                    r0 = (i * cpp + n) * CHUNK_PITCH
                    x_scr[c * 2 + pr, r0:r0 + CMP_STRIDE, :] = xt[n * CMP_STRIDE:(n + 1) * CMP_STRIDE,
                                                                  pr * 128:(pr + 1) * 128]
    nch = n_pages * cpp
    _cmp_lohi_matmuls(lambda slab, s: x_scr[slab, pl.ds(s, nch, stride=CHUNK_PITCH), :], w_ref, o_ref)


def _cmp_lohi_paged(cache_t, layer, page_table, w1p, pg):
    b, n_pages = page_table.shape
    nch = pg * (PAGE_SIZE // CMP_STRIDE)

    def page_spec(i):
        return pl.BlockSpec((None, None, 2, KV_HEADS, HEAD_DIM, PAGE_SIZE),
                            lambda bb, s, pt: (layer, pt[bb, s * pg + i], 0, 0, 0, 0))

    return pl.pallas_call(
        functools.partial(_cmp_lohi_paged_kernel, n_pages=pg),
        grid_spec=pltpu.PrefetchScalarGridSpec(
            num_scalar_prefetch=1,
            grid=(b, n_pages // pg),
            in_specs=[page_spec(i) for i in range(pg)] + [_const_spec(w1p.shape)],
            out_specs=pl.BlockSpec((None, nch, 2048), lambda bb, s, pt: (bb, s, 0)),
            scratch_shapes=[pltpu.VMEM((4, nch * CHUNK_PITCH, 128), F32)],
        ),
        out_shape=jax.ShapeDtypeStruct((b, n_pages * (PAGE_SIZE // CMP_STRIDE), 2048), F32),
        compiler_params=_cparams(("parallel", "parallel")),
        name="cmp_lohi_paged",
    )(page_table, *([cache_t] * pg), w1p)


def _cmp_fin_kernel(lohi_ref, nxt_ref, bias_ref, w2_ref, o_ref, vt_ref, *, nch):
    row = lax.broadcasted_iota(jnp.int32, (nch, 128), 0)
    for c in range(2):
        for pr in range(2):
            base = (c * 2 + pr) * 512
            hs = []
            for kk in range(2):
                lo = lohi_ref[:, base + kk * 256:base + kk * 256 + 128]
                hi = lohi_ref[:, base + kk * 256 + 128:base + kk * 256 + 256]
                nx = nxt_ref[:, base + kk * 256 + 128:base + kk * 256 + 256]
                hin = jnp.where(row == nch - 1, nx, pltpu.roll(hi, nch - 1, 0))
                hs.append(jax.nn.gelu(lo + hin + bias_ref[c:c + 1, :]))
            hid = jnp.concatenate(hs, axis=1).astype(BF16)
            out = _mm(hid, w2_ref[c])
            o_ref[:, c * 256 + pr * 128:c * 256 + (pr + 1) * 128] = out.astype(BF16)
            if c == 1:
                vt_ref[pr * 128:(pr + 1) * 128, :] = out.T.astype(BF16)


def _cmp_fin(lohi, nxt, bias, w2p):
    b, nch, _ = lohi.shape
    return pl.pallas_call(
        functools.partial(_cmp_fin_kernel, nch=nch),
        grid=(b,),
        in_specs=[pl.BlockSpec((None, nch, 2048), lambda i: (i, 0, 0)),
                  pl.BlockSpec((None, 1, 2048), lambda i: (i, 0, 0)),
                  _const_spec(bias.shape), _const_spec(w2p.shape)],
        out_specs=[pl.BlockSpec((None, nch, 512), lambda i: (i, 0, 0)),
                   pl.BlockSpec((None, 256, nch), lambda i: (i, 0, 0))],
        out_shape=[jax.ShapeDtypeStruct((b, nch, 512), BF16), jax.ShapeDtypeStruct((b, 256, nch), BF16)],
        compiler_params=_cparams(("parallel",)),
        name="cmp_fin",
    )(lohi, nxt, bias, w2p)


def _split3(x):
    x1 = x.astype(BF16)
    r1 = x - x1.astype(F32)
    x2 = r1.astype(BF16)
    x3 = (r1 - x2.astype(F32)).astype(BF16)
    return x1, x2, x3


def _block_rank_select(imp_ref, n_blocks, valid):
    x = imp_ref[...]
    jrow = lax.broadcasted_iota(jnp.int32, x.shape, 0)
    assert x.shape[0] % RANK_UNROLL == 0

    def body(i, cnt):
        for u in range(RANK_UNROLL):
            jp = i * RANK_UNROLL + u
            r = imp_ref[pl.ds(jp, 1), :]
            cnt = cnt + jnp.where(jrow > jp, jnp.where(r >= x, 1.0, 0.0), jnp.where(r > x, 1.0, 0.0))
        return cnt

    cnt = lax.fori_loop(0, (n_blocks + RANK_UNROLL - 1) // RANK_UNROLL, body, jnp.zeros(x.shape, F32))
    return jnp.where((cnt < SLC_TOP) & valid, 1.0, 0.0)


def _update_t(m_ref, acc_ref, sl, s, v_ext):
    m_old = m_ref[:, sl]
    m_new = jnp.maximum(m_old, s.max(axis=0, keepdims=True))
    alpha = jnp.exp2(m_old - m_new)
    p = jnp.exp2(s - m_new).astype(BF16)
    acc_ref[:, sl] = acc_ref[:, sl] * alpha + _mm(v_ext, p)
    m_ref[:, sl] = m_new


def _finish_t(acc_ref, sl):
    return acc_ref[0:128, sl] * (1.0 / jnp.maximum(acc_ref[128:129, sl], 1e-30))


def _attn_prompt_kernel(qt_ref, qrt_ref, gt_ref, kc_ref, vct_ref, ks_ref, vst_ref, kw_ref, vwt_ref, ct_ref, et_ref,
                        o_ref, imp_scr, lim_scr, qa_scr, oc_scr, m_scr, acc_scr, mw_scr, accw_scr, sa_scr, sb_scr, sw_scr,
                        *, tq, nch, n_slc, kc_sel, kc_win):
    k = pl.program_id(1)
    t0 = pl.program_id(2) * tq
    par = k % 2
    r = GROUP * tq

    def stack(ref):
        z = jnp.zeros((HEAD_DIM, tq), BF16)
        cols = []
        for g in range(GROUP):
            qg = ref[g * HEAD_DIM:(g + 1) * HEAD_DIM, :]
            cols.append(jnp.where(par == 0, jnp.concatenate([qg, z], axis=0), jnp.concatenate([z, qg], axis=0)))
        return jnp.concatenate(cols, axis=1)

    q4 = stack(qt_ref)
    qr4 = stack(qrt_ref)
    pos = t0 + lax.broadcasted_iota(jnp.int32, (1, tq), 1)

    vis = lax.broadcasted_iota(jnp.int32, (nch, tq), 0) <= ((pos - (CMP_BLOCK - 1)) >> 4)
    kc = kc_ref[...]
    vct = vct_ref[...]
    p_sum = None
    s_c = _mm(kc, q4)
    es, invs = [], []
    for g in range(GROUP):
        sg = jnp.where(vis, s_c[:, g * tq:(g + 1) * tq], NEG)
        e = jnp.where(vis, jnp.exp2(sg - sg.max(axis=0, keepdims=True)), 0.0)
        inv = 1.0 / jnp.maximum(e.sum(axis=0, keepdims=True), 1e-30)
        p_sum = e * inv if p_sum is None else p_sum + e * inv
        es.append(e.astype(BF16))
        invs.append(inv)
    oc_scr[...] = _mm(vct, jnp.concatenate(es, axis=1)) * jnp.concatenate(invs, axis=1)

    ct = ct_ref[...]
    imp = sum(_mm(ct, piece) for piece in _split3(p_sum))
    nb = imp.shape[0]
    jb = lax.broadcasted_iota(jnp.int32, (nb, tq), 0)
    cur = pos >> 6
    forced = (jb == 0) | (jb == cur) | (jb == cur - 1)
    valid = jb <= cur
    posb = jnp.broadcast_to(pos, (nb, tq))
    imp_scr[...] = jnp.where(valid, jnp.where(forced, jnp.inf, imp), -jnp.inf)

    @pl.when(t0 + tq <= SLC_TOP * SLC_BLOCK)
    def _():
        lim_scr[...] = jnp.where(valid, posb, -1)

    @pl.when(t0 + tq > SLC_TOP * SLC_BLOCK)
    def _():
        sel = _block_rank_select(imp_scr, jnp.minimum(n_slc, (t0 + tq - 1) // SLC_BLOCK + 1), valid)
        lim_scr[...] = jnp.where(sel > 0.5, posb, -1)

    bias = jnp.where(lim_scr[...] >= 0, 0.0, NEG).astype(BF16)
    bias = jnp.concatenate([bias, jnp.zeros((128 - nb, tq), BF16)], axis=0)
    qa_scr[0:128, :] = qr4
    qa_scr[128:256, :] = jnp.concatenate([bias] * GROUP, axis=1)
    m_scr[...] = jnp.full((1, r), NEG, F32)
    acc_scr[...] = jnp.zeros(acc_scr.shape, F32)
    ones_s = jnp.ones((16, kc_sel), BF16)

    def scores(c):
        k0 = pl.multiple_of(c * kc_sel, kc_sel)
        k_aug = jnp.concatenate([ks_ref[pl.ds(k0, kc_sel), :], et_ref[pl.ds(k0, kc_sel), :]], axis=1)
        return _mm(k_aug, qa_scr[...])

    def soft_pv(s_ref, c, causal):
        k0 = pl.multiple_of(c * kc_sel, kc_sel)
        s = s_ref[...]
        if causal:
            al = (k0 + lax.broadcasted_iota(jnp.int32, (kc_sel, tq), 0)) <= pos
            s = jnp.concatenate([jnp.where(al, s[:, g * tq:(g + 1) * tq], NEG) for g in range(GROUP)], axis=1)
        _update_t(m_scr, acc_scr, slice(None), s, jnp.concatenate([vst_ref[:, pl.ds(k0, kc_sel)], ones_s], axis=0))

    c_last = t0 // kc_sel
    sa_scr[...] = scores(0)

    def pair_body(i, carry):
        sb_scr[...] = scores(2 * i + 1)
        soft_pv(sa_scr, 2 * i, False)
        sa_scr[...] = scores(2 * i + 2)
        soft_pv(sb_scr, 2 * i + 1, False)
        return carry

    lax.fori_loop(0, c_last // 2, pair_body, 0)

    k0w = pl.multiple_of(jnp.maximum(t0 - WINDOW, 0), tq)

    def window_scores():
        sw_scr[...] = _mm(kw_ref[pl.ds(k0w, kc_win), :], qa_scr[0:128, :])

    @pl.when(c_last % 2 == 0)
    def _():
        window_scores()
        soft_pv(sa_scr, c_last, True)

    @pl.when(c_last % 2 == 1)
    def _():
        sb_scr[...] = scores(c_last)
        soft_pv(sa_scr, c_last - 1, False)
        window_scores()
        soft_pv(sb_scr, c_last, True)

    mw_scr[...] = jnp.full((1, r), NEG, F32)
    accw_scr[...] = jnp.zeros(accw_scr.shape, F32)
    dp = pos - (k0w + lax.broadcasted_iota(jnp.int32, (kc_win, tq), 0))
    al = lax.bitcast_convert_type(dp, jnp.uint32) < WINDOW
    s = sw_scr[...]
    s = jnp.concatenate([jnp.where(al, s[:, g * tq:(g + 1) * tq], NEG) for g in range(GROUP)], axis=1)
    _update_t(mw_scr, accw_scr, slice(None), s,
              jnp.concatenate([vwt_ref[:, pl.ds(k0w, kc_win)], jnp.ones((16, kc_win), BF16)], axis=0))

    gt = gt_ref[...]
    for g in range(GROUP):
        sl = slice(g * tq, (g + 1) * tq)
        og = gt[g:g + 1] * oc_scr[:, sl] + gt[GROUP + g:GROUP + g + 1] * _finish_t(acc_scr, sl) \
            + gt[2 * GROUP + g:2 * GROUP + g + 1] * _finish_t(accw_scr, sl)
        o_ref[g * HEAD_DIM:(g + 1) * HEAD_DIM, :] = jnp.where(par == 0, og[0:HEAD_DIM], og[HEAD_DIM:]).astype(BF16)


def _attn_prompt(qt, qrt, gt, kvc, vct, ks, vst, kw, vwt, ct, et, bsz, seq, tq, kc_sel, kc_win):
    nch = kvc.shape[1]
    nt = seq // tq
    n_slc = seq // SLC_BLOCK
    nb = ct.shape[0]
    r = GROUP * tq
    qspec = pl.BlockSpec((256, tq), lambda b, k, t: (b * KV_HEADS + k, t))
    kspec = pl.BlockSpec((seq, 128), lambda b, k, t: (b, k // 2))
    vspec = pl.BlockSpec((128, seq), lambda b, k, t: (b * 2 + k // 2, 0))
    return pl.pallas_call(
        functools.partial(_attn_prompt_kernel, tq=tq, nch=nch, n_slc=n_slc, kc_sel=kc_sel, kc_win=kc_win),
        grid=(bsz, KV_HEADS, nt),
        in_specs=[qspec, qspec,
                  pl.BlockSpec((GATE_ROWS, tq), lambda b, k, t: (b * KV_HEADS + k, t)),
                  pl.BlockSpec((None, nch, 128), lambda b, k, t: (b, 0, k // 2)),
                  pl.BlockSpec((None, 128, nch), lambda b, k, t: (b, k // 2, 0)),
                  kspec, vspec, kspec, vspec, _const_spec(ct.shape), _const_spec(et.shape)],
        out_specs=pl.BlockSpec((256, tq), lambda b, k, t: (b * KV_HEADS + k, t)),
        out_shape=jax.ShapeDtypeStruct((bsz * 1024, seq), BF16),
        scratch_shapes=[pltpu.VMEM((nb, tq), F32), pltpu.VMEM((nb, tq), jnp.int32), pltpu.VMEM((256, r), BF16),
                        pltpu.VMEM((128, r), F32),
                        pltpu.VMEM((1, r), F32), pltpu.VMEM((128 + 16, r), F32),
                        pltpu.VMEM((1, r), F32), pltpu.VMEM((128 + 16, r), F32),
                        pltpu.VMEM((kc_sel, r), F32), pltpu.VMEM((kc_sel, r), F32), pltpu.VMEM((kc_win, r), F32)],
        compiler_params=_cparams(("parallel", "parallel", "arbitrary")),
        name="attn_prompt",
    )(qt, qrt, gt, kvc, vct, ks, vst, kw, vwt, ct, et)


def _pad_q(qf, par, g, lane):
    slab = qf[:, (g // 2) * 128:(g // 2 + 1) * 128]
    rolled = pltpu.roll(slab, 64, 1)
    src = slab if par == g % 2 else rolled
    return jnp.where((lane >> 6) == par, src, 0.0)


def _stack_q(q_blk, par):
    qf = q_blk.astype(F32)
    lane = lax.broadcasted_iota(jnp.int32, (qf.shape[0], 128), 1)
    return jnp.concatenate([_pad_q(qf, par, g, lane) for g in range(GROUP)], axis=0).astype(BF16)


def _softmax_parts(parts, r):
    ss = []
    for s, al, _ in parts:
        ss.append(jnp.where(al[None], s.reshape(GROUP, r, s.shape[-1]), NEG))
    m = ss[0].max(axis=-1, keepdims=True)
    for s in ss[1:]:
        m = jnp.maximum(m, s.max(axis=-1, keepdims=True))
    l = jnp.zeros_like(m)
    acc = jnp.zeros((GROUP, r, 128), F32)
    ps = []
    for (_, al, pv), s in zip(parts, ss):
        p = jnp.where(al[None], jnp.exp(s - m), 0.0)
        ps.append(p)
        l = l + p.sum(axis=-1, keepdims=True)
        acc = acc + pv(p.reshape(GROUP * r, p.shape[-1]).astype(BF16)).reshape(GROUP, r, 128)
    inv = 1.0 / jnp.maximum(l, 1e-30)
    return acc * inv, [p * inv for p in ps]


def _online_update(m_ref, l_ref, acc_ref, s, allowed, pv, r):
    n = s.shape[-1]
    s3 = jnp.where(allowed[None], s.reshape(GROUP, r, n), NEG)
    m_old = m_ref[...]
    m_new = jnp.maximum(m_old, s3.max(axis=-1, keepdims=True))
    alpha = jnp.exp(m_old - m_new)
    p = jnp.where(allowed[None], jnp.exp(s3 - m_new), 0.0)
    l_ref[...] = alpha * l_ref[...] + p.sum(axis=-1, keepdims=True)
    acc_ref[...] = alpha * acc_ref[...] + pv(p.reshape(GROUP * r, n).astype(BF16)).reshape(GROUP, r, 128)
    m_ref[...] = m_new


def _place_heads(o4, par, r):
    lane = lax.broadcasted_iota(jnp.int32, (r, 128), 1)
    outs = []
    for pr in range(2):
        lo, hi = o4[2 * pr], o4[2 * pr + 1]
        lo = lo if par == 0 else pltpu.roll(lo, 64, 1)
        hi = hi if par == 1 else pltpu.roll(hi, 64, 1)
        outs.append(jnp.where(lane < 64, lo, hi))
    return jnp.concatenate(outs, axis=1)


def _gate(gn, br, r):
    return jnp.stack([gn[:, br * GROUP + g:br * GROUP + g + 1] for g in range(GROUP)], axis=0)


def _pair_t(ref, c, pr):
    return ref[c, 2 * pr:2 * pr + 2].reshape(2 * HEAD_DIM, ref.shape[-1]).astype(BF16)


def _samp_a_kernel(q_ref, qr_ref, gn_ref, kvc_ref, win_ref, kvwn_ref, ct_ref, o_ref, sel_ref, imp_scr,
                   *, t, nch, n_slc, past, wb):
    nb = ct_ref.shape[0]
    p_sums = []
    pos_c = past + lax.broadcasted_iota(jnp.int32, (t, nch), 0)
    cend = lax.broadcasted_iota(jnp.int32, (t, nch), 1) * CMP_STRIDE + (CMP_BLOCK - 1)
    tt = lax.broadcasted_iota(jnp.int32, (t, wb), 0)
    ii = lax.broadcasted_iota(jnp.int32, (t, wb), 1)
    dp_buf = wb + tt - ii
    al_buf = (dp_buf >= 0) & (dp_buf < WINDOW) & (past - wb + ii >= 0)
    nn = kvwn_ref.shape[0]
    dp_new = lax.broadcasted_iota(jnp.int32, (t, nn), 0) - lax.broadcasted_iota(jnp.int32, (t, nn), 1)
    al_new = (dp_new >= 0) & (dp_new < WINDOW)
    for k in range(KV_HEADS):
        par, pr = k % 2, k // 2
        q4 = _stack_q(q_ref[:, k * 256:(k + 1) * 256], par)
        qr4 = _stack_q(qr_ref[:, k * 256:(k + 1) * 256], par)
        ksl = slice(pr * 128, (pr + 1) * 128)
        vsl = slice(256 + pr * 128, 256 + (pr + 1) * 128)
        kc, vc = kvc_ref[:, ksl], kvc_ref[:, vsl]
        o_cmp, (p_cmp,) = _softmax_parts([(_nt(q4, kc), cend <= pos_c, lambda p, vc=vc: _mm(p, vc))], t)
        p_sums.append(p_cmp.sum(axis=0))
        kwt, vwt = _pair_t(win_ref, 0, pr), _pair_t(win_ref, 1, pr)
        kn, vn = kvwn_ref[:, ksl].astype(BF16), kvwn_ref[:, vsl].astype(BF16)
        o_win, _ = _softmax_parts([(_mm(qr4, kwt), al_buf, lambda p, vwt=vwt: _nt(p, vwt)),
                                   (_nt(qr4, kn), al_new, lambda p, vn=vn: _mm(p, vn))], t)
        gn = gn_ref[k]
        o4 = _gate(gn, 0, t) * o_cmp + _gate(gn, 2, t) * o_win
        o_ref[:, k * 256:(k + 1) * 256] = _place_heads(o4, par, t)
    p_all = jnp.concatenate(p_sums + [jnp.zeros((128 - KV_HEADS * t, nch), F32)], axis=0)
    ct = ct_ref[...]
    imp = sum(_nt(ct, piece) for piece in _split3(p_all))
    jb = lax.broadcasted_iota(jnp.int32, (nb, 128), 0)
    col = lax.broadcasted_iota(jnp.int32, (nb, 128), 1)
    cur = (past + (col & (t - 1))) >> 6
    forced = (jb == 0) | (jb == cur) | (jb == cur - 1)
    valid = jb <= cur
    imp_scr[...] = jnp.where(valid, jnp.where(forced, jnp.inf, imp), -jnp.inf)
    sel_ref[...] = _block_rank_select(imp_scr, n_slc, valid)


def _samp_a(q3, qr3, gn4, kvc, win_t, layer, kvwn3, ct, past):
    b, t, _ = q3.shape
    nch = kvc.shape[1]
    nb = ct.shape[0]
    wb = win_t.shape[-1]
    n_slc = (past + t + SLC_BLOCK - 1) // SLC_BLOCK
    per_b = lambda r, c: pl.BlockSpec((None, r, c), lambda i: (i, 0, 0))
    return pl.pallas_call(
        functools.partial(_samp_a_kernel, t=t, nch=nch, n_slc=n_slc, past=past, wb=wb),
        grid=(b,),
        in_specs=[per_b(t, 1024), per_b(t, 1024),
                  pl.BlockSpec((KV_HEADS, None, t, 128), lambda i: (0, i, 0, 0)),
                  per_b(nch, 512),
                  pl.BlockSpec((None, None, 2, KV_HEADS, HEAD_DIM, wb), lambda i: (layer, i, 0, 0, 0, 0)),
                  per_b(kvwn3.shape[1], 512), _const_spec(ct.shape)],
        out_specs=[per_b(t, 1024), per_b(nb, 128)],
        out_shape=[jax.ShapeDtypeStruct((b, t, 1024), F32), jax.ShapeDtypeStruct((b, nb, 128), F32)],
        scratch_shapes=[pltpu.VMEM((nb, 128), F32)],
        compiler_params=_cparams(("parallel",)),
        name="samp_a",
    )(q3, qr3, gn4, kvc, win_t, kvwn3, ct)


def _samp_b_kernel(pt_ref, *refs, pg, t):
    del pt_ref
    pages = refs[:pg]
    qr_ref, sel_ref, seln_ref, e_ref, gn_ref, kvn_ref, op_ref, o_ref, m_scr, l_scr, acc_scr, s_scr = refs[pg:]
    s = pl.program_id(1)
    ns = pl.num_programs(1)
    nbc = sel_ref.shape[0]

    @pl.when(s == 0)
    def _():
        m_scr[...] = jnp.full(m_scr.shape, NEG, F32)
        l_scr[...] = jnp.zeros(l_scr.shape, F32)
        acc_scr[...] = jnp.zeros(acc_scr.shape, F32)

    selc = jnp.concatenate([sel_ref[...], jnp.zeros((128 - nbc, 128), F32)], axis=0).T
    msk_all = _mm(selc[:KV_HEADS * t].astype(BF16), e_ref[...])
    for pr in range(2):
        q_pair = jnp.concatenate([_stack_q(qr_ref[:, k * 256:(k + 1) * 256], k % 2) for k in (2 * pr, 2 * pr + 1)],
                                 axis=0)
        s_scr[pr] = _mm(q_pair, jnp.concatenate([_pair_t(p, 0, pr) for p in pages], axis=1))
    rows = GROUP * t
    for pr in range(2):
        vt = jnp.concatenate([_pair_t(p, 1, pr) for p in pages], axis=1)
        ps, alphas = [], []
        for hh in range(2):
            k = 2 * pr + hh
            al = (msk_all[k * t:(k + 1) * t] > 0.5)[None]
            s3 = jnp.where(al, s_scr[pr, hh * rows:(hh + 1) * rows, :].reshape(GROUP, t, -1), NEG)
            m_old = m_scr[k]
            m_new = jnp.maximum(m_old, s3.max(axis=-1, keepdims=True))
            alpha = jnp.exp(m_old - m_new)
            p = jnp.where(al, jnp.exp(s3 - m_new), 0.0)
            l_scr[k] = alpha * l_scr[k] + p.sum(axis=-1, keepdims=True)
            m_scr[k] = m_new
            ps.append(p.reshape(rows, -1).astype(BF16))
            alphas.append(alpha)
        pv = _nt(jnp.concatenate(ps, axis=0), vt)
        for hh in range(2):
            k = 2 * pr + hh
            acc_scr[k] = alphas[hh] * acc_scr[k] + pv[hh * rows:(hh + 1) * rows].reshape(GROUP, t, 128)

    @pl.when(s == ns - 1)
    def _():
        nn = kvn_ref.shape[0]
        dp = lax.broadcasted_iota(jnp.int32, (t, nn), 0) - lax.broadcasted_iota(jnp.int32, (t, nn), 1)
        seln = jnp.concatenate([seln_ref[...], jnp.zeros((128 - 8, 128), F32)], axis=0).T
        for k in range(KV_HEADS):
            par, pr = k % 2, k // 2
            qr4 = _stack_q(qr_ref[:, k * 256:(k + 1) * 256], par)
            kk = kvn_ref[:, pr * 128:(pr + 1) * 128].astype(BF16)
            vv = kvn_ref[:, 256 + pr * 128:256 + (pr + 1) * 128].astype(BF16)
            al = (dp >= 0) & (seln[k * t:(k + 1) * t, 0:1] > 0.5)
            _online_update(m_scr.at[k], l_scr.at[k], acc_scr.at[k], _nt(qr4, kk), al, lambda p, vv=vv: _mm(p, vv), t)
            o_sel = acc_scr[k] * (1.0 / jnp.maximum(l_scr[k], 1e-30))
            o4 = _gate(gn_ref[k], 1, t) * o_sel
            o_ref[:, k * 256:(k + 1) * 256] = (op_ref[:, k * 256:(k + 1) * 256] + _place_heads(o4, par, t)).astype(BF16)


def _samp_b(cache_t, layer, page_table, qr3, sel, e, gn4, kvn3, o_part, pg):
    b, n_pages = page_table.shape
    t = qr3.shape[1]
    ns = n_pages // pg
    nbc = pg * (PAGE_SIZE // SLC_BLOCK)
    n_past_blk = n_pages * (PAGE_SIZE // SLC_BLOCK)

    def page_spec(i):
        return pl.BlockSpec((None, None, 2, KV_HEADS, HEAD_DIM, PAGE_SIZE),
                            lambda bb, s, pt: (layer, pt[bb, s * pg + i], 0, 0, 0, 0))

    per_b = lambda r, c: pl.BlockSpec((None, r, c), lambda bb, s, pt: (bb, 0, 0))
    return pl.pallas_call(
        functools.partial(_samp_b_kernel, pg=pg, t=t),
        grid_spec=pltpu.PrefetchScalarGridSpec(
            num_scalar_prefetch=1,
            grid=(b, ns),
            in_specs=[page_spec(i) for i in range(pg)] + [
                per_b(t, 1024),
                pl.BlockSpec((None, nbc, 128), lambda bb, s, pt: (bb, s, 0)),
                pl.BlockSpec((None, 8, 128), lambda bb, s, pt: (bb, n_past_blk // 8, 0)),
                _const_spec(e.shape),
                pl.BlockSpec((KV_HEADS, None, t, 128), lambda bb, s, pt: (0, bb, 0, 0)),
                per_b(kvn3.shape[1], 512), per_b(t, 1024)],
            out_specs=per_b(t, 1024),
            scratch_shapes=[pltpu.VMEM((KV_HEADS, GROUP, t, 1), F32), pltpu.VMEM((KV_HEADS, GROUP, t, 1), F32),
                            pltpu.VMEM((KV_HEADS, GROUP, t, 128), F32),
                            pltpu.VMEM((2, 2 * GROUP * t, pg * PAGE_SIZE), F32)],
        ),
        out_shape=jax.ShapeDtypeStruct((b, t, 1024), BF16),
        compiler_params=_cparams(("parallel", "arbitrary")),
        name="samp_b",
    )(page_table, *([cache_t] * pg), qr3, sel, sel, e, gn4, kvn3, o_part)


def _mixpre_kernel(u_ref, up_ref, a_ref, ap_ref, cw_ref, cb_ref, lg_ref, lb_ref, pool_ref, cact_ref,
                   u_scr, a_scr, ph_scr, *, tm, tiles_per_seq, pos_base, has_state):
    ti = pl.program_id(0) % tiles_per_seq
    pos0 = pos_base + ti * tm
    keep = 1.0 if has_state else jnp.where(ti == 0, 0.0, 1.0)
    u = u_ref[...]
    u_scr[0:POOL_PREFIX, :] = up_ref[...] * keep
    u_scr[POOL_PREFIX:, :] = u
    a_scr[0:CONV_PREFIX, :] = ap_ref[...] * keep
    a_scr[CONV_PREFIX:, :] = a_ref[...]
    pos = pos0 + lax.broadcasted_iota(jnp.int32, (tm, 128), 0)
    for gi, w in enumerate(POOL_WINDOWS):
        sl = slice(gi * 128, (gi + 1) * 128)
        acc = u[:, sl]
        for d in range(1, w):
            acc = acc + u_scr[pl.ds(POOL_PREFIX - d, tm), sl]
        cnt = jnp.minimum(pos + 1, w).astype(F32)
        pool_ref[:, sl] = (acc / cnt - u[:, sl]).astype(BF16)
    y = jnp.zeros((tm, 512), F32) + cb_ref[...]
    base = CONV_PREFIX - (CONV_WIDTH - 1)
    span = tm + CONV_PREFIX - 8
    for ph in range(8):
        offs = [o for o in range(base, base + CONV_WIDTH) if o % 8 == ph]
        if ph:
            ph_scr[0:span, :] = a_scr[pl.ds(ph, span), :]
        for o in offs:
            src = a_scr[pl.ds(o, tm), :] if ph == 0 else ph_scr[pl.ds(o - ph, tm), :]
            y = y + src * cw_ref[o - base:o - base + 1, :]
    mu = jnp.mean(y, axis=-1, keepdims=True)
    yc = y - mu
    var = jnp.mean(yc * yc, axis=-1, keepdims=True)
    yn = yc * lax.rsqrt(var + 1e-5) * lg_ref[...] + lb_ref[...]
    cact_ref[...] = (yn * jax.nn.sigmoid(yn)).astype(BF16)


def _mixpre(u, u_state, a, a_state, cw, cb, lg, lb, tm, tiles_per_seq, pos_base):
    has_state = u_state is not None
    if has_state:
        up, ap = u_state, a_state
        n_tiles = u.shape[0]
        up_spec = pl.BlockSpec((None, POOL_PREFIX, 512), lambda i: (i, 0, 0))
        ap_spec = pl.BlockSpec((None, CONV_PREFIX, 512), lambda i: (i, 0, 0))
        row = pl.BlockSpec((None, tm, 512), lambda i: (i, 0, 0))
        out_shape = (n_tiles, tm, 512)
    else:
        up, ap = u, a
        n_tiles = u.shape[0] // tm
        up_spec = pl.BlockSpec((POOL_PREFIX, 512), lambda i: (jnp.maximum(i * (tm // POOL_PREFIX) - 1, 0), 0))
        ap_spec = pl.BlockSpec((CONV_PREFIX, 512), lambda i: (jnp.maximum(i * (tm // CONV_PREFIX) - 1, 0), 0))
        row = pl.BlockSpec((tm, 512), lambda i: (i, 0))
        out_shape = (u.shape[0], 512)
    return pl.pallas_call(
        functools.partial(_mixpre_kernel, tm=tm, tiles_per_seq=tiles_per_seq, pos_base=pos_base, has_state=has_state),
        grid=(n_tiles,),
        in_specs=[row, up_spec, row, ap_spec, _const_spec(cw.shape), _const_spec(cb.shape),
                  _const_spec(lg.shape), _const_spec(lb.shape)],
        out_specs=[row, row],
        out_shape=[jax.ShapeDtypeStruct(out_shape, BF16), jax.ShapeDtypeStruct(out_shape, BF16)],
        scratch_shapes=[pltpu.VMEM((tm + POOL_PREFIX, 512), F32), pltpu.VMEM((tm + CONV_PREFIX, 512), F32),
                        pltpu.VMEM((tm + CONV_PREFIX, 512), F32)],
        compiler_params=_cparams(("parallel",)),
        name="mixpre",
    )(u, up, a, ap, cw, cb, lg, lb)


def _dense_kernel(x_ref, pool_ref, onsa_ref, cact_ref, gm_ref, pw_ref, ps_ref, wbp_ref, wbn_ref, wbc_ref, wo_ref,
                  nf_ref, wg_ref, wu_ref, wd_ref, nfin_ref, o_ref, *, final, ff_chunk, onsa_t):
    op = jnp.concatenate([_mm(pool_ref[:, g * 128:(g + 1) * 128], pw_ref[g]) for g in range(4)], axis=1)
    op = (op * ps_ref[...]).astype(BF16)
    onsa = onsa_ref[...].astype(F32).T.astype(BF16) if onsa_t else onsa_ref[...]
    m = gm_ref[:, 0:1024] * _mm(op, wbp_ref[...])
    m = m + gm_ref[:, 1024:2048] * _mm(onsa, wbn_ref[...])
    m = m + gm_ref[:, 2048:3072] * _mm(cact_ref[...], wbc_ref[...])
    x1 = x_ref[...] + _mm(m.astype(BF16), wo_ref[...])
    hb = _rms(x1, nf_ref[...]).astype(BF16)
    dff = wg_ref.shape[1]
    acc = x1
    for c0 in range(0, dff, ff_chunk):
        gt = _mm(hb, wg_ref[:, c0:c0 + ff_chunk])
        up = _mm(hb, wu_ref[:, c0:c0 + ff_chunk])
        acc = acc + _mm((gt * jax.nn.sigmoid(gt) * up).astype(BF16), wd_ref[c0:c0 + ff_chunk, :])
    o_ref[...] = _rms(acc, nfin_ref[...]) if final else acc


def _dense(x2, pooled, onsa, cact, gm, weights, tm, final, seq=None):
    m = x2.shape[0]
    row = lambda c: pl.BlockSpec((tm, c), lambda i: (i, 0))
    if seq is None:
        onsa_spec = row(1024)
    else:
        nt = seq // tm
        onsa_spec = pl.BlockSpec((1024, tm), lambda i: (i // nt, i % nt))
    dff = weights[8].shape[1]
    ff_chunk = dff // 2 if (dff // 2) % 128 == 0 else dff
    return pl.pallas_call(
        functools.partial(_dense_kernel, final=final, ff_chunk=ff_chunk, onsa_t=seq is not None),
        grid=(m // tm,),
        in_specs=[row(1024), row(512), onsa_spec, row(512), row(3072)] + [_const_spec(w.shape) for w in weights],
        out_specs=row(1024),
        out_shape=jax.ShapeDtypeStruct((m, 1024), F32),
        compiler_params=_cparams(("parallel",)),
        name="dense",
    )(x2, pooled, onsa, cact, gm, *weights)


def _rope_angles(pos):
    inv = ROPE_THETA ** (-jnp.arange(ROPE_HALF, dtype=F32) * (2.0 / ROPE_DIM))
    return pos.astype(F32)[:, None] * inv[None, :]


def _rope_table(pos):
    ang = _rope_angles(pos)
    cos, sin = jnp.cos(ang), jnp.sin(ang)
    t = pos.shape[0]
    zeros = jnp.zeros((t, HEAD_DIM - ROPE_DIM), F32)
    zh = jnp.zeros((t, ROPE_HALF), F32)
    c = jnp.concatenate([cos, cos, zeros + 1.0], axis=1)
    s1 = jnp.concatenate([-sin, zh, zeros], axis=1)
    s2 = jnp.concatenate([zh, sin, zeros], axis=1)
    return jnp.concatenate([c, c, s1, s1, s2, s2], axis=1)


def _rope_table_t(pos):
    ang = _rope_angles(pos)
    return jnp.concatenate([jnp.cos(ang).T, jnp.sin(ang).T], axis=0)


def _cmp_to_slc_t(n_cmp_rows, nb):
    i0 = jnp.arange(n_cmp_rows)[None, :] * CMP_STRIDE
    j0 = jnp.arange(nb)[:, None] * SLC_BLOCK
    ov = jnp.clip(jnp.minimum(i0 + CMP_BLOCK, j0 + SLC_BLOCK) - jnp.maximum(i0, j0), 0, None)
    return (ov.astype(F32) / CMP_BLOCK).astype(BF16)


def _expand_matrix(n_rows, n_keys):
    return (jnp.arange(n_rows)[:, None] == jnp.arange(n_keys)[None, :] // SLC_BLOCK).astype(BF16)


def _layer_weights(l, norm_mix, w_in, pool_w, pool_scale, cmp_w1, cmp_pe, cmp_w2, conv_w, conv_b, conv_ln_g,
                   conv_ln_b, w_br_pool, w_br_nsa, w_br_conv, w_out, norm_ffn, w_gate, w_up, w_down, norm_final):
    w = w_in[l]
    s_q, s_kv, s_gate = 512, 512 + 1024, 512 + 1024 + 1536
    n_g = 3 * N_HEADS
    order = [(k, br, g) for k in range(KV_HEADS) for br in range(3) for g in range(GROUP)]
    idx = jnp.array([s_gate + (k * GROUP + g) * 3 + br for k, br, g in order])
    w_s = jnp.concatenate([w[:, :s_gate], w[:, idx], jnp.zeros((w.shape[0], GATE_COLS - n_g), w.dtype),
                           w[:, s_gate + n_g:]], axis=1).astype(BF16)
    w_rm = jnp.concatenate([w[:, :s_q], w[:, s_kv:s_kv + 512], w[:, s_gate + n_g:]], axis=1).astype(BF16)
    gate_rows = jnp.zeros((KV_HEADS * GATE_ROWS, w.shape[0]), w.dtype)
    gate_rows = gate_rows.at[jnp.array([k * GATE_ROWS + br * GROUP + g for k, br, g in order])].set(w[:, idx].T)
    w_fm = jnp.concatenate([w[:, s_q:s_gate].T, gate_rows], axis=0).astype(BF16)
    w1 = cmp_w1[l]
    eye2 = jnp.eye(2, dtype=F32)
    w1h = w1.reshape(2, 2, CMP_STRIDE, HEAD_DIM, 128)
    w1p = jnp.einsum('chsde,kq->cskdqhe', w1h, eye2).reshape(2, CMP_STRIDE * 128, 512).astype(BF16)
    w2p = jnp.einsum('ced,kq->ckeqd', cmp_w2[l], eye2).reshape(2, 256, 128).astype(BF16)
    bias = jnp.einsum('cpd,cpde->ce', cmp_pe[l], w1, precision=lax.Precision.HIGHEST)
    cw = jnp.concatenate([conv_w[l], jnp.zeros((1, conv_w.shape[2]), F32)], axis=0)
    dense_w = (pool_w[l].astype(BF16), pool_scale[l][None], w_br_pool[l].astype(BF16), w_br_nsa[l].astype(BF16),
               w_br_conv[l].astype(BF16), w_out[l].astype(BF16), norm_ffn[l][None], w_gate[l].astype(BF16),
               w_up[l].astype(BF16), w_down[l].astype(BF16), norm_final[None])
    return dict(g=norm_mix[l][None], w_s=w_s, w_rm=w_rm, w_fm=w_fm, w1p=w1p, w2p=w2p, bias=bias, cw=cw,
                cb=conv_b[l][None], lg=conv_ln_g[l][None], lb=conv_ln_b[l][None], dense=dense_w)


def kernel(x_prompt, x_sample, cache_cmp_kv, cache_slc_kv, state_win_kv, state_pool, state_conv, page_table,
           norm_mix, w_in, pool_w, pool_scale, cmp_w1, cmp_pe, cmp_w2, conv_w, conv_b, conv_ln_g, conv_ln_b,
           w_br_pool, w_br_nsa, w_br_conv, w_out, norm_ffn, w_gate, w_up, w_down, norm_final):
    bp, sp, d = x_prompt.shape
    bs, ts, _ = x_sample.shape
    depth = w_in.shape[0]
    n_pages = page_table.shape[1]
    past = n_pages * PAGE_SIZE
    wb = state_win_kv.shape[2]
    assert d == 1024 and sp % 512 == 0 and sp >= WINDOW + 256 and ts == 8 and n_pages % 4 == 0

    tm = 256
    tq = 256
    kc_sel = 512
    kc_win = WINDOW + tq
    pg = 16 if n_pages % 16 == 0 else 4
    pg_cmp = 32 if n_pages % 32 == 0 else pg
    mp, ms = bp * sp, bs * ts

    rope_pt = _rope_table_t(jnp.arange(sp))
    rope_s = jnp.tile(_rope_table(past + jnp.arange(ts)), (bs, 1))
    nch_p = sp // CMP_STRIDE
    nb_p = ((sp // SLC_BLOCK + 7) // 8) * 8
    ct_p = _cmp_to_slc_t(nch_p, nb_p)
    nch_s = past // CMP_STRIDE
    n_slc_s = (past + ts + SLC_BLOCK - 1) // SLC_BLOCK
    nb_s = ((n_slc_s + 7) // 8) * 8
    ct_s = _cmp_to_slc_t(nch_s, nb_s)
    e_s = _expand_matrix(128, pg * PAGE_SIZE)
    et_p = _expand_matrix(128, sp).T

    fm = lambda z: jnp.transpose(z, (0, 1, 3, 4, 5, 2))
    cache_cmp_t, cache_slc_t, win_t = fm(cache_cmp_kv), fm(cache_slc_kv), fm(state_win_kv)

    xp = x_prompt.reshape(mp, d)
    xs = x_sample.reshape(ms, d)
    outs = {n: [] for n in ("cmp_p", "cmp_s", "slc_p", "slc_s", "win_p", "win_s", "pool_p", "pool_s", "conv_p", "conv_s")}
    kvshape = (2, KV_HEADS, HEAD_DIM)
    tok_major = lambda z: jnp.transpose(z.reshape(bp, *kvshape, z.shape[-1]), (0, 4, 1, 2, 3))
    for l in range(depth):
        lw = _layer_weights(l, norm_mix, w_in, pool_w, pool_scale, cmp_w1, cmp_pe, cmp_w2, conv_w, conv_b, conv_ln_g,
                            conv_ln_b, w_br_pool, w_br_nsa, w_br_conv, w_out, norm_ffn, w_gate, w_up, w_down, norm_final)
        final = l == depth - 1

        (u, a, gm, kvc, ks, kw, qt, qrt, kvct, kvst, kvwt, vst, vwt, gt) = _proj_prompt(
            xp, lw["g"], lw["w_rm"], lw["w_fm"], rope_pt, bp, sp, tm)
        lohi = _cmp_lohi_dense(kvc, lw["w1p"], sp).reshape(bp, nch_p, 2048)
        kvcb, vct = _cmp_fin(lohi, jnp.zeros((bp, 1, 2048), F32), lw["bias"], lw["w2p"])
        onsa_t = _attn_prompt(qt, qrt, gt, kvcb, vct, ks, vst, kw, vwt, ct_p, et_p, bp, sp, tq, kc_sel, kc_win)
        pooled, cact = _mixpre(u, None, a, None, lw["cw"], lw["cb"], lw["lg"], lw["lb"], tm, sp // tm, 0)
        xp = _dense(xp, pooled, onsa_t, cact, gm, lw["dense"], tm, final, seq=sp)
        outs["cmp_p"].append(tok_major(kvct))
        outs["slc_p"].append(tok_major(kvst))
        outs["win_p"].append(tok_major(kvwt[:, -min(WINDOW, sp):]))
        outs["pool_p"].append(u.reshape(bp, sp, 512)[:, -(POOL_PREFIX - 1):])
        outs["conv_p"].append(a.reshape(bp, sp, 512)[:, -(CONV_WIDTH - 1):])

        u, q, qr, kvc, kvs, kvw, gn, a, gm = _proj_sample(xs, lw["g"], lw["w_s"], rope_s)
        pad8 = lambda z: jnp.concatenate([z.reshape(bs, ts, 512), jnp.zeros((bs, 16 - ts, 512), F32)], axis=1)
        lohi = _cmp_lohi_paged(cache_cmp_t, l, page_table, lw["w1p"], pg_cmp)
        lohi_new = _cmp_lohi_dense(pad8(kvc).reshape(bs * 16, 512), lw["w1p"], bs * 16)
        kvcb, _ = _cmp_fin(lohi, lohi_new.reshape(bs, 1, 2048), lw["bias"], lw["w2p"])
        q3, qr3 = q.reshape(bs, ts, 1024), qr.reshape(bs, ts, 1024)
        gn4 = gn.reshape(KV_HEADS, bs, ts, 128)
        o_part, sel = _samp_a(q3, qr3, gn4, kvcb, win_t, l, pad8(kvw), ct_s, past)
        onsa = _samp_b(cache_slc_t, l, page_table, qr3, sel, e_s, gn4, pad8(kvs), o_part, pg)
        u_state = jnp.concatenate([jnp.zeros((bs, 1, 512), F32), state_pool[l]], axis=1)
        a_state = jnp.concatenate([jnp.zeros((bs, 2, 512), F32), state_conv[l]], axis=1)
        pooled, cact = _mixpre(u.reshape(bs, ts, 512), u_state, a.reshape(bs, ts, 512), a_state,
                               lw["cw"], lw["cb"], lw["lg"], lw["lb"], ts, 1, past)
        xs = _dense(xs, pooled.reshape(ms, 512), onsa.reshape(ms, 1024), cact.reshape(ms, 512), gm, lw["dense"],
                    ms, final)
        outs["cmp_s"].append(kvc.reshape(bs, ts, *kvshape))
        outs["slc_s"].append(kvs.reshape(bs, ts, *kvshape))
        kvw_t = jnp.transpose(kvw.reshape(bs, ts, *kvshape), (0, 2, 3, 4, 1))
        win_new_t = jnp.concatenate([win_t[l], kvw_t], axis=-1)[..., -wb:]
        outs["win_s"].append(jnp.transpose(win_new_t, (0, 4, 1, 2, 3)))
        outs["pool_s"].append(jnp.concatenate([state_pool[l], u.reshape(bs, ts, 512)], axis=1)[:, -(POOL_PREFIX - 1):])
        outs["conv_s"].append(jnp.concatenate([state_conv[l], a.reshape(bs, ts, 512)], axis=1)[:, -(CONV_WIDTH - 1):])

    st = lambda n: jnp.stack(outs[n])
    return (xp.reshape(bp, sp, d), xs.reshape(bs, ts, d), st("cmp_p"), st("cmp_s"), st("slc_p"), st("slc_s"),
            st("win_p"), st("win_s"), st("pool_p"), st("pool_s"), st("conv_p"), st("conv_s"))
```
